```python
import math
import jax
import jax.numpy as jnp
from jax import lax
import numpy as np

D_MODEL = 1024
BATCH = 2
SEQ = 8192
DEPTH = 4

CTX_LEN = 256
GRID_W = 64
EPS = 1e-6
ROPE_BASE = 10000.0
Q_BLOCK = 128
N_MOD = 6
F32 = jnp.float32

GROUP_WIDTH = D_MODEL // 4
MIX_WIDTH = 4 * GROUP_WIDTH

MLA_V = 64
MLA_HEADS = GROUP_WIDTH // MLA_V
MLA_NOPE = 64
MLA_ROPE = 32
MLA_QK = MLA_NOPE + MLA_ROPE
MLA_KV_RANK = 128

SSD_HEAD_DIM = 64
SSD_INNER = GROUP_WIDTH
SSD_HEADS = SSD_INNER // SSD_HEAD_DIM
SSD_GROUPS = 2
SSD_STATE = 64
SSD_CONV = 3
SSD_CHUNK = 128
SSD_GN = SSD_GROUPS * SSD_STATE
SSD_XBC = SSD_INNER + 2 * SSD_GN

S5_WIDTH = GROUP_WIDTH
S5_GROUP = 16
S5_GROUPS = S5_WIDTH // S5_GROUP
S5_STATE = 64

DIFF_V = 64
DIFF_HEADS = GROUP_WIDTH // DIFF_V
DIFF_HEAD = DIFF_V // 2

MLA_COLS = MLA_HEADS * MLA_QK + MLA_KV_RANK + MLA_ROPE
SSD_COLS = SSD_INNER + SSD_XBC + 2 * SSD_HEADS
S5_COLS = S5_WIDTH
DIFF_COLS = 2 * DIFF_HEADS * 2 * DIFF_HEAD + DIFF_HEADS * DIFF_V
IN_COLS = MLA_COLS + SSD_COLS + S5_COLS + DIFF_COLS

MOE_GROUPS = 4
MOE_PER_GROUP = 8
N_EXPERTS = MOE_GROUPS * MOE_PER_GROUP
MOE_TOP_K = 2
EXPERT_HIDDEN = D_MODEL // 4

kernel_name = 'hybrid_mla_ssd_s5_diffattn_hmoe_dit'


def rmsnorm(x, g):
    xf = x.astype(F32)
    y = xf * lax.rsqrt(jnp.mean(xf * xf, axis=-1, keepdims=True) + EPS)
    return (y * g.astype(F32)).astype(x.dtype)


def modulate(x, g, shift, scale):
    return rmsnorm(x, g) * (1.0 + scale) + shift


def _angles(pos, dim):
    inv = ROPE_BASE ** (-jnp.arange(0, dim, 2, dtype=F32) / dim)
    return pos.astype(F32)[:, None] * inv[None, :]


def _rotate_half(x, ang):
    m = ang.shape[-1]
    cos = jnp.cos(ang)[None, :, None, :].astype(x.dtype)
    sin = jnp.sin(ang)[None, :, None, :].astype(x.dtype)
    x1, x2 = x[..., :m], x[..., m:]
    return jnp.concatenate([x1 * cos - x2 * sin, x1 * sin + x2 * cos], axis=-1)


def axial_rope(x, rows, cols):
    half = x.shape[-1] // 2
    return jnp.concatenate([_rotate_half(x[..., :half], _angles(rows, half)),
                            _rotate_half(x[..., half:], _angles(cols, half))], axis=-1)


def blocked_attention(q, k, v, scale):
    b, lq, h, dk = q.shape
    nb = lq // Q_BLOCK
    qb = q.reshape(b, nb, Q_BLOCK, h, dk).transpose(1, 0, 2, 3, 4)

    def one(qblk):
        s = jnp.einsum('bqhd,bkhd->bhqk', qblk, k).astype(F32) * scale
        p = jax.nn.softmax(s, axis=-1)
        return jnp.einsum('bhqk,bkhd->bqhd', p.astype(v.dtype), v)

    o = lax.map(one, qb)
    return o.transpose(1, 0, 2, 3, 4).reshape(b, lq, h, v.shape[-1])


def blocked_diff_attention(q1, q2, k1, k2, v, lam, scale):
    b, lq, h, dk = q1.shape
    nb = lq // Q_BLOCK
    to_blocks = lambda t: t.reshape(b, nb, Q_BLOCK, h, dk).transpose(1, 0, 2, 3, 4)

    def one(qs):
        a1, a2 = qs
        p1 = jax.nn.softmax(jnp.einsum('bqhd,bkhd->bhqk', a1, k1).astype(F32) * scale, axis=-1)
        p2 = jax.nn.softmax(jnp.einsum('bqhd,bkhd->bhqk', a2, k2).astype(F32) * scale, axis=-1)
        return jnp.einsum('bhqk,bkhd->bqhd', (p1 - lam * p2).astype(v.dtype), v)

    o = lax.map(one, (to_blocks(q1), to_blocks(q2)))
    return o.transpose(1, 0, 2, 3, 4).reshape(b, lq, h, v.shape[-1])


def centred_depthwise_conv(x, w, b):
    k = w.shape[0]
    y = lax.conv_general_dilated(x, w[:, None, :].astype(x.dtype), window_strides=(1,),
                                 padding=[(k // 2, k - 1 - k // 2)],
                                 dimension_numbers=('NWC', 'WIO', 'NWC'),
                                 feature_group_count=x.shape[-1])
    return y + b


def split_groups(p):
    o = 0
    parts = []
    for n in (MLA_COLS, SSD_COLS, S5_COLS, DIFF_COLS):
        parts.append(p[..., o:o + n])
        o += n
    return parts


def mla_mixer(p_c, p_l, rows, cols, kv_norm, w_uk, w_uv, q_norm, k_norm, need_ctx):
    nq = MLA_HEADS * MLA_QK

    def project(p):
        b, l, _ = p.shape
        q = rmsnorm(p[..., :nq].reshape(b, l, MLA_HEADS, MLA_QK), q_norm)
        ckv = rmsnorm(p[..., nq:nq + MLA_KV_RANK], kv_norm)
        k_rope = jnp.broadcast_to(p[..., nq + MLA_KV_RANK:][:, :, None, :], (b, l, MLA_HEADS, MLA_ROPE))
        k_nope = (ckv @ w_uk).reshape(b, l, MLA_HEADS, MLA_NOPE)
        v = (ckv @ w_uv).reshape(b, l, MLA_HEADS, MLA_V)
        k = rmsnorm(jnp.concatenate([k_nope, k_rope], axis=-1), k_norm)
        return q, k, v

    def rope_tail(t):
        return jnp.concatenate([t[..., :MLA_NOPE], axial_rope(t[..., MLA_NOPE:], rows, cols)], axis=-1)

    qc, kc, vc = project(p_c)
    ql, kl, vl = project(p_l)
    ql, kl = rope_tail(ql), rope_tail(kl)
    scale = MLA_QK ** -0.5
    b, l = p_l.shape[:2]
    o_l = blocked_attention(ql, jnp.concatenate([kc, kl], 1), jnp.concatenate([vc, vl], 1), scale)
    o_l = o_l.reshape(b, l, MLA_HEADS * MLA_V)
    o_c = None
    if need_ctx:
        o_c = blocked_attention(qc, kc, vc, scale).reshape(b, p_c.shape[1], MLA_HEADS * MLA_V)
    return o_c, o_l


def ssd_scan(x, dt, a, bm, cm, h0):
    b, l, h, p = x.shape
    n = bm.shape[-1]
    nc = l // SSD_CHUNK
    xs = (x * dt[..., None]).reshape(b, nc, SSD_CHUNK, h, p)
    bc = bm.reshape(b, nc, SSD_CHUNK, h, n)
    cc = cm.reshape(b, nc, SSD_CHUNK, h, n)
    a_cum = jnp.cumsum((dt * a).reshape(b, nc, SSD_CHUNK, h), axis=2)
    mask = jnp.tril(jnp.ones((SSD_CHUNK, SSD_CHUNK), dtype=bool))[None, None, :, :, None]
    seg = a_cum[:, :, :, None, :] - a_cum[:, :, None, :, :]
    decay = jnp.exp(jnp.where(mask, seg, -jnp.inf))
    scores = jnp.einsum('bcihn,bcjhn->bcijh', cc, bc) * decay
    y_diag = jnp.einsum('bcijh,bcjhp->bcihp', scores, xs)
    decay_to_end = jnp.exp(a_cum[:, :, -1:, :] - a_cum)
    states = jnp.einsum('bclhn,bclhp->bchpn', bc * decay_to_end[..., None], xs)
    chunk_decay = jnp.exp(a_cum[:, :, -1, :])

    def step(hs, inp):
        dec, st = inp
        return hs * dec[:, :, None, None] + st, hs

    h_final, h_prev = lax.scan(step, h0.astype(F32),
                               (chunk_decay.transpose(1, 0, 2), states.transpose(1, 0, 2, 3, 4)))
    h_prev = h_prev.transpose(1, 0, 2, 3, 4)
    y_off = jnp.einsum('bclhn,bchpn->bclhp', cc * jnp.exp(a_cum)[..., None], h_prev)
    return (y_diag + y_off).reshape(b, l, h, p), h_final


def _ssd_prep(p, conv_w, conv_b, dt_bias):
    b, l, _ = p.shape
    z = p[..., :SSD_INNER]
    xbc = jax.nn.silu(centred_depthwise_conv(p[..., SSD_INNER:SSD_INNER + SSD_XBC], conv_w, conv_b))
    xs = xbc[..., :SSD_INNER].reshape(b, l, SSD_HEADS, SSD_HEAD_DIM)
    rep = SSD_HEADS // SSD_GROUPS
    bm = jnp.repeat(xbc[..., SSD_INNER:SSD_INNER + SSD_GN].reshape(b, l, SSD_GROUPS, SSD_STATE), rep, axis=2)
    cm = jnp.repeat(xbc[..., SSD_INNER + SSD_GN:].reshape(b, l, SSD_GROUPS, SSD_STATE), rep, axis=2)
    dt = jax.nn.softplus(p[..., SSD_INNER + SSD_XBC:].astype(F32).reshape(b, l, 2, SSD_HEADS)
                         + dt_bias.astype(F32))
    return z, xs, bm, cm, dt


def _ssd_out(y, xs, z, d_skip, norm_g):
    b, l = z.shape[:2]
    y = y + xs * d_skip[:, None]
    y = y.reshape(b, l, SSD_INNER) * jax.nn.silu(z)
    return rmsnorm(y, norm_g).astype(z.dtype)


def ssd_mixer(p_c, p_l, conv_w, conv_b, a_log, dt_bias, d_skip, norm_g, need_ctx):
    zc, xc, bc, cc, dtc = _ssd_prep(p_c, conv_w, conv_b, dt_bias)
    zl, xl, bl, cl, dtl = _ssd_prep(p_l, conv_w, conv_b, dt_bias)
    a = -jnp.exp(a_log.astype(F32))
    flip = lambda t: jnp.flip(t, axis=1)
    h0 = jnp.zeros((p_l.shape[0], SSD_HEADS, SSD_HEAD_DIM, SSD_STATE), F32)
    yc_f, hc_f = ssd_scan(xc, dtc[:, :, 0], a[0], bc, cc, h0)
    yl_f, _ = ssd_scan(xl, dtl[:, :, 0], a[0], bl, cl, hc_f)
    yc_b, hc_b = ssd_scan(flip(xc), flip(dtc[:, :, 1]), a[1], flip(bc), flip(cc), h0)
    yl_b, _ = ssd_scan(flip(xl), flip(dtl[:, :, 1]), a[1], flip(bl), flip(cl), hc_b)
    o_l = _ssd_out(yl_f + flip(yl_b), xl, zl, d_skip, norm_g)
    o_c = _ssd_out(yc_f + flip(yc_b), xc, zc, d_skip, norm_g) if need_ctx else None
    return o_c, o_l


def s5_discretise(lam_re, lam_im, log_step, b_re, b_im):
    step = jnp.exp(log_step)[:, None]
    er = jnp.exp(lam_re * step)
    ar = er * jnp.cos(lam_im * step)
    ai = er * jnp.sin(lam_im * step)
    nr, ni = ar - 1.0, ai
    den = lam_re * lam_re + lam_im * lam_im
    fr = (nr * lam_re + ni * lam_im) / den
    fi = (ni * lam_re - nr * lam_im) / den
    br = fr[..., None] * b_re - fi[..., None] * b_im
    bi = fr[..., None] * b_im + fi[..., None] * b_re
    return ar, ai, br, bi


def s5_scan(u, ar, ai, br, bi, s0_re, s0_im):
    l = u.shape[0]
    bu_re = jnp.einsum('gpj,lbgj->lbgp', br, u)
    bu_im = jnp.einsum('gpj,lbgj->lbgp', bi, u)
    bu_re = bu_re.at[0].add(ar * s0_re - ai * s0_im)
    bu_im = bu_im.at[0].add(ar * s0_im + ai * s0_re)
    a_re = jnp.broadcast_to(ar[None, None], (l, 1) + ar.shape)
    a_im = jnp.broadcast_to(ai[None, None], (l, 1) + ai.shape)

    def combine(e1, e2):
        a1r, a1i, b1r, b1i = e1
        a2r, a2i, b2r, b2i = e2
        return (a2r * a1r - a2i * a1i, a2r * a1i + a2i * a1r,
                a2r * b1r - a2i * b1i + b2r, a2r * b1i + a2i * b1r + b2i)

    _, _, s_re, s_im = lax.associative_scan(combine, (a_re, a_im, bu_re, bu_im), axis=0)
    return s_re, s_im


def s5_readout(s_re, s_im, c_re, c_im):
    return jnp.einsum('gjp,lbgp->lbgj', c_re, s_re) - jnp.einsum('gjp,lbgp->lbgj', c_im, s_im)


def s5_mixer(u_c, u_l, lam_re, lam_im, log_step, b_re, b_im, c_re, c_im, d_skip, w_glu, b_glu, need_ctx):
    lam_re, lam_im, log_step = lam_re.astype(F32), lam_im.astype(F32), log_step.astype(F32)
    b_re, b_im, c_re, c_im = b_re.astype(F32), b_im.astype(F32), c_re.astype(F32), c_im.astype(F32)

    def to_lbgj(u):
        b, l, _ = u.shape
        return u.astype(F32).reshape(b, l, S5_GROUPS, S5_GROUP).transpose(1, 0, 2, 3)

    uc, ul = to_lbgj(u_c), to_lbgj(u_l)
    zero = jnp.zeros((u_l.shape[0], S5_GROUPS, S5_STATE), F32)
    yc = jnp.zeros_like(uc)
    yl = jnp.zeros_like(ul)
    for d in range(2):
        rev = (lambda t: t[::-1]) if d == 1 else (lambda t: t)
        ar, ai, br, bi = s5_discretise(lam_re[d], lam_im[d], log_step[d], b_re, b_im)
        sc_re, sc_im = s5_scan(rev(uc), ar, ai, br, bi, zero, zero)
        sl_re, sl_im = s5_scan(rev(ul), ar, ai, br, bi, sc_re[-1], sc_im[-1])
        yl = yl + rev(s5_readout(sl_re, sl_im, c_re[d], c_im[d]))
        if need_ctx:
            yc = yc + rev(s5_readout(sc_re, sc_im, c_re[d], c_im[d]))

    def finish(y, u_in):
        b, l, _ = u_in.shape
        y = y.transpose(1, 0, 2, 3).reshape(b, l, S5_WIDTH) + d_skip * u_in.astype(F32)
        g = jax.nn.gelu(y)
        return (g * jax.nn.sigmoid(g @ w_glu + b_glu)).astype(u_in.dtype)

    o_l = finish(yl, u_l)
    o_c = finish(yc, u_c) if need_ctx else None
    return o_c, o_l


def diff_mixer(p_c, p_l, rows, cols, q_norm, k_norm, lq1, lk1, lq2, lk2, subln, lam_init, need_ctx):
    nq = DIFF_HEADS * 2 * DIFF_HEAD

    def project(p):
        b, l, _ = p.shape
        q = rmsnorm(p[..., :nq].reshape(b, l, 2 * DIFF_HEADS, DIFF_HEAD), q_norm)
        k = rmsnorm(p[..., nq:2 * nq].reshape(b, l, 2 * DIFF_HEADS, DIFF_HEAD), k_norm)
        v = p[..., 2 * nq:].reshape(b, l, DIFF_HEADS, DIFF_V)
        return q, k, v

    def pair(t):
        b, l = t.shape[:2]
        t = t.reshape(b, l, DIFF_HEADS, 2, DIFF_HEAD)
        return t[:, :, :, 0], t[:, :, :, 1]

    qc, kc, vc = project(p_c)
    ql, kl, vl = project(p_l)
    ql, kl = axial_rope(ql, rows, cols), axial_rope(kl, rows, cols)
    lam = (jnp.exp(jnp.sum(lq1.astype(F32) * lk1.astype(F32)))
           - jnp.exp(jnp.sum(lq2.astype(F32) * lk2.astype(F32))) + lam_init)
    scale = DIFF_HEAD ** -0.5

    def attend(q, k, v):
        q1, q2 = pair(q)
        k1, k2 = pair(k)
        o = blocked_diff_attention(q1, q2, k1, k2, v, lam, scale)
        b, l = o.shape[:2]
        return (rmsnorm(o, subln) * (1.0 - lam_init)).reshape(b, l, DIFF_HEADS * DIFF_V)

    o_l = attend(ql, jnp.concatenate([kc, kl], 1), jnp.concatenate([vc, vl], 1))
    o_c = attend(qc, kc, vc) if need_ctx else None
    return o_c, o_l


def hier_moe(h, w_group, b_group, w_expert, b_expert, w_gate, w_up, w_down):
    t = h.shape[0]
    lg = (h @ w_group + b_group).astype(F32)
    g_idx = jnp.argmax(lg, axis=-1)
    g_hot = jax.nn.one_hot(g_idx, MOE_GROUPS, dtype=F32)
    p_group = jnp.sum(jax.nn.softmax(lg, axis=-1) * g_hot, axis=-1, keepdims=True)
    le = (jnp.einsum('td,gde->tge', h, w_expert) + b_expert).astype(F32)
    le = jnp.einsum('tg,tge->te', g_hot, le)
    top_v, top_i = lax.top_k(le, MOE_TOP_K)
    w_top = jax.nn.softmax(top_v, axis=-1) * p_group
    inner = jnp.einsum('tk,tke->te', w_top, jax.nn.one_hot(top_i, MOE_PER_GROUP, dtype=F32))
    gates = (g_hot[:, :, None] * inner[:, None, :]).reshape(t, N_EXPERTS).astype(h.dtype)
    out = jnp.zeros_like(h)
    for e in range(N_EXPERTS):
        hid = jax.nn.silu(h @ w_gate[e]) * (h @ w_up[e])
        out = out + gates[:, e:e + 1] * (hid @ w_down[e])
    return out


def setup_inputs(seed: int = 0) -> dict:
    key = jax.random.key(seed)
    keys = list(jax.random.split(key, 64))
    nk = lambda: keys.pop()
    nrm = lambda shape, s: jax.random.normal(nk(), shape, F32) * s
    gain = lambda shape: 1.0 + 0.05 * jax.random.normal(nk(), shape, F32)
    unif = lambda shape, lo, hi: jax.random.uniform(nk(), shape, F32, lo, hi)
    dm = D_MODEL
    inp = {}
    inp['x'] = nrm((BATCH, SEQ, dm), 1.0)
    inp['c'] = nrm((BATCH, dm), 1.0)
    inp['ctx'] = nrm((BATCH, CTX_LEN, dm), 1.0)
    inp['c_ctx'] = nrm((dm,), 1.0)
    inp['w_ada'] = nrm((DEPTH, dm, N_MOD * dm), 0.5 * dm ** -0.5)
    inp['b_ada'] = nrm((DEPTH, N_MOD * dm), 0.02)
    inp['norm1'] = gain((DEPTH, dm))
    inp['norm2'] = gain((DEPTH, dm))
    inp['w_in'] = nrm((DEPTH, dm, IN_COLS), dm ** -0.5)
    inp['w_out'] = nrm((DEPTH, MIX_WIDTH, dm), MIX_WIDTH ** -0.5)
    inp['mla_kv_norm'] = gain((DEPTH, MLA_KV_RANK))
    inp['mla_w_uk'] = nrm((DEPTH, MLA_KV_RANK, MLA_HEADS * MLA_NOPE), MLA_KV_RANK ** -0.5)
    inp['mla_w_uv'] = nrm((DEPTH, MLA_KV_RANK, MLA_HEADS * MLA_V), MLA_KV_RANK ** -0.5)
    inp['mla_q_norm'] = gain((DEPTH, MLA_QK))
    inp['mla_k_norm'] = gain((DEPTH, MLA_QK))
    inp['ssd_conv_w'] = nrm((DEPTH, SSD_CONV, SSD_XBC), SSD_CONV ** -0.5)
    inp['ssd_conv_b'] = nrm((DEPTH, SSD_XBC), 0.02)
    inp['ssd_a_log'] = jnp.log(unif((DEPTH, 2, SSD_HEADS), 1.0, 16.0))
    dt0 = jnp.exp(unif((DEPTH, 2, SSD_HEADS), math.log(1e-3), math.log(1e-1)))
    inp['ssd_dt_bias'] = dt0 + jnp.log(-jnp.expm1(-dt0))
    inp['ssd_d'] = gain((DEPTH, SSD_HEADS))
    inp['ssd_norm'] = gain((DEPTH, SSD_INNER))
    n_idx = jnp.arange(S5_STATE, dtype=F32)
    inp['s5_lam_re'] = -0.5 + 0.01 * jax.random.normal(nk(), (DEPTH, 2, S5_GROUPS, S5_STATE), F32)
    inp['s5_lam_im'] = math.pi * n_idx + 0.01 * jax.random.normal(nk(), (DEPTH, 2, S5_GROUPS, S5_STATE), F32)
    inp['s5_log_step'] = unif((DEPTH, 2, S5_GROUPS), math.log(1e-3), math.log(1e-1))
    inp['s5_b_re'] = nrm((DEPTH, S5_GROUPS, S5_STATE, S5_GROUP), (2 * S5_GROUP) ** -0.5)
    inp['s5_b_im'] = nrm((DEPTH, S5_GROUPS, S5_STATE, S5_GROUP), (2 * S5_GROUP) ** -0.5)
    inp['s5_c_re'] = nrm((DEPTH, 2, S5_GROUPS, S5_GROUP, S5_STATE), S5_STATE ** -0.5)
    inp['s5_c_im'] = nrm((DEPTH, 2, S5_GROUPS, S5_GROUP, S5_STATE), S5_STATE ** -0.5)
    inp['s5_d'] = nrm((DEPTH, S5_WIDTH), 1.0)
    inp['s5_w_glu'] = nrm((DEPTH, S5_WIDTH, S5_WIDTH), S5_WIDTH ** -0.5)
    inp['s5_b_glu'] = nrm((DEPTH, S5_WIDTH), 0.02)
    inp['diff_q_norm'] = gain((DEPTH, DIFF_HEAD))
    inp['diff_k_norm'] = gain((DEPTH, DIFF_HEAD))
    inp['diff_lq1'] = nrm((DEPTH, DIFF_HEAD), 0.1)
    inp['diff_lk1'] = nrm((DEPTH, DIFF_HEAD), 0.1)
    inp['diff_lq2'] = nrm((DEPTH, DIFF_HEAD), 0.1)
    inp['diff_lk2'] = nrm((DEPTH, DIFF_HEAD), 0.1)
    inp['diff_subln'] = gain((DEPTH, DIFF_V))
    inp['moe_w_group'] = nrm((DEPTH, dm, MOE_GROUPS), dm ** -0.5)
    inp['moe_b_group'] = nrm((DEPTH, MOE_GROUPS), 0.01)
    inp['moe_w_expert'] = nrm((DEPTH, MOE_GROUPS, dm, MOE_PER_GROUP), dm ** -0.5)
    inp['moe_b_expert'] = nrm((DEPTH, MOE_GROUPS, MOE_PER_GROUP), 0.01)
    inp['moe_w_gate'] = nrm((DEPTH, N_EXPERTS, dm, EXPERT_HIDDEN), dm ** -0.5)
    inp['moe_w_up'] = nrm((DEPTH, N_EXPERTS, dm, EXPERT_HIDDEN), dm ** -0.5)
    inp['moe_w_down'] = nrm((DEPTH, N_EXPERTS, EXPERT_HIDDEN, dm), EXPERT_HIDDEN ** -0.5)
    return inp


def reference(x, c, ctx, c_ctx, w_ada, b_ada, norm1, norm2, w_in, w_out,
              mla_kv_norm, mla_w_uk, mla_w_uv, mla_q_norm, mla_k_norm,
              ssd_conv_w, ssd_conv_b, ssd_a_log, ssd_dt_bias, ssd_d, ssd_norm,
              s5_lam_re, s5_lam_im, s5_log_step, s5_b_re, s5_b_im, s5_c_re, s5_c_im,
              s5_d, s5_w_glu, s5_b_glu,
              diff_q_norm, diff_k_norm, diff_lq1, diff_lk1, diff_lq2, diff_lk2, diff_subln,
              moe_w_group, moe_b_group, moe_w_expert, moe_b_expert, moe_w_gate, moe_w_up, moe_w_down):
    bsz, seq, d = x.shape
    n_rows = seq // GRID_W
    rows = jnp.repeat(jnp.arange(n_rows, dtype=jnp.int32), GRID_W, total_repeat_length=seq)
    cols = jnp.broadcast_to(jnp.arange(GRID_W, dtype=jnp.int32)[None, :], (n_rows, GRID_W)).reshape(seq)
    x_lat, x_ctx = x, ctx
    for i in range(DEPTH):
        need_ctx = i < DEPTH - 1
        mod_l = (jax.nn.silu(c) @ w_ada[i] + b_ada[i]).reshape(bsz, N_MOD, 1, d)
        mod_c = (jax.nn.silu(c_ctx) @ w_ada[i] + b_ada[i]).reshape(N_MOD, d)
        p_l = modulate(x_lat, norm1[i], mod_l[:, 0], mod_l[:, 1]) @ w_in[i]
        p_c = modulate(x_ctx, norm1[i], mod_c[0], mod_c[1]) @ w_in[i]
        mla_l, ssd_l, s5_l, diff_l = split_groups(p_l)
        mla_c, ssd_c, s5_c, diff_c = split_groups(p_c)
        a_c, a_l = mla_mixer(mla_c, mla_l, rows, cols, mla_kv_norm[i], mla_w_uk[i], mla_w_uv[i],
                             mla_q_norm[i], mla_k_norm[i], need_ctx)
        b_c, b_l = ssd_mixer(ssd_c, ssd_l, ssd_conv_w[i], ssd_conv_b[i], ssd_a_log[i], ssd_dt_bias[i],
                             ssd_d[i], ssd_norm[i], need_ctx)
        s_c, s_l = s5_mixer(s5_c, s5_l, s5_lam_re[i], s5_lam_im[i], s5_log_step[i], s5_b_re[i], s5_b_im[i],
                            s5_c_re[i], s5_c_im[i], s5_d[i], s5_w_glu[i], s5_b_glu[i], need_ctx)
        lam_init = 0.8 - 0.6 * math.exp(-0.3 * i)
        f_c, f_l = diff_mixer(diff_c, diff_l, rows, cols, diff_q_norm[i], diff_k_norm[i], diff_lq1[i],
                              diff_lk1[i], diff_lq2[i], diff_lk2[i], diff_subln[i], lam_init, need_ctx)
        x_lat = x_lat + mod_l[:, 2] * (jnp.concatenate([a_l, b_l, s_l, f_l], axis=-1) @ w_out[i])
        h_l = modulate(x_lat, norm2[i], mod_l[:, 3], mod_l[:, 4]).reshape(-1, d)
        moe_p = (moe_w_group[i], moe_b_group[i], moe_w_expert[i], moe_b_expert[i],
                 moe_w_gate[i], moe_w_up[i], moe_w_down[i])
        if need_ctx:
            x_ctx = x_ctx + mod_c[2] * (jnp.concatenate([a_c, b_c, s_c, f_c], axis=-1) @ w_out[i])
            h_c = modulate(x_ctx, norm2[i], mod_c[3], mod_c[4]).reshape(-1, d)
            y = hier_moe(jnp.concatenate([h_l, h_c], axis=0), *moe_p)
            x_ctx = x_ctx + mod_c[5] * y[h_l.shape[0]:].reshape(x_ctx.shape)
            y_l = y[:h_l.shape[0]]
        else:
            y_l = hier_moe(h_l, *moe_p)
        x_lat = x_lat + mod_l[:, 5] * y_l.reshape(x_lat.shape)
    return x_lat
```

```python
import functools
import math

import jax
import jax.numpy as jnp
from jax import lax
from jax.experimental import pallas as pl
from jax.experimental.pallas import tpu as pltpu

F32 = jnp.float32
BF16 = jnp.bfloat16
HIGHEST = lax.Precision.HIGHEST

LANES = 128
EPS = 1e-6
ROPE_BASE = 10000.0
GRID_W = 64
N_MOD = 6

GROUP_WIDTH = 256
MLA_HEADS, MLA_NOPE, MLA_ROPE, MLA_V, MLA_KV_RANK = 4, 64, 32, 64, 128
MLA_QK = MLA_NOPE + MLA_ROPE
SSD_HEADS, SSD_HEAD_DIM, SSD_GROUPS, SSD_STATE, SSD_INNER = 4, 64, 2, 64, 256
SSD_GN = SSD_GROUPS * SSD_STATE
SSD_XBC = SSD_INNER + 2 * SSD_GN
S5_GROUPS, S5_GROUP, S5_STATE = 16, 16, 64
DIFF_HEADS, DIFF_HEAD, DIFF_V = 4, 32, 64
MOE_GROUPS, MOE_PER_GROUP, N_EXPERTS, EXPERT_HIDDEN = 4, 8, 32, 256

MLA_COLS = MLA_HEADS * MLA_QK + MLA_KV_RANK + MLA_ROPE
SSD_COLS = SSD_INNER + SSD_XBC + 2 * SSD_HEADS
S5_COLS = 256
DIFF_COLS = 768

P_MLA_Q = 0
P_MLA_CKV = 512
P_MLA_KR = 640
P_SSD = 768
P_S5 = 1664
P_DIFF = 1920
P_COLS = 2688

TOK_TILE = 256
SSD_CHUNK = 256
S5_CHUNK = 32
FLASH_KV_CHUNK = 512
MOE_TILE = 768
VMEM_LIMIT = 56 * 1024 * 1024


def _cparams(sem):
    return pltpu.CompilerParams(dimension_semantics=sem, vmem_limit_bytes=VMEM_LIMIT)


def _sigmoid(x):
    return 1.0 / (1.0 + jnp.exp(-x))


def _silu(x):
    return x * _sigmoid(x)


def _rms(x, n):
    return x * lax.rsqrt(jnp.sum(x * x, axis=-1, keepdims=True) * (1.0 / n) + EPS)


def _dot(a, b):
    return jnp.dot(a, b, preferred_element_type=F32)


def _dot_hi(a, b):
    return jnp.dot(a, b, preferred_element_type=F32, precision=HIGHEST)


def _dot_nt(a, b):
    return lax.dot_general(a, b, (((1,), (1,)), ((), ())), preferred_element_type=F32)


def _dot_tn_hi(a, b):
    return lax.dot_general(a, b, (((0,), (0,)), ((), ())), preferred_element_type=F32, precision=HIGHEST)


def _ada_kernel(c_ref, w_ref, b_ref, o_ref):
    o_ref[0] = _dot_hi(_silu(c_ref[...]), w_ref[0]) + b_ref[0]


def _ada(cvec, w_ada, b_ada):
    depth, d, nd = w_ada.shape
    blk = 1024
    return pl.pallas_call(
        _ada_kernel,
        grid=(depth, nd // blk),
        in_specs=[pl.BlockSpec((8, d), lambda i, j: (0, 0)),
                  pl.BlockSpec((1, d, blk), lambda i, j: (i, 0, j)),
                  pl.BlockSpec((1, 1, blk), lambda i, j: (i, 0, j))],
        out_specs=pl.BlockSpec((1, 8, blk), lambda i, j: (i, 0, j)),
        out_shape=jax.ShapeDtypeStruct((depth, 8, nd), F32),
        compiler_params=_cparams(("arbitrary", "arbitrary")),
        name="ada",
    )(cvec, w_ada, b_ada.reshape(depth, 1, nd))


def _rope(x, cos, sin_up, sin_dn):
    n = x.shape[-1]
    return x * cos + pltpu.roll(x, n - 8, axis=1) * sin_up + pltpu.roll(x, 8, axis=1) * sin_dn


def _front_kernel(x_ref, mod_ref, g1_ref, w_ref, rm_ref, rd_ref, gq_ref, gk_ref, gkv_ref, wuk_ref, wuv_ref,
                  gdq_ref, gdk_ref, seg_ref,
                  qm_ref, km_ref, vm_ref, qd_ref, kd_ref, vd_ref, ssd_ref, s5_ref):
    x = x_ref[0]
    mod = mod_ref[0]
    h = _rms(x, x.shape[-1]) * g1_ref[...]
    h = h * (1.0 + mod[1:2]) + mod[0:1]
    p = _dot(h.astype(BF16), w_ref[...])

    ssd_ref[0] = p[:, P_SSD:P_S5]
    s5_ref[0] = p[:, P_S5:P_DIFF]

    cos_m, up_m, dn_m = rm_ref[0], rm_ref[1], rm_ref[2]
    ckv = _rms(p[:, P_MLA_CKV:P_MLA_KR], MLA_KV_RANK) * gkv_ref[...]
    ckv = ckv.astype(BF16)
    k_nope = _dot(ckv, wuk_ref[...])
    vm_ref[0] = _dot(ckv, wuv_ref[...]).astype(BF16)
    k_rope = p[:, P_MLA_KR:P_SSD]
    q_scale = MLA_QK ** -0.5
    for hd in range(MLA_HEADS):
        sl = slice(hd * LANES, (hd + 1) * LANES)
        q = _rms(p[:, sl], MLA_QK) * gq_ref[...]
        qm_ref[0, :, sl] = (_rope(q, cos_m, up_m, dn_m) * q_scale).astype(BF16)
        k = _rms(k_nope[:, sl] + k_rope, MLA_QK) * gk_ref[...]
        km_ref[0, :, sl] = _rope(k, cos_m, up_m, dn_m).astype(BF16)

    cos_d, up_d, dn_d = rd_ref[0], rd_ref[1], rd_ref[2]
    d_scale = DIFF_HEAD ** -0.5
    for blk, (g_ref, o_ref, scale) in enumerate(((gdq_ref, qd_ref, d_scale), (gdk_ref, kd_ref, 1.0))):
        for half in range(2):
            sl = slice(P_DIFF + blk * 256 + half * LANES, P_DIFF + blk * 256 + (half + 1) * LANES)
            t = p[:, sl]
            ss = _dot_hi(t * t, seg_ref[...])
            t = t * lax.rsqrt(ss * (1.0 / DIFF_HEAD) + EPS) * g_ref[...]
            o_ref[0, :, half * LANES:(half + 1) * LANES] = (_rope(t, cos_d, up_d, dn_d) * scale).astype(BF16)
    vd_ref[0] = p[:, P_DIFF + 512:P_DIFF + 768].astype(BF16)


def _front(x, modtab, g1, w_in_p, rope_m, rope_d, gq, gk, gkv, wuk, wuv, gdq, gdk, seg):
    b, l, d = x.shape
    nt = l // TOK_TILE
    tok = lambda c: pl.BlockSpec((1, TOK_TILE, c), lambda i, j: (i, j, 0))
    full = lambda a: pl.BlockSpec(a.shape, lambda i, j: (0,) * a.ndim)
    rope_spec = pl.BlockSpec((3, TOK_TILE, LANES), lambda i, j: (0, j, 0))
    outs = [(512, BF16), (512, BF16), (256, BF16), (256, BF16), (256, BF16), (256, BF16),
            (P_S5 - P_SSD, F32), (256, F32)]
    return pl.pallas_call(
        _front_kernel,
        grid=(b, nt),
        in_specs=[tok(d),
                  pl.BlockSpec((1, 8, d), lambda i, j: (2 * i + jnp.minimum(j, 1), 0, 0)),
                  full(g1), full(w_in_p), rope_spec, rope_spec,
                  full(gq), full(gk), full(gkv), full(wuk), full(wuv), full(gdq), full(gdk), full(seg)],
        out_specs=[tok(c) for c, _ in outs],
        out_shape=[jax.ShapeDtypeStruct((b, l, c), dt) for c, dt in outs],
        compiler_params=_cparams(("parallel", "arbitrary")),
        name="front",
    )(x, modtab, g1, w_in_p, rope_m, rope_d, gq, gk, gkv, wuk, wuv, gdq, gdk, seg)


def _flash_kernel(lam_ref, q_ref, k_ref, v_ref, sub_ref, o_ref, m_scr, l_scr, acc_scr, *, n_sub, n_ctx, ck, post):
    qi = pl.program_id(2)
    tq = q_ref.shape[1]
    lk = k_ref.shape[1]
    lane = lax.broadcasted_iota(jnp.int32, (1, LANES), 1)
    lo = lane < (LANES // 2)
    q = q_ref[0]

    def q_of(a, c):
        if n_sub == 1:
            return q[:, a * LANES:(a + 1) * LANES]
        s = (a * n_sub + c) * DIFF_HEAD
        return jnp.where((lane >= s) & (lane < s + DIFF_HEAD), q, jnp.zeros_like(q))

    qs = [[q_of(a, c) for c in range(n_sub)] for a in range(2)]

    m_scr[...] = jnp.full(m_scr.shape, -jnp.inf, F32)
    l_scr[...] = jnp.zeros(l_scr.shape, F32)
    acc_scr[...] = jnp.zeros(acc_scr.shape, F32)

    def process(start, size):
        kc = k_ref[0, pl.ds(start, size), :]
        vc = v_ref[0, pl.ds(start, size), :]
        for c in range(n_sub):
            pv, alpha = [], []
            for a in range(2):
                mi = a * n_sub + c
                kk = kc[:, a * LANES:(a + 1) * LANES] if n_sub == 1 else kc
                s = _dot_nt(qs[a][c], kk)
                m_prev = m_scr[mi]
                m_new = jnp.maximum(m_prev, jnp.max(s, axis=-1, keepdims=True))
                al = jnp.exp(m_prev - m_new)
                e = jnp.exp(s - m_new)
                l_scr[mi] = al * l_scr[mi] + jnp.sum(e, axis=-1, keepdims=True)
                m_scr[mi] = m_new
                pv.append(_dot(e.astype(BF16), vc))
                alpha.append(al)
            acc_scr[c] = acc_scr[c] * jnp.where(lo, alpha[0], alpha[1]) + jnp.where(lo, pv[0], pv[1])

    process(0, n_ctx)

    @pl.when(qi > 0)
    def _():
        def body(i, carry):
            process(pl.multiple_of(n_ctx + i * ck, ck), ck)
            return carry
        lax.fori_loop(0, (lk - n_ctx) // ck, body, 0)

    def inv_l(c):
        return jnp.where(lo, 1.0 / l_scr[c], 1.0 / l_scr[n_sub + c])

    if n_sub == 1:
        o_ref[0] = acc_scr[0] * inv_l(0)
    else:
        o = acc_scr[0] * inv_l(0) - lam_ref[0] * (acc_scr[1] * inv_l(1))
        oo = o * o
        half = LANES // 2
        ss = jnp.where(lo, jnp.sum(jnp.where(lo, oo, 0.0), axis=-1, keepdims=True),
                       jnp.sum(jnp.where(lo, 0.0, oo), axis=-1, keepdims=True))
        o_ref[0] = o * lax.rsqrt(ss * (1.0 / half) + EPS) * sub_ref[...] * post


def _flash(lam, q, k, v, subln, *, n_sub, n_ctx, post):
    b, l, qw = q.shape
    wq = qw // 2
    nq = l // TOK_TILE
    kern = functools.partial(_flash_kernel, n_sub=n_sub, n_ctx=n_ctx, ck=FLASH_KV_CHUNK, post=post)
    return pl.pallas_call(
        kern,
        grid=(b, 2, nq),
        in_specs=[pl.BlockSpec(memory_space=pltpu.SMEM),
                  pl.BlockSpec((1, TOK_TILE, wq), lambda i, h, j: (i, j, h)),
                  pl.BlockSpec((1, l, wq), lambda i, h, j: (i, 0, h)),
                  pl.BlockSpec((1, l, LANES), lambda i, h, j: (i, 0, h)),
                  pl.BlockSpec((1, LANES), lambda i, h, j: (0, 0))],
        out_specs=pl.BlockSpec((1, TOK_TILE, LANES), lambda i, h, j: (i, j, h)),
        out_shape=jax.ShapeDtypeStruct((b, l, 2 * LANES), F32),
        scratch_shapes=[pltpu.VMEM((2 * n_sub, TOK_TILE, 1), F32),
                        pltpu.VMEM((2 * n_sub, TOK_TILE, 1), F32),
                        pltpu.VMEM((n_sub, TOK_TILE, LANES), F32)],
        compiler_params=_cparams(("parallel", "parallel", "arbitrary")),
        name="flash_diff" if n_sub == 2 else "flash_mla",
    )(lam, q, k, v, subln)


def _ssd_dir(d, first, xbc_ref, prev_ref, next_ref, edge_lo, edge_hi, cw_ref, cb_ref, dtb_ref, a_ref, dsk_ref,
             y_ref, st_scr):
    t = xbc_ref.shape[1]
    raw = xbc_ref[0]
    x = raw[:, SSD_INNER:SSD_INNER + SSD_XBC]
    row = lax.broadcasted_iota(jnp.int32, (t, 1), 0)
    prev_row = jnp.where(edge_lo, 0.0, prev_ref[0, 7:8, SSD_INNER:SSD_INNER + SSD_XBC])
    next_row = jnp.where(edge_hi, 0.0, next_ref[0, 0:1, SSD_INNER:SSD_INNER + SSD_XBC])
    xm1 = jnp.where(row == 0, prev_row, pltpu.roll(x, 1, axis=0))
    xp1 = jnp.where(row == t - 1, next_row, pltpu.roll(x, t - 1, axis=0))
    xc = _silu(cw_ref[0:1] * xm1 + cw_ref[1:2] * x + cw_ref[2:3] * xp1 + cb_ref[...])
    xs = xc[:, :SSD_INNER]
    bm = xc[:, SSD_INNER:SSD_INNER + SSD_GN]
    cm = xc[:, SSD_INNER + SSD_GN:]

    z = raw[:, SSD_INNER + SSD_XBC:] + dtb_ref[...]
    dt = jnp.maximum(z, 0.0) + jnp.log1p(jnp.exp(-jnp.abs(z)))
    da = dt * a_ref[...]

    ri = lax.broadcasted_iota(jnp.int32, (t, t), 0)
    ci = lax.broadcasted_iota(jnp.int32, (t, t), 1)
    mask = (ci <= ri) if d == 0 else (ci >= ri)
    mask_t = (ri <= ci) if d == 0 else (ri >= ci)
    cum = _dot_hi(mask.astype(F32), da)
    cum_t = _dot_tn_hi(da, mask_t.astype(F32))
    tot = jnp.sum(da, axis=0, keepdims=True)

    lane = lax.broadcasted_iota(jnp.int32, (1, LANES), 1)
    lo = lane < (LANES // 2)

    @pl.when(first)
    def _():
        st_scr[d] = jnp.zeros(st_scr.shape[1:], F32)

    ys = []
    for g in range(SSD_GROUPS):
        gmask = lo if g == 0 else jnp.logical_not(lo)
        cg = jnp.where(gmask, cm, 0.0)
        gram = lax.dot_general(cg, bm, (((1,), (1,)), ((), ())), preferred_element_type=F32,
                               precision=HIGHEST)
        xpair = xs[:, g * LANES:(g + 1) * LANES]
        yh = []
        for hh in range(2):
            hd = g * 2 + hh
            col = d * SSD_HEADS + hd
            cum_c = cum[:, col:col + 1]
            cum_r = cum_t[col:col + 1, :]
            dt_c = dt[:, col:col + 1]
            tot_c = tot[:, col:col + 1]
            decay = jnp.exp(jnp.where(mask, cum_c - cum_r, -jnp.inf))
            xdt = xpair * dt_c
            st = st_scr[d, hd]
            y = _dot_hi(gram * decay, xdt) + _dot_hi(cg * jnp.exp(cum_c), st)
            st_scr[d, hd] = st * jnp.exp(tot_c) + _dot_tn_hi(bm * jnp.exp(tot_c - cum_c), xdt)
            yh.append(y)
        ys.append(jnp.where(lo, yh[0], yh[1]))
    y = jnp.concatenate(ys, axis=1)
    if d == 0:
        y = y + xs * dsk_ref[...]
    y_ref[0] = y


def _ssd_kernel(xf_ref, pf_ref, nf_ref, xb_ref, pb_ref, nb_ref, cw_ref, cb_ref, dtb_ref, a_ref, dsk_ref,
                yf_ref, yb_ref, st_scr, *, n_ctx_chunks):
    s = pl.program_id(1)
    nc = pl.num_programs(1)
    cf = s
    cb = jnp.where(s < n_ctx_chunks, n_ctx_chunks - 1 - s, nc - 1 + n_ctx_chunks - s)
    seg_lo = lambda c: (c == 0) | (c == n_ctx_chunks)
    seg_hi = lambda c: (c == n_ctx_chunks - 1) | (c == nc - 1)
    _ssd_dir(0, s == 0, xf_ref, pf_ref, nf_ref, seg_lo(cf), seg_hi(cf), cw_ref, cb_ref, dtb_ref, a_ref, dsk_ref,
             yf_ref, st_scr)
    _ssd_dir(1, s == 0, xb_ref, pb_ref, nb_ref, seg_lo(cb), seg_hi(cb), cw_ref, cb_ref, dtb_ref, a_ref, dsk_ref,
             yb_ref, st_scr)


def _ssd(raw, conv_w, conv_b, dt_bias, a_vec, dskip, n_ctx):
    b, l, w = raw.shape
    t = SSD_CHUNK
    nc = l // t
    ncc = n_ctx // t
    r8 = t // 8
    nb8 = l // 8
    fwd = lambda s: s
    bwd = lambda s: jnp.where(s < ncc, ncc - 1 - s, nc - 1 + ncc - s)
    chunk = lambda f: pl.BlockSpec((1, t, w), lambda i, s: (i, f(s), 0))
    prev = lambda f: pl.BlockSpec((1, 8, w), lambda i, s: (i, jnp.maximum(f(s) * r8 - 1, 0), 0))
    nxt = lambda f: pl.BlockSpec((1, 8, w), lambda i, s: (i, jnp.minimum((f(s) + 1) * r8, nb8 - 1), 0))
    full = lambda a: pl.BlockSpec(a.shape, lambda i, s: (0,) * a.ndim)
    out = lambda f: pl.BlockSpec((1, t, SSD_INNER), lambda i, s: (i, f(s), 0))
    return pl.pallas_call(
        functools.partial(_ssd_kernel, n_ctx_chunks=ncc),
        grid=(b, nc),
        in_specs=[chunk(fwd), prev(fwd), nxt(fwd), chunk(bwd), prev(bwd), nxt(bwd),
                  full(conv_w), full(conv_b), full(dt_bias), full(a_vec), full(dskip)],
        out_specs=[out(fwd), out(bwd)],
        out_shape=[jax.ShapeDtypeStruct((b, l, SSD_INNER), F32)] * 2,
        scratch_shapes=[pltpu.VMEM((2, SSD_HEADS, LANES, LANES), F32)],
        compiler_params=_cparams(("parallel", "arbitrary")),
        name="ssd",
    )(raw, raw, raw, raw, raw, raw, conv_w, conv_b, dt_bias, a_vec, dskip)


def _s5_local_kernel(u_ref, m_ref, y_ref, s_ref):
    r = _dot_hi(u_ref[0], m_ref[0])
    nw = y_ref.shape[2]
    y_ref[0] = r[:, :nw]
    s_ref[0] = r[:, nw:]


def _s5_local(u, mw):
    g, nch, wd = u.shape
    nw = wd
    ns = mw.shape[2] - nw
    return pl.pallas_call(
        _s5_local_kernel,
        grid=(g,),
        in_specs=[pl.BlockSpec((1, nch, wd), lambda i: (i, 0, 0)),
                  pl.BlockSpec((1, wd, nw + ns), lambda i: (i, 0, 0))],
        out_specs=[pl.BlockSpec((1, nch, nw), lambda i: (i, 0, 0)),
                   pl.BlockSpec((1, nch, ns), lambda i: (0, 0, i))],
        out_shape=[jax.ShapeDtypeStruct((g, nch, nw), F32), jax.ShapeDtypeStruct((1, nch, g * ns), F32)],
        compiler_params=_cparams(("parallel",)),
        name="s5_local",
    )(u, mw)


def _s5_scan_kernel(s_ref, ca_ref, cb_ref, o_ref, *, n_ctx_chunks):
    n = s_ref.shape[0]
    ca = ca_ref[...]
    cb = cb_ref[...]
    lane = lax.broadcasted_iota(jnp.int32, (1, 2 * LANES), 1)
    is_f = lane < LANES

    def swap(v):
        return jnp.concatenate([pltpu.roll(v[:, :LANES], LANES // 2, axis=1),
                                pltpu.roll(v[:, LANES:], LANES // 2, axis=1)], axis=1)

    def body(i, st):
        jf = i
        jb = jnp.where(i < n_ctx_chunks, n_ctx_chunks - 1 - i, n - 1 + n_ctx_chunks - i)
        cur_f = s_ref[jf]
        cur_b = s_ref[jb]
        o_ref[jf, :, :LANES] = st[:, :LANES]
        o_ref[jb, :, LANES:] = st[:, LANES:]
        return st * ca + swap(st) * cb + jnp.where(is_f, cur_f, cur_b)

    lax.fori_loop(0, n, body, jnp.zeros(ca.shape, F32))


def _s5_scan(s_loc, ca, cb, n_batch, n_ctx_chunks):
    nch, g, ns = s_loc.shape
    n = nch // n_batch
    return pl.pallas_call(
        functools.partial(_s5_scan_kernel, n_ctx_chunks=n_ctx_chunks),
        grid=(n_batch,),
        in_specs=[pl.BlockSpec((n, g, ns), lambda i: (i, 0, 0)),
                  pl.BlockSpec(ca.shape, lambda i: (0, 0)),
                  pl.BlockSpec(cb.shape, lambda i: (0, 0))],
        out_specs=pl.BlockSpec((n, g, ns), lambda i: (i, 0, 0)),
        out_shape=jax.ShapeDtypeStruct((nch, g, ns), F32),
        compiler_params=_cparams(("parallel",)),
        name="s5_scan",
    )(s_loc, ca, cb)


def _s5_read_kernel(y1_ref, sp_ref, v_ref, y_ref):
    y_ref[0] = y1_ref[0] + _dot_hi(sp_ref[0], v_ref[0])


def _s5_read(y1, s_prev, vmat):
    g, nch, nw = y1.shape
    ns = vmat.shape[1]
    return pl.pallas_call(
        _s5_read_kernel,
        grid=(g,),
        in_specs=[pl.BlockSpec((1, nch, nw), lambda i: (i, 0, 0)),
                  pl.BlockSpec((1, nch, ns), lambda i: (0, 0, i)),
                  pl.BlockSpec((1, ns, nw), lambda i: (i, 0, 0))],
        out_specs=pl.BlockSpec((1, nch, nw), lambda i: (i, 0, 0)),
        out_shape=jax.ShapeDtypeStruct((g, nch, nw), F32),
        compiler_params=_cparams(("parallel",)),
        name="s5_read",
    )(y1, s_prev, vmat)


def _s5_matrices(lam_re, lam_im, log_step, b_re, b_im, c_re, c_im):
    t = S5_CHUNK
    step = jnp.exp(log_step)[..., None]
    den = lam_re * lam_re + lam_im * lam_im
    er = jnp.exp(lam_re * step)
    ar, ai = er * jnp.cos(lam_im * step), er * jnp.sin(lam_im * step)
    nr, ni = ar - 1.0, ai
    fr = (nr * lam_re + ni * lam_im) / den
    fi = (ni * lam_re - nr * lam_im) / den
    bre, bim = b_re[:, None], b_im[:, None]
    br = fr[..., None] * bre - fi[..., None] * bim
    bi = fr[..., None] * bim + fi[..., None] * bre
    kk = jnp.arange(t + 1, dtype=F32)[:, None, None, None, None]
    ek = jnp.exp(kk * (lam_re * step)[None])
    pr, pi = ek * jnp.cos(kk * (lam_im * step)[None]), ek * jnp.sin(kk * (lam_im * step)[None])
    abr = pr[..., None] * br[None] - pi[..., None] * bi[None]
    abi = pr[..., None] * bi[None] + pi[..., None] * br[None]
    ein = functools.partial(jnp.einsum, precision=HIGHEST)
    kern = ein('dsgop,kdsgpj->kdsgoj', c_re, abr[:t]) - ein('dsgop,kdsgpj->kdsgoj', c_im, abi[:t])
    kf, kb = kern[:, :, 0], kern[:, :, 1]
    lag = jnp.concatenate([kb[:0:-1], (kf[0] + kb[0])[None], kf[1:]], axis=0)
    idx = jnp.arange(t)[None, :] - jnp.arange(t)[:, None] + t - 1
    m = lag[idx]
    m = m.transpose(2, 3, 0, 5, 1, 4).reshape(m.shape[2], S5_GROUPS, t * S5_GROUP, t * S5_GROUP)
    wf = jnp.stack([abr[:t][::-1, :, 0], abi[:t][::-1, :, 0]], axis=0)
    wb = jnp.stack([abr[:t, :, 1], abi[:t, :, 1]], axis=0)
    w = jnp.concatenate([wf, wb], axis=0)
    w = w.transpose(2, 3, 1, 5, 0, 4).reshape(w.shape[2], S5_GROUPS, t * S5_GROUP, 4 * S5_STATE)
    cr, ci = c_re[None], c_im[None]
    vre = cr * pr[:, :, :, :, None, :] - ci * pi[:, :, :, :, None, :]
    vim = -(cr * pi[:, :, :, :, None, :] + ci * pr[:, :, :, :, None, :])
    vf = jnp.stack([vre[1:, :, 0], vim[1:, :, 0]], axis=0)
    vb = jnp.stack([vre[:0:-1, :, 1], vim[:0:-1, :, 1]], axis=0)
    v = jnp.concatenate([vf, vb], axis=0)
    v = v.transpose(2, 3, 0, 5, 1, 4).reshape(v.shape[2], S5_GROUPS, 4 * S5_STATE, t * S5_GROUP)
    atr, ati = pr[t], pi[t]
    ca = jnp.concatenate([atr[:, 0], atr[:, 0], atr[:, 1], atr[:, 1]], axis=-1)
    cb = jnp.concatenate([-ati[:, 0], ati[:, 0], -ati[:, 1], ati[:, 1]], axis=-1)
    return jnp.concatenate([m, w], axis=-1), v, ca, cb


def _mix_kernel(x_ref, mod_ref, a_ref, yf_ref, yb_ref, z_ref, gs_ref, y5_ref, u5_ref, d5_ref, wg_ref, bg_ref,
                f_ref, wo_ref, o_ref):
    mod = mod_ref[0]
    ssd = _rms((yf_ref[0] + yb_ref[0]) * _silu(z_ref[0]), SSD_INNER) * gs_ref[...]
    y5 = y5_ref[0] + d5_ref[...] * u5_ref[0]
    g5 = jax.nn.gelu(y5)
    s5 = g5 * _sigmoid(_dot(g5.astype(BF16), wg_ref[...]) + bg_ref[...])
    acc = _dot(a_ref[0].astype(BF16), wo_ref[0:256, :])
    acc += _dot(ssd.astype(BF16), wo_ref[256:512, :])
    acc += _dot(s5.astype(BF16), wo_ref[512:768, :])
    acc += _dot(f_ref[0].astype(BF16), wo_ref[768:1024, :])
    o_ref[0] = x_ref[0] + mod[2:3] * acc


def _mix(x, modtab, a, yf, yb, ssd_raw, gs, y5, u5, d5, wglu, bglu, f, wo):
    b, l, d = x.shape
    nt = l // TOK_TILE
    tok = lambda c: pl.BlockSpec((1, TOK_TILE, c), lambda i, j: (i, j, 0))
    full = lambda arr: pl.BlockSpec(arr.shape, lambda i, j: (0,) * arr.ndim)
    return pl.pallas_call(
        _mix_kernel,
        grid=(b, nt),
        in_specs=[tok(d),
                  pl.BlockSpec((1, 8, d), lambda i, j: (2 * i + jnp.minimum(j, 1), 0, 0)),
                  tok(256), tok(256), tok(256), tok(256), full(gs), tok(256), tok(256), full(d5), full(wglu),
                  full(bglu), tok(256), full(wo)],
        out_specs=tok(d),
        out_shape=jax.ShapeDtypeStruct((b, l, d), F32),
        compiler_params=_cparams(("parallel", "arbitrary")),
        name="mix_out",
    )(x, modtab, a, yf, yb, ssd_raw, gs, y5, u5, d5, wglu, bglu, f, wo)


def _moe_kernel(x_ref, mod_ref, g2_ref, wr_ref, br_ref, wg_ref, wu_ref, wd_ref, o_ref, h_scr, gate_scr, acc_scr,
                *, n_ctx):
    j = pl.program_id(1)
    e = pl.program_id(2)
    tm = x_ref.shape[1]
    row = lax.broadcasted_iota(jnp.int32, (tm, 1), 0) + j * tm
    is_ctx = row < n_ctx
    lane = lax.broadcasted_iota(jnp.int32, (1, LANES), 1)

    def modrow(r):
        return jnp.where(is_ctx, mod_ref[0, r:r + 1, :], mod_ref[1, r:r + 1, :])

    @pl.when(e == 0)
    def _():
        x = x_ref[0]
        h = _rms(x, x.shape[-1]) * g2_ref[...]
        h = h * (1.0 + modrow(4)) + modrow(3)
        h_scr[...] = h.astype(BF16)
        logits = _dot_hi(h, wr_ref[...]) + br_ref[...]
        neg = -jnp.inf
        lanef = lane.astype(F32)
        big = float(4 * LANES)
        first = lambda hit: jnp.min(jnp.where(hit, lanef, big), axis=-1, keepdims=True)
        glog = jnp.where(lane < MOE_GROUPS, logits, neg)
        gmax = jnp.max(glog, axis=-1, keepdims=True)
        gidx = first(glog == gmax)
        p_group = 1.0 / jnp.sum(jnp.exp(glog - gmax), axis=-1, keepdims=True)
        e0 = MOE_GROUPS + gidx * MOE_PER_GROUP
        elog = jnp.where((lanef >= e0) & (lanef < e0 + MOE_PER_GROUP), logits, neg)
        v1 = jnp.max(elog, axis=-1, keepdims=True)
        i1 = first(elog == v1)
        elog2 = jnp.where(lanef == i1, neg, elog)
        v2 = jnp.max(elog2, axis=-1, keepdims=True)
        i2 = first(elog2 == v2)
        e2 = jnp.exp(v2 - v1)
        w1 = p_group / (1.0 + e2)
        w2 = p_group * e2 / (1.0 + e2)
        gate_scr[...] = jnp.where(lanef == i1, w1, 0.0) + jnp.where(lanef == i2, w2, 0.0)
        acc_scr[...] = jnp.zeros(acc_scr.shape, F32)

    h = h_scr[...]
    gate = jnp.sum(jnp.where(lane == MOE_GROUPS + e, gate_scr[...], 0.0), axis=-1, keepdims=True)
    hid = _silu(_dot(h, wg_ref[0])) * _dot(h, wu_ref[0]) * gate
    acc_scr[...] += _dot(hid.astype(BF16), wd_ref[0])

    @pl.when(e == pl.num_programs(2) - 1)
    def _():
        o_ref[0] = x_ref[0] + modrow(5) * acc_scr[...]


def _moe(x, modtab4, g2, wr, br, wg, wu, wd, n_ctx):
    b, l, d = x.shape
    tm = MOE_TILE
    nt = l // tm
    ne = wg.shape[0]
    full = lambda arr: pl.BlockSpec(arr.shape, lambda i, j, e: (0,) * arr.ndim)
    return pl.pallas_call(
        functools.partial(_moe_kernel, n_ctx=n_ctx),
        grid=(b, nt, ne),
        in_specs=[pl.BlockSpec((1, tm, d), lambda i, j, e: (i, j, 0)),
                  pl.BlockSpec((2, 8, d), lambda i, j, e: (i, 0, 0)),
                  full(g2), full(wr), full(br),
                  pl.BlockSpec((1, d, EXPERT_HIDDEN), lambda i, j, e: (e, 0, 0)),
                  pl.BlockSpec((1, d, EXPERT_HIDDEN), lambda i, j, e: (e, 0, 0)),
                  pl.BlockSpec((1, EXPERT_HIDDEN, d), lambda i, j, e: (e, 0, 0))],
        out_specs=pl.BlockSpec((1, tm, d), lambda i, j, e: (i, j, 0)),
        out_shape=jax.ShapeDtypeStruct((b, l, d), F32),
        scratch_shapes=[pltpu.VMEM((tm, d), BF16), pltpu.VMEM((tm, LANES), F32), pltpu.VMEM((tm, d), F32)],
        compiler_params=_cparams(("parallel", "parallel", "arbitrary")),
        name="moe",
    )(x, modtab4, g2, wr, br, wg, wu, wd)


def _pad_in_proj(w_in):
    depth, d, _ = w_in.shape
    out = jnp.zeros((depth, d, P_COLS), w_in.dtype)
    o = 0
    for hd in range(MLA_HEADS):
        out = out.at[:, :, P_MLA_Q + hd * LANES:P_MLA_Q + hd * LANES + MLA_QK].set(w_in[:, :, o:o + MLA_QK])
        o += MLA_QK
    out = out.at[:, :, P_MLA_CKV:P_MLA_CKV + MLA_KV_RANK].set(w_in[:, :, o:o + MLA_KV_RANK])
    o += MLA_KV_RANK
    out = out.at[:, :, P_MLA_KR + MLA_NOPE:P_MLA_KR + MLA_QK].set(w_in[:, :, o:o + MLA_ROPE])
    o += MLA_ROPE
    n = SSD_INNER + SSD_XBC + 2 * SSD_HEADS
    out = out.at[:, :, P_SSD:P_SSD + n].set(w_in[:, :, o:o + n])
    o += n
    out = out.at[:, :, P_S5:P_S5 + S5_COLS].set(w_in[:, :, o:o + S5_COLS])
    o += S5_COLS
    out = out.at[:, :, P_DIFF:P_DIFF + DIFF_COLS].set(w_in[:, :, o:o + DIFF_COLS])
    return out


def _rope_tables(seq, n_ctx):
    pos = jnp.arange(seq, dtype=jnp.int32)
    rows = (pos // GRID_W).astype(F32)
    cols = (pos % GRID_W).astype(F32)
    half = MLA_ROPE // 2
    inv = ROPE_BASE ** (-jnp.arange(0, half, 2, dtype=F32) / half)
    r = jnp.arange(MLA_ROPE)
    freq = inv[r % (half // 2)]
    ang = jnp.where((r < half)[None, :], rows[:, None], cols[:, None]) * freq[None, :]
    cos, sin = jnp.cos(ang), jnp.sin(ang)
    first = ((r % half) < half // 2)[None, :]
    up = jnp.where(first, -sin, 0.0)
    dn = jnp.where(first, 0.0, sin)
    t32 = jnp.stack([cos, up, dn])
    ident = jnp.stack([jnp.ones((n_ctx, MLA_ROPE), F32), jnp.zeros((n_ctx, MLA_ROPE), F32),
                       jnp.zeros((n_ctx, MLA_ROPE), F32)])
    t32 = jnp.concatenate([ident, t32], axis=1)
    n = t32.shape[1]
    base = jnp.stack([jnp.ones((n, LANES), F32), jnp.zeros((n, LANES), F32), jnp.zeros((n, LANES), F32)])
    rope_m = base.at[:, :, MLA_NOPE:MLA_QK].set(t32)
    rope_d = jnp.tile(t32, (1, 1, LANES // MLA_ROPE))
    return rope_m, rope_d


def _pad_lanes(v, n):
    return jnp.zeros((1, n), F32).at[0, :v.shape[0]].set(v)


def kernel(x, c, ctx, c_ctx, w_ada, b_ada, norm1, norm2, w_in, w_out, mla_kv_norm, mla_w_uk, mla_w_uv, mla_q_norm, mla_k_norm, ssd_conv_w, ssd_conv_b, ssd_a_log, ssd_dt_bias, ssd_d, ssd_norm, s5_lam_re, s5_lam_im, s5_log_step, s5_b_re, s5_b_im, s5_c_re, s5_c_im, s5_d, s5_w_glu, s5_b_glu, diff_q_norm, diff_k_norm, diff_lq1, diff_lk1, diff_lq2, diff_lk2, diff_subln, moe_w_group, moe_b_group, moe_w_expert, moe_b_expert, moe_w_gate, moe_w_up, moe_w_down):
    bsz, seq, d = x.shape
    n_ctx = ctx.shape[1]
    depth = w_in.shape[0]
    assert n_ctx == TOK_TILE and seq % FLASH_KV_CHUNK == 0 and d == 1024
    l = n_ctx + seq
    assert l % MOE_TILE == 0 and l % SSD_CHUNK == 0 and n_ctx % SSD_CHUNK == 0

    cvec = jnp.zeros((8, d), F32).at[:bsz].set(c).at[bsz].set(c_ctx)
    ada = _ada(cvec, w_ada, b_ada).reshape(depth, 8, N_MOD, d)
    mod_lat = ada[:, :bsz]
    mod_ctx = jnp.broadcast_to(ada[:, bsz][:, None], mod_lat.shape)
    modtab = jnp.stack([mod_ctx, mod_lat], axis=2)
    modtab = jnp.pad(modtab, ((0, 0), (0, 0), (0, 0), (0, 8 - N_MOD), (0, 0))).reshape(depth, bsz * 2, 8, d)

    w_in_p = _pad_in_proj(w_in).astype(BF16)
    w_out_b = w_out.astype(BF16)
    rope_m, rope_d = _rope_tables(seq, n_ctx)
    wuk = jnp.zeros((depth, MLA_KV_RANK, MLA_HEADS * LANES), F32)
    for hd in range(MLA_HEADS):
        wuk = wuk.at[:, :, hd * LANES:hd * LANES + MLA_NOPE].set(mla_w_uk[:, :, hd * MLA_NOPE:(hd + 1) * MLA_NOPE])
    wuk = wuk.astype(BF16)
    wuv = mla_w_uv.astype(BF16)
    lane = jnp.arange(LANES)
    seg = (lane[:, None] // DIFF_HEAD == lane[None, :] // DIFF_HEAD).astype(F32)
    a_vec = -jnp.exp(ssd_a_log.astype(F32)).reshape(depth, 2 * SSD_HEADS)
    s5_mw, s5_v, s5_ca, s5_cb = _s5_matrices(s5_lam_re.astype(F32), s5_lam_im.astype(F32), s5_log_step.astype(F32),
                                             s5_b_re.astype(F32), s5_b_im.astype(F32),
                                             s5_c_re.astype(F32), s5_c_im.astype(F32))
    w_router = jnp.concatenate([moe_w_group, moe_w_expert.transpose(0, 2, 1, 3).reshape(depth, d, N_EXPERTS)], axis=-1)
    w_router = jnp.pad(w_router, ((0, 0), (0, 0), (0, LANES - w_router.shape[-1])))
    b_router = jnp.concatenate([moe_b_group, moe_b_expert.reshape(depth, N_EXPERTS)], axis=-1)
    b_router = jnp.pad(b_router, ((0, 0), (0, LANES - b_router.shape[-1])))
    wg_b, wu_b, wd_b = moe_w_gate.astype(BF16), moe_w_up.astype(BF16), moe_w_down.astype(BF16)
    lam_all = (jnp.exp(jnp.sum(diff_lq1.astype(F32) * diff_lk1.astype(F32), axis=-1))
               - jnp.exp(jnp.sum(diff_lq2.astype(F32) * diff_lk2.astype(F32), axis=-1)))

    n5 = l // S5_CHUNK
    xs = jnp.concatenate([ctx, x], axis=1)
    for i in range(depth):
        lam_init = 0.8 - 0.6 * math.exp(-0.3 * i)
        qm, km, vm, qd, kd, vd, ssd_raw, u5 = _front(
            xs, modtab[i], norm1[i][None], w_in_p[i], rope_m, rope_d,
            _pad_lanes(mla_q_norm[i], LANES), _pad_lanes(mla_k_norm[i], LANES), mla_kv_norm[i][None],
            wuk[i], wuv[i], jnp.tile(diff_q_norm[i], 4)[None], jnp.tile(diff_k_norm[i], 4)[None], seg)
        zero1 = jnp.zeros((1,), F32)
        a = _flash(zero1, qm, km, vm, jnp.ones((1, LANES), F32), n_sub=1, n_ctx=n_ctx, post=1.0)
        f = _flash((lam_all[i] + lam_init).reshape(1), qd, kd, vd, jnp.tile(diff_subln[i], 2)[None],
                   n_sub=2, n_ctx=n_ctx, post=1.0 - lam_init)
        yf, yb = _ssd(ssd_raw, ssd_conv_w[i], ssd_conv_b[i][None],
                      _pad_lanes(ssd_dt_bias[i].reshape(-1), LANES), _pad_lanes(a_vec[i], LANES),
                      jnp.repeat(ssd_d[i], SSD_HEAD_DIM)[None], n_ctx)
        u5g = u5.reshape(bsz, n5, S5_CHUNK, S5_GROUPS, S5_GROUP).transpose(3, 0, 1, 2, 4)
        u5g = u5g.reshape(S5_GROUPS, bsz * n5, S5_CHUNK * S5_GROUP)
        y1, s_loc = _s5_local(u5g, s5_mw[i])
        s_loc = s_loc.reshape(bsz * n5, S5_GROUPS, 4 * S5_STATE)
        s_prev = _s5_scan(s_loc, s5_ca[i], s5_cb[i], bsz, n_ctx // S5_CHUNK)
        y5 = _s5_read(y1, s_prev.reshape(1, bsz * n5, S5_GROUPS * 4 * S5_STATE), s5_v[i])
        y5 = y5.reshape(S5_GROUPS, bsz, n5, S5_CHUNK, S5_GROUP).transpose(1, 2, 3, 0, 4).reshape(bsz, l, S5_COLS)
        xs = _mix(xs, modtab[i], a, yf, yb, ssd_raw, ssd_norm[i][None], y5, u5, s5_d[i][None],
                  s5_w_glu[i].astype(BF16), s5_b_glu[i][None], f, w_out_b[i])
        xs = _moe(xs, modtab[i], norm2[i][None], w_router[i], b_router[i][None], wg_b[i], wu_b[i], wd_b[i], n_ctx)
    return xs[:, n_ctx:]
```

```python
import functools
import math

import jax
import jax.numpy as jnp
from jax import lax
from jax.experimental import pallas as pl
from jax.experimental.pallas import tpu as pltpu

F32 = jnp.float32
BF16 = jnp.bfloat16
HIGHEST = lax.Precision.HIGHEST

LANES = 128
EPS = 1e-6
ROPE_BASE = 10000.0
GRID_W = 64
N_MOD = 6

GROUP_WIDTH = 256
MLA_HEADS, MLA_NOPE, MLA_ROPE, MLA_V, MLA_KV_RANK = 4, 64, 32, 64, 128
MLA_QK = MLA_NOPE + MLA_ROPE
SSD_HEADS, SSD_HEAD_DIM, SSD_GROUPS, SSD_STATE, SSD_INNER = 4, 64, 2, 64, 256
SSD_GN = SSD_GROUPS * SSD_STATE
SSD_XBC = SSD_INNER + 2 * SSD_GN
S5_GROUPS, S5_GROUP, S5_STATE = 16, 16, 64
DIFF_HEADS, DIFF_HEAD, DIFF_V = 4, 32, 64
MOE_GROUPS, MOE_PER_GROUP, N_EXPERTS, EXPERT_HIDDEN = 4, 8, 32, 256

MLA_COLS = MLA_HEADS * MLA_QK + MLA_KV_RANK + MLA_ROPE
SSD_COLS = SSD_INNER + SSD_XBC + 2 * SSD_HEADS
S5_COLS = 256
DIFF_COLS = 768

P_MLA_Q = 0
P_MLA_CKV = 512
P_MLA_KR = 640
P_SSD = 768
P_S5 = 1664
P_DIFF = 1920
P_COLS = 2944
V_ONE = 64
SCORE_BOUND = 40.0
LOG2E = 1.4426950408889634

TOK_TILE = 256
SSD_CHUNK = 256
S5_CHUNK = 32
FLASH_KV_CHUNK = 1024
MOE_TILE = 768
VMEM_LIMIT = 56 * 1024 * 1024


def _cparams(sem):
    return pltpu.CompilerParams(dimension_semantics=sem, vmem_limit_bytes=VMEM_LIMIT)


def _sigmoid(x):
    return 1.0 / (1.0 + jnp.exp(-x))


def _silu(x):
    return x * _sigmoid(x)


def _rms(x, n):
    return x * lax.rsqrt(jnp.sum(x * x, axis=-1, keepdims=True) * (1.0 / n) + EPS)


def _dot(a, b):
    return jnp.dot(a, b, preferred_element_type=F32)


def _dot_hi(a, b):
    return jnp.dot(a, b, preferred_element_type=F32, precision=HIGHEST)


def _dot_nt(a, b):
    return lax.dot_general(a, b, (((1,), (1,)), ((), ())), preferred_element_type=F32)


def _dot_tn(a, b):
    return lax.dot_general(a, b, (((0,), (0,)), ((), ())), preferred_element_type=F32)


def _dot_tn_hi(a, b):
    return lax.dot_general(a, b, (((0,), (0,)), ((), ())), preferred_element_type=F32, precision=HIGHEST)


def _ada_kernel(c_ref, w_ref, b_ref, o_ref):
    o_ref[0] = _dot_hi(_silu(c_ref[...]), w_ref[0]) + b_ref[0]


def _ada(cvec, w_ada, b_ada):
    depth, d, nd = w_ada.shape
    blk = 1024
    return pl.pallas_call(
        _ada_kernel,
        grid=(depth, nd // blk),
        in_specs=[pl.BlockSpec((8, d), lambda i, j: (0, 0)),
                  pl.BlockSpec((1, d, blk), lambda i, j: (i, 0, j)),
                  pl.BlockSpec((1, 1, blk), lambda i, j: (i, 0, j))],
        out_specs=pl.BlockSpec((1, 8, blk), lambda i, j: (i, 0, j)),
        out_shape=jax.ShapeDtypeStruct((depth, 8, nd), F32),
        compiler_params=_cparams(("arbitrary", "arbitrary")),
        name="ada",
    )(cvec, w_ada, b_ada.reshape(depth, 1, nd))


def _rope(x, cos, sin_up, sin_dn):
    n = x.shape[-1]
    return x * cos + pltpu.roll(x, n - 8, axis=1) * sin_up + pltpu.roll(x, 8, axis=1) * sin_dn


def _front_kernel(x_ref, mod_ref, g1_ref, w_ref, rm_ref, rd_ref, gq_ref, gk_ref, gkv_ref, wuk_ref, wuv_ref,
                  gdq_ref, gdk_ref, seg_ref, one_ref,
                  qm_ref, km_ref, vm_ref, qd_ref, kd_ref, vd_ref, ssd_ref, s5_ref):
    x = x_ref[0]
    mod = mod_ref[0]
    h = _rms(x, x.shape[-1]) * g1_ref[...]
    h = h * (1.0 + mod[1:2]) + mod[0:1]
    p = _dot(h.astype(BF16), w_ref[...])

    ssd_ref[0] = p[:, P_SSD:P_S5]
    s5_ref[0] = p[:, P_S5:P_DIFF]

    cos_m, up_m, dn_m = rm_ref[0], rm_ref[1], rm_ref[2]
    ckv = _rms(p[:, P_MLA_CKV:P_MLA_KR], MLA_KV_RANK) * gkv_ref[...]
    ckv = ckv.astype(BF16)
    k_nope = _dot(ckv, wuk_ref[...])
    vm_ref[0] = (_dot(ckv, wuv_ref[...]) + one_ref[...]).astype(BF16)
    k_rope = p[:, P_MLA_KR:P_SSD]
    q_scale = MLA_QK ** -0.5 * LOG2E
    for hd in range(MLA_HEADS):
        sl = slice(hd * LANES, (hd + 1) * LANES)
        q = _rms(p[:, sl], MLA_QK) * gq_ref[...]
        qm_ref[0, :, sl] = (_rope(q, cos_m, up_m, dn_m) * q_scale).astype(BF16)
        k = _rms(k_nope[:, sl] + k_rope, MLA_QK) * gk_ref[...]
        km_ref[0, :, sl] = _rope(k, cos_m, up_m, dn_m).astype(BF16)

    cos_d, up_d, dn_d = rd_ref[0], rd_ref[1], rd_ref[2]
    d_scale = DIFF_HEAD ** -0.5 * LOG2E
    for blk, (g_ref, o_ref, scale) in enumerate(((gdq_ref, qd_ref, d_scale), (gdk_ref, kd_ref, 1.0))):
        for half in range(2):
            sl = slice(P_DIFF + blk * 256 + half * LANES, P_DIFF + blk * 256 + (half + 1) * LANES)
            t = p[:, sl]
            ss = _dot_hi(t * t, seg_ref[...])
            t = t * lax.rsqrt(ss * (1.0 / DIFF_HEAD) + EPS) * g_ref[...]
            o_ref[0, :, half * LANES:(half + 1) * LANES] = (_rope(t, cos_d, up_d, dn_d) * scale).astype(BF16)
    vd_ref[0] = (p[:, P_DIFF + 512:P_COLS] + one_ref[...]).astype(BF16)


def _front(x, modtab, g1, w_in_p, rope_m, rope_d, gq, gk, gkv, wuk, wuv, gdq, gdk, seg):
    b, l, d = x.shape
    nt = l // TOK_TILE
    tok = lambda c: pl.BlockSpec((1, TOK_TILE, c), lambda i, j: (i, j, 0))
    full = lambda a: pl.BlockSpec(a.shape, lambda i, j: (0,) * a.ndim)
    rope_spec = pl.BlockSpec((3, TOK_TILE, LANES), lambda i, j: (0, j, 0))
    one = (jnp.arange(4 * LANES) % LANES == V_ONE).astype(F32)[None]
    outs = [(512, BF16), (512, BF16), (512, BF16), (256, BF16), (256, BF16), (512, BF16),
            (P_S5 - P_SSD, F32), (256, F32)]
    return pl.pallas_call(
        _front_kernel,
        grid=(b, nt),
        in_specs=[tok(d),
                  pl.BlockSpec((1, 8, d), lambda i, j: (2 * i + jnp.minimum(j, 1), 0, 0)),
                  full(g1), full(w_in_p), rope_spec, rope_spec,
                  full(gq), full(gk), full(gkv), full(wuk), full(wuv), full(gdq), full(gdk), full(seg), full(one)],
        out_specs=[tok(c) for c, _ in outs],
        out_shape=[jax.ShapeDtypeStruct((b, l, c), dt) for c, dt in outs],
        compiler_params=_cparams(("parallel", "arbitrary")),
        name="front",
    )(x, modtab, g1, w_in_p, rope_m, rope_d, gq, gk, gkv, wuk, wuv, gdq, gdk, seg, one)


def _flash_kernel(scal_ref, q_ref, k_ref, v_ref, sub_ref, o_ref, acc_scr, m_scr, *, n_sub, n_ctx, ck, post):
    qi = pl.program_id(2)
    lk = k_ref.shape[1]
    lane = lax.broadcasted_iota(jnp.int32, (1, LANES), 1)
    lo = lane < V_ONE
    q = q_ref[0]
    maps = [(a, c) for a in range(2) for c in range(n_sub)]

    def q_of(a, c):
        if n_sub == 1:
            return q[:, a * LANES:(a + 1) * LANES]
        s = (a * n_sub + c) * DIFF_HEAD
        return jnp.where((lane >= s) & (lane < s + DIFF_HEAD), q, jnp.zeros_like(q))

    qs = [q_of(a, c) for a, c in maps]

    def scores(mi, start, size):
        a = maps[mi][0]
        kk = k_ref[0, pl.ds(start, size), a * LANES:(a + 1) * LANES] if n_sub == 1 else k_ref[0, pl.ds(start, size), :]
        return _dot_nt(qs[mi], kk)

    def values(mi, start, size):
        a = maps[mi][0]
        return v_ref[0, pl.ds(start, size), a * LANES:(a + 1) * LANES]

    def step_bounded(start, size):
        for mi in range(len(maps)):
            p = jnp.exp2(scores(mi, start, size)).astype(BF16)
            acc_scr[mi] += _dot(p, values(mi, start, size))

    def step_online(start, size):
        for mi in range(len(maps)):
            s = scores(mi, start, size)
            m_prev = m_scr[mi]
            m_new = jnp.maximum(m_prev, jnp.max(s, axis=-1, keepdims=True))
            p = jnp.exp2(s - m_new).astype(BF16)
            acc_scr[mi] = acc_scr[mi] * jnp.exp2(m_prev - m_new) + _dot(p, values(mi, start, size))
            m_scr[mi] = m_new

    def sweep(step):
        step(0, n_ctx)

        @pl.when(qi > 0)
        def _():
            def body(i, carry):
                step(pl.multiple_of(n_ctx + i * ck, n_ctx), ck)
                return carry
            lax.fori_loop(0, (lk - n_ctx) // ck, body, 0)

    acc_scr[...] = jnp.zeros(acc_scr.shape, F32)
    bounded = scal_ref[0] != 0.0

    @pl.when(bounded)
    def _():
        sweep(step_bounded)

    @pl.when(jnp.logical_not(bounded))
    def _():
        m_scr[...] = jnp.full(m_scr.shape, -jnp.inf, F32)
        sweep(step_online)

    def normalised(mi):
        acc = acc_scr[mi]
        return acc * (1.0 / acc[:, V_ONE:V_ONE + 1])

    heads = []
    for a in range(2):
        if n_sub == 1:
            o = normalised(a)
        else:
            o = normalised(a * n_sub) - scal_ref[1] * normalised(a * n_sub + 1)
            ss = jnp.sum(jnp.where(lo, o * o, 0.0), axis=-1, keepdims=True)
            o = o * lax.rsqrt(ss * (1.0 / V_ONE) + EPS)
        heads.append(o)
    pair = jnp.where(lo, heads[0], pltpu.roll(heads[1], V_ONE, axis=1))
    o_ref[0] = pair if n_sub == 1 else pair * sub_ref[...] * post


def _flash(scal, q, k, v, subln, *, n_sub, n_ctx, post):
    b, l, qw = q.shape
    wq = qw // 2
    nq = l // TOK_TILE
    kern = functools.partial(_flash_kernel, n_sub=n_sub, n_ctx=n_ctx, ck=FLASH_KV_CHUNK, post=post)
    return pl.pallas_call(
        kern,
        grid=(b, 2, nq),
        in_specs=[pl.BlockSpec(memory_space=pltpu.SMEM),
                  pl.BlockSpec((1, TOK_TILE, wq), lambda i, h, j: (i, j, h)),
                  pl.BlockSpec((1, l, wq), lambda i, h, j: (i, 0, h)),
                  pl.BlockSpec((1, l, 2 * LANES), lambda i, h, j: (i, 0, h)),
                  pl.BlockSpec((1, LANES), lambda i, h, j: (0, 0))],
        out_specs=pl.BlockSpec((1, TOK_TILE, LANES), lambda i, h, j: (i, j, h)),
        out_shape=jax.ShapeDtypeStruct((b, l, 2 * LANES), F32),
        scratch_shapes=[pltpu.VMEM((2 * n_sub, TOK_TILE, LANES), F32),
                        pltpu.VMEM((2 * n_sub, TOK_TILE, 1), F32)],
        compiler_params=_cparams(("parallel", "parallel", "arbitrary")),
        name="flash_diff" if n_sub == 2 else "flash_mla",
    )(scal, q, k, v, subln)


def _ssd_dir(d, first, xbc_ref, prev_ref, next_ref, edge_lo, edge_hi, cw_ref, cb_ref, dtb_ref, a_ref, dsk_ref,
             y_ref, st_scr):
    t = xbc_ref.shape[1]
    raw = xbc_ref[0]
    x = raw[:, SSD_INNER:SSD_INNER + SSD_XBC]
    row = lax.broadcasted_iota(jnp.int32, (t, 1), 0)
    prev_row = jnp.where(edge_lo, 0.0, prev_ref[0, 7:8, SSD_INNER:SSD_INNER + SSD_XBC])
    next_row = jnp.where(edge_hi, 0.0, next_ref[0, 0:1, SSD_INNER:SSD_INNER + SSD_XBC])
    xm1 = jnp.where(row == 0, prev_row, pltpu.roll(x, 1, axis=0))
    xp1 = jnp.where(row == t - 1, next_row, pltpu.roll(x, t - 1, axis=0))
    xc = _silu(cw_ref[0:1] * xm1 + cw_ref[1:2] * x + cw_ref[2:3] * xp1 + cb_ref[...])
    xs = xc[:, :SSD_INNER]
    bm = xc[:, SSD_INNER:SSD_INNER + SSD_GN]
    cm = xc[:, SSD_INNER + SSD_GN:]

    z = raw[:, SSD_INNER + SSD_XBC:] + dtb_ref[...]
    dt = jnp.maximum(z, 0.0) + jnp.log1p(jnp.exp(-jnp.abs(z)))
    da = dt * a_ref[...]

    ri = lax.broadcasted_iota(jnp.int32, (t, t), 0)
    ci = lax.broadcasted_iota(jnp.int32, (t, t), 1)
    mask = (ci <= ri) if d == 0 else (ci >= ri)
    mask_t = (ri <= ci) if d == 0 else (ri >= ci)
    cum = _dot_hi(mask.astype(F32), da)
    cum_t = _dot_tn_hi(da, mask_t.astype(F32))
    tot = jnp.sum(da, axis=0, keepdims=True)

    lane = lax.broadcasted_iota(jnp.int32, (1, LANES), 1)
    lo = lane < (LANES // 2)

    @pl.when(first)
    def _():
        st_scr[d] = jnp.zeros(st_scr.shape[1:], F32)

    ys = []
    for g in range(SSD_GROUPS):
        gmask = lo if g == 0 else jnp.logical_not(lo)
        cg = jnp.where(gmask, cm, 0.0)
        gram = _dot_nt(cg.astype(BF16), bm.astype(BF16))
        xpair = xs[:, g * LANES:(g + 1) * LANES]
        yh = []
        for hh in range(2):
            hd = g * 2 + hh
            col = d * SSD_HEADS + hd
            cum_c = cum[:, col:col + 1]
            cum_r = cum_t[col:col + 1, :]
            dt_c = dt[:, col:col + 1]
            tot_c = tot[:, col:col + 1]
            decay = jnp.exp(jnp.where(mask, cum_c - cum_r, -jnp.inf))
            xdt = (xpair * dt_c).astype(BF16)
            st = st_scr[d, hd]
            y = (_dot((gram * decay).astype(BF16), xdt)
                 + _dot((cg * jnp.exp(cum_c)).astype(BF16), st.astype(BF16)))
            st_scr[d, hd] = st * jnp.exp(tot_c) + _dot_tn((bm * jnp.exp(tot_c - cum_c)).astype(BF16), xdt)
            yh.append(y)
        ys.append(jnp.where(lo, yh[0], yh[1]))
    y = jnp.concatenate(ys, axis=1)
    if d == 0:
        y = y + xs * dsk_ref[...]
    y_ref[0] = y


def _ssd_kernel(xf_ref, pf_ref, nf_ref, xb_ref, pb_ref, nb_ref, cw_ref, cb_ref, dtb_ref, a_ref, dsk_ref,
                yf_ref, yb_ref, st_scr, *, n_ctx_chunks):
    s = pl.program_id(1)
    nc = pl.num_programs(1)
    cf = s
    cb = jnp.where(s < n_ctx_chunks, n_ctx_chunks - 1 - s, nc - 1 + n_ctx_chunks - s)
    seg_lo = lambda c: (c == 0) | (c == n_ctx_chunks)
    seg_hi = lambda c: (c == n_ctx_chunks - 1) | (c == nc - 1)
    _ssd_dir(0, s == 0, xf_ref, pf_ref, nf_ref, seg_lo(cf), seg_hi(cf), cw_ref, cb_ref, dtb_ref, a_ref, dsk_ref,
             yf_ref, st_scr)
    _ssd_dir(1, s == 0, xb_ref, pb_ref, nb_ref, seg_lo(cb), seg_hi(cb), cw_ref, cb_ref, dtb_ref, a_ref, dsk_ref,
             yb_ref, st_scr)


def _ssd(raw, conv_w, conv_b, dt_bias, a_vec, dskip, n_ctx):
    b, l, w = raw.shape
    t = SSD_CHUNK
    nc = l // t
    ncc = n_ctx // t
    r8 = t // 8
    nb8 = l // 8
    fwd = lambda s: s
    bwd = lambda s: jnp.where(s < ncc, ncc - 1 - s, nc - 1 + ncc - s)
    chunk = lambda f: pl.BlockSpec((1, t, w), lambda i, s: (i, f(s), 0))
    prev = lambda f: pl.BlockSpec((1, 8, w), lambda i, s: (i, jnp.maximum(f(s) * r8 - 1, 0), 0))
    nxt = lambda f: pl.BlockSpec((1, 8, w), lambda i, s: (i, jnp.minimum((f(s) + 1) * r8, nb8 - 1), 0))
    full = lambda a: pl.BlockSpec(a.shape, lambda i, s: (0,) * a.ndim)
    out = lambda f: pl.BlockSpec((1, t, SSD_INNER), lambda i, s: (i, f(s), 0))
    return pl.pallas_call(
        functools.partial(_ssd_kernel, n_ctx_chunks=ncc),
        grid=(b, nc),
        in_specs=[chunk(fwd), prev(fwd), nxt(fwd), chunk(bwd), prev(bwd), nxt(bwd),
                  full(conv_w), full(conv_b), full(dt_bias), full(a_vec), full(dskip)],
        out_specs=[out(fwd), out(bwd)],
        out_shape=[jax.ShapeDtypeStruct((b, l, SSD_INNER), F32)] * 2,
        scratch_shapes=[pltpu.VMEM((2, SSD_HEADS, LANES, LANES), F32)],
        compiler_params=_cparams(("parallel", "arbitrary")),
        name="ssd",
    )(raw, raw, raw, raw, raw, raw, conv_w, conv_b, dt_bias, a_vec, dskip)


def _s5_local_kernel(u_ref, m_ref, w_ref, y_ref, s_ref):
    y_ref[0] = _dot_hi(u_ref[0], m_ref[0])
    s_ref[0] = _dot_hi(u_ref[0], w_ref[0])


def _s5_local(u, m, w):
    g, nch, wd = u.shape
    nw = wd
    ns = w.shape[2]
    return pl.pallas_call(
        _s5_local_kernel,
        grid=(g,),
        in_specs=[pl.BlockSpec((1, nch, wd), lambda i: (i, 0, 0)),
                  pl.BlockSpec((1, wd, nw), lambda i: (i, 0, 0)),
                  pl.BlockSpec((1, wd, ns), lambda i: (i, 0, 0))],
        out_specs=[pl.BlockSpec((1, nch, nw), lambda i: (i, 0, 0)),
                   pl.BlockSpec((1, nch, ns), lambda i: (0, 0, i))],
        out_shape=[jax.ShapeDtypeStruct((g, nch, nw), F32), jax.ShapeDtypeStruct((1, nch, g * ns), F32)],
        compiler_params=_cparams(("parallel",)),
        name="s5_local",
    )(u, m, w)


def _s5_scan_kernel(s_ref, ca_ref, cb_ref, o_ref, *, n_ctx_chunks):
    n = s_ref.shape[0]
    ca = ca_ref[...]
    cb = cb_ref[...]
    lane = lax.broadcasted_iota(jnp.int32, (1, 2 * LANES), 1)
    is_f = lane < LANES

    def swap(v):
        return jnp.concatenate([pltpu.roll(v[:, :LANES], LANES // 2, axis=1),
                                pltpu.roll(v[:, LANES:], LANES // 2, axis=1)], axis=1)

    def body(i, st):
        jf = i
        jb = jnp.where(i < n_ctx_chunks, n_ctx_chunks - 1 - i, n - 1 + n_ctx_chunks - i)
        cur_f = s_ref[jf]
        cur_b = s_ref[jb]
        o_ref[jf, :, :LANES] = st[:, :LANES]
        o_ref[jb, :, LANES:] = st[:, LANES:]
        return st * ca + swap(st) * cb + jnp.where(is_f, cur_f, cur_b)

    lax.fori_loop(0, n, body, jnp.zeros(ca.shape, F32))


def _s5_scan(s_loc, ca, cb, n_batch, n_ctx_chunks):
    nch, g, ns = s_loc.shape
    n = nch // n_batch
    return pl.pallas_call(
        functools.partial(_s5_scan_kernel, n_ctx_chunks=n_ctx_chunks),
        grid=(n_batch,),
        in_specs=[pl.BlockSpec((n, g, ns), lambda i: (i, 0, 0)),
                  pl.BlockSpec(ca.shape, lambda i: (0, 0)),
                  pl.BlockSpec(cb.shape, lambda i: (0, 0))],
        out_specs=pl.BlockSpec((n, g, ns), lambda i: (i, 0, 0)),
        out_shape=jax.ShapeDtypeStruct((nch, g, ns), F32),
        compiler_params=_cparams(("parallel",)),
        name="s5_scan",
    )(s_loc, ca, cb)


def _s5_read_kernel(y1_ref, sp_ref, v_ref, y_ref):
    y_ref[0] = y1_ref[0] + _dot_hi(sp_ref[0], v_ref[0])


def _s5_read(y1, s_prev, vmat):
    g, nch, nw = y1.shape
    ns = vmat.shape[1]
    return pl.pallas_call(
        _s5_read_kernel,
        grid=(g,),
        in_specs=[pl.BlockSpec((1, nch, nw), lambda i: (i, 0, 0)),
                  pl.BlockSpec((1, nch, ns), lambda i: (0, 0, i)),
                  pl.BlockSpec((1, ns, nw), lambda i: (i, 0, 0))],
        out_specs=pl.BlockSpec((1, nch, nw), lambda i: (i, 0, 0)),
        out_shape=jax.ShapeDtypeStruct((g, nch, nw), F32),
        compiler_params=_cparams(("parallel",)),
        name="s5_read",
    )(y1, s_prev, vmat)


def _s5_matrices(lam_re, lam_im, log_step, b_re, b_im, c_re, c_im):
    t = S5_CHUNK
    step = jnp.exp(log_step)[..., None]
    den = lam_re * lam_re + lam_im * lam_im
    er = jnp.exp(lam_re * step)
    ar, ai = er * jnp.cos(lam_im * step), er * jnp.sin(lam_im * step)
    nr, ni = ar - 1.0, ai
    fr = (nr * lam_re + ni * lam_im) / den
    fi = (ni * lam_re - nr * lam_im) / den
    bre, bim = b_re[:, None], b_im[:, None]
    br = fr[..., None] * bre - fi[..., None] * bim
    bi = fr[..., None] * bim + fi[..., None] * bre
    kk = jnp.arange(t + 1, dtype=F32)[:, None, None, None, None]
    ek = jnp.exp(kk * (lam_re * step)[None])
    pr, pi = ek * jnp.cos(kk * (lam_im * step)[None]), ek * jnp.sin(kk * (lam_im * step)[None])
    abr = pr[..., None] * br[None] - pi[..., None] * bi[None]
    abi = pr[..., None] * bi[None] + pi[..., None] * br[None]
    ein = functools.partial(jnp.einsum, precision=HIGHEST)
    kern = ein('dsgop,kdsgpj->kdsgoj', c_re, abr[:t]) - ein('dsgop,kdsgpj->kdsgoj', c_im, abi[:t])
    kf, kb = kern[:, :, 0], kern[:, :, 1]
    lag = jnp.concatenate([kb[:0:-1], (kf[0] + kb[0])[None], kf[1:]], axis=0)
    nd = lag.shape[1]
    flat = lag.transpose(1, 2, 4, 0, 3).reshape(nd, S5_GROUPS, S5_GROUP, (2 * t - 1) * S5_GROUP)
    m = jnp.stack([flat[..., (t - 1 - j) * S5_GROUP:(2 * t - 1 - j) * S5_GROUP] for j in range(t)], axis=2)
    m = m.reshape(nd, S5_GROUPS, t * S5_GROUP, t * S5_GROUP)
    wf = jnp.stack([abr[:t][::-1, :, 0], abi[:t][::-1, :, 0]], axis=0)
    wb = jnp.stack([abr[:t, :, 1], abi[:t, :, 1]], axis=0)
    w = jnp.concatenate([wf, wb], axis=0)
    w = w.transpose(2, 3, 1, 5, 0, 4).reshape(w.shape[2], S5_GROUPS, t * S5_GROUP, 4 * S5_STATE)
    cr, ci = c_re[None], c_im[None]
    vre = cr * pr[:, :, :, :, None, :] - ci * pi[:, :, :, :, None, :]
    vim = -(cr * pi[:, :, :, :, None, :] + ci * pr[:, :, :, :, None, :])
    vf = jnp.stack([vre[1:, :, 0], vim[1:, :, 0]], axis=0)
    vb = jnp.stack([vre[:0:-1, :, 1], vim[:0:-1, :, 1]], axis=0)
    v = jnp.concatenate([vf, vb], axis=0)
    v = v.transpose(2, 3, 0, 5, 1, 4).reshape(v.shape[2], S5_GROUPS, 4 * S5_STATE, t * S5_GROUP)
    atr, ati = pr[t], pi[t]
    ca = jnp.concatenate([atr[:, 0], atr[:, 0], atr[:, 1], atr[:, 1]], axis=-1)
    cb = jnp.concatenate([-ati[:, 0], ati[:, 0], -ati[:, 1], ati[:, 1]], axis=-1)
    return m, w, v, ca, cb


def _mix_kernel(x_ref, mod_ref, a_ref, yf_ref, yb_ref, z_ref, gs_ref, y5_ref, u5_ref, d5_ref, wg_ref, bg_ref,
                f_ref, wo_ref, o_ref):
    mod = mod_ref[0]
    ssd = _rms((yf_ref[0] + yb_ref[0]) * _silu(z_ref[0]), SSD_INNER) * gs_ref[...]
    y5 = y5_ref[0] + d5_ref[...] * u5_ref[0]
    g5 = jax.nn.gelu(y5)
    s5 = g5 * _sigmoid(_dot(g5.astype(BF16), wg_ref[...]) + bg_ref[...])
    acc = _dot(a_ref[0].astype(BF16), wo_ref[0:256, :])
    acc += _dot(ssd.astype(BF16), wo_ref[256:512, :])
    acc += _dot(s5.astype(BF16), wo_ref[512:768, :])
    acc += _dot(f_ref[0].astype(BF16), wo_ref[768:1024, :])
    o_ref[0] = x_ref[0] + mod[2:3] * acc


def _mix(x, modtab, a, yf, yb, ssd_raw, gs, y5, u5, d5, wglu, bglu, f, wo):
    b, l, d = x.shape
    nt = l // TOK_TILE
    tok = lambda c: pl.BlockSpec((1, TOK_TILE, c), lambda i, j: (i, j, 0))
    full = lambda arr: pl.BlockSpec(arr.shape, lambda i, j: (0,) * arr.ndim)
    return pl.pallas_call(
        _mix_kernel,
        grid=(b, nt),
        in_specs=[tok(d),
                  pl.BlockSpec((1, 8, d), lambda i, j: (2 * i + jnp.minimum(j, 1), 0, 0)),
                  tok(256), tok(256), tok(256), tok(256), full(gs), tok(256), tok(256), full(d5), full(wglu),
                  full(bglu), tok(256), full(wo)],
        out_specs=tok(d),
        out_shape=jax.ShapeDtypeStruct((b, l, d), F32),
        compiler_params=_cparams(("parallel", "arbitrary")),
        name="mix_out",
    )(x, modtab, a, yf, yb, ssd_raw, gs, y5, u5, d5, wglu, bglu, f, wo)


def _moe_kernel(x_ref, mod_ref, g2_ref, wr_ref, br_ref, wg_ref, wu_ref, wd_ref, o_ref, h_scr, gate_scr, acc_scr,
                *, n_ctx):
    j = pl.program_id(1)
    e = pl.program_id(2)
    tm = x_ref.shape[1]
    row = lax.broadcasted_iota(jnp.int32, (tm, 1), 0) + j * tm
    is_ctx = row < n_ctx
    lane = lax.broadcasted_iota(jnp.int32, (1, LANES), 1)

    def modrow(r):
        return jnp.where(is_ctx, mod_ref[0, r:r + 1, :], mod_ref[1, r:r + 1, :])

    @pl.when(e == 0)
    def _():
        x = x_ref[0]
        h = _rms(x, x.shape[-1]) * g2_ref[...]
        h = h * (1.0 + modrow(4)) + modrow(3)
        h_scr[...] = h.astype(BF16)
        logits = _dot_hi(h, wr_ref[...]) + br_ref[...]
        neg = -jnp.inf
        lanef = lane.astype(F32)
        big = float(4 * LANES)
        first = lambda hit: jnp.min(jnp.where(hit, lanef, big), axis=-1, keepdims=True)
        glog = jnp.where(lane < MOE_GROUPS, logits, neg)
        gmax = jnp.max(glog, axis=-1, keepdims=True)
        gidx = first(glog == gmax)
        p_group = 1.0 / jnp.sum(jnp.exp(glog - gmax), axis=-1, keepdims=True)
        e0 = MOE_GROUPS + gidx * MOE_PER_GROUP
        elog = jnp.where((lanef >= e0) & (lanef < e0 + MOE_PER_GROUP), logits, neg)
        v1 = jnp.max(elog, axis=-1, keepdims=True)
        i1 = first(elog == v1)
        elog2 = jnp.where(lanef == i1, neg, elog)
        v2 = jnp.max(elog2, axis=-1, keepdims=True)
        i2 = first(elog2 == v2)
        e2 = jnp.exp(v2 - v1)
        w1 = p_group / (1.0 + e2)
        w2 = p_group * e2 / (1.0 + e2)
        gate_scr[...] = jnp.where(lanef == i1, w1, 0.0) + jnp.where(lanef == i2, w2, 0.0)
        acc_scr[...] = jnp.zeros(acc_scr.shape, F32)

    h = h_scr[...]
    gate = jnp.sum(jnp.where(lane == MOE_GROUPS + e, gate_scr[...], 0.0), axis=-1, keepdims=True)
    hid = _silu(_dot(h, wg_ref[0])) * _dot(h, wu_ref[0]) * gate
    acc_scr[...] += _dot(hid.astype(BF16), wd_ref[0])

    @pl.when(e == pl.num_programs(2) - 1)
    def _():
        o_ref[0] = x_ref[0] + modrow(5) * acc_scr[...]


def _moe(x, modtab4, g2, wr, br, wg, wu, wd, n_ctx):
    b, l, d = x.shape
    tm = MOE_TILE
    nt = l // tm
    ne = wg.shape[0]
    full = lambda arr: pl.BlockSpec(arr.shape, lambda i, j, e: (0,) * arr.ndim)
    return pl.pallas_call(
        functools.partial(_moe_kernel, n_ctx=n_ctx),
        grid=(b, nt, ne),
        in_specs=[pl.BlockSpec((1, tm, d), lambda i, j, e: (i, j, 0)),
                  pl.BlockSpec((2, 8, d), lambda i, j, e: (i, 0, 0)),
                  full(g2), full(wr), full(br),
                  pl.BlockSpec((1, d, EXPERT_HIDDEN), lambda i, j, e: (e, 0, 0)),
                  pl.BlockSpec((1, d, EXPERT_HIDDEN), lambda i, j, e: (e, 0, 0)),
                  pl.BlockSpec((1, EXPERT_HIDDEN, d), lambda i, j, e: (e, 0, 0))],
        out_specs=pl.BlockSpec((1, tm, d), lambda i, j, e: (i, j, 0)),
        out_shape=jax.ShapeDtypeStruct((b, l, d), F32),
        scratch_shapes=[pltpu.VMEM((tm, d), BF16), pltpu.VMEM((tm, LANES), F32), pltpu.VMEM((tm, d), F32)],
        compiler_params=_cparams(("parallel", "parallel", "arbitrary")),
        name="moe",
    )(x, modtab4, g2, wr, br, wg, wu, wd)


def _pad_in_proj(w_in):
    depth, d, _ = w_in.shape
    out = jnp.zeros((depth, d, P_COLS), w_in.dtype)
    o = 0
    for hd in range(MLA_HEADS):
        out = out.at[:, :, P_MLA_Q + hd * LANES:P_MLA_Q + hd * LANES + MLA_QK].set(w_in[:, :, o:o + MLA_QK])
        o += MLA_QK
    out = out.at[:, :, P_MLA_CKV:P_MLA_CKV + MLA_KV_RANK].set(w_in[:, :, o:o + MLA_KV_RANK])
    o += MLA_KV_RANK
    out = out.at[:, :, P_MLA_KR + MLA_NOPE:P_MLA_KR + MLA_QK].set(w_in[:, :, o:o + MLA_ROPE])
    o += MLA_ROPE
    n = SSD_INNER + SSD_XBC + 2 * SSD_HEADS
    out = out.at[:, :, P_SSD:P_SSD + n].set(w_in[:, :, o:o + n])
    o += n
    out = out.at[:, :, P_S5:P_S5 + S5_COLS].set(w_in[:, :, o:o + S5_COLS])
    o += S5_COLS
    out = out.at[:, :, P_DIFF:P_DIFF + 512].set(w_in[:, :, o:o + 512])
    o += 512
    for hd in range(DIFF_HEADS):
        c0 = P_DIFF + 512 + hd * LANES
        out = out.at[:, :, c0:c0 + DIFF_V].set(w_in[:, :, o:o + DIFF_V])
        o += DIFF_V
    return out


def _rope_tables(seq, n_ctx):
    pos = jnp.arange(seq, dtype=jnp.int32)
    rows = (pos // GRID_W).astype(F32)
    cols = (pos % GRID_W).astype(F32)
    half = MLA_ROPE // 2
    inv = ROPE_BASE ** (-jnp.arange(0, half, 2, dtype=F32) / half)
    r = jnp.arange(MLA_ROPE)
    freq = inv[r % (half // 2)]
    ang = jnp.where((r < half)[None, :], rows[:, None], cols[:, None]) * freq[None, :]
    cos, sin = jnp.cos(ang), jnp.sin(ang)
    first = ((r % half) < half // 2)[None, :]
    up = jnp.where(first, -sin, 0.0)
    dn = jnp.where(first, 0.0, sin)
    t32 = jnp.stack([cos, up, dn])
    ident = jnp.stack([jnp.ones((n_ctx, MLA_ROPE), F32), jnp.zeros((n_ctx, MLA_ROPE), F32),
                       jnp.zeros((n_ctx, MLA_ROPE), F32)])
    t32 = jnp.concatenate([ident, t32], axis=1)
    n = t32.shape[1]
    base = jnp.stack([jnp.ones((n, LANES), F32), jnp.zeros((n, LANES), F32), jnp.zeros((n, LANES), F32)])
    rope_m = base.at[:, :, MLA_NOPE:MLA_QK].set(t32)
    rope_d = jnp.tile(t32, (1, 1, LANES // MLA_ROPE))
    return rope_m, rope_d


def _pad_lanes(v, n):
    return jnp.zeros((1, n), F32).at[0, :v.shape[0]].set(v)


def kernel(x, c, ctx, c_ctx, w_ada, b_ada, norm1, norm2, w_in, w_out, mla_kv_norm, mla_w_uk, mla_w_uv, mla_q_norm, mla_k_norm, ssd_conv_w, ssd_conv_b, ssd_a_log, ssd_dt_bias, ssd_d, ssd_norm, s5_lam_re, s5_lam_im, s5_log_step, s5_b_re, s5_b_im, s5_c_re, s5_c_im, s5_d, s5_w_glu, s5_b_glu, diff_q_norm, diff_k_norm, diff_lq1, diff_lk1, diff_lq2, diff_lk2, diff_subln, moe_w_group, moe_b_group, moe_w_expert, moe_b_expert, moe_w_gate, moe_w_up, moe_w_down):
    bsz, seq, d = x.shape
    n_ctx = ctx.shape[1]
    depth = w_in.shape[0]
    assert n_ctx == TOK_TILE and seq % FLASH_KV_CHUNK == 0 and d == 1024
    l = n_ctx + seq
    assert l % MOE_TILE == 0 and l % SSD_CHUNK == 0 and n_ctx % SSD_CHUNK == 0

    cvec = jnp.zeros((8, d), F32).at[:bsz].set(c).at[bsz].set(c_ctx)
    ada = _ada(cvec, w_ada, b_ada).reshape(depth, 8, N_MOD, d)
    mod_lat = ada[:, :bsz]
    mod_ctx = jnp.broadcast_to(ada[:, bsz][:, None], mod_lat.shape)
    modtab = jnp.stack([mod_ctx, mod_lat], axis=2)
    modtab = jnp.pad(modtab, ((0, 0), (0, 0), (0, 0), (0, 8 - N_MOD), (0, 0))).reshape(depth, bsz * 2, 8, d)

    w_in_p = _pad_in_proj(w_in).astype(BF16)
    w_out_b = w_out.astype(BF16)
    rope_m, rope_d = _rope_tables(seq, n_ctx)
    wuk = jnp.zeros((depth, MLA_KV_RANK, MLA_HEADS * LANES), F32)
    for hd in range(MLA_HEADS):
        wuk = wuk.at[:, :, hd * LANES:hd * LANES + MLA_NOPE].set(mla_w_uk[:, :, hd * MLA_NOPE:(hd + 1) * MLA_NOPE])
    wuk = wuk.astype(BF16)
    wuv = jnp.zeros((depth, MLA_KV_RANK, MLA_HEADS * LANES), F32)
    for hd in range(MLA_HEADS):
        wuv = wuv.at[:, :, hd * LANES:hd * LANES + MLA_V].set(mla_w_uv[:, :, hd * MLA_V:(hd + 1) * MLA_V])
    wuv = wuv.astype(BF16)
    amax = lambda g: jnp.max(jnp.abs(g.astype(F32)), axis=-1)
    bounded_m = (math.sqrt(MLA_QK) * amax(mla_q_norm) * amax(mla_k_norm) <= SCORE_BOUND).astype(F32)
    bounded_d = (math.sqrt(DIFF_HEAD) * amax(diff_q_norm) * amax(diff_k_norm) <= SCORE_BOUND).astype(F32)
    lane = jnp.arange(LANES)
    seg = (lane[:, None] // DIFF_HEAD == lane[None, :] // DIFF_HEAD).astype(F32)
    a_vec = -jnp.exp(ssd_a_log.astype(F32)).reshape(depth, 2 * SSD_HEADS)
    s5_m, s5_w, s5_v, s5_ca, s5_cb = _s5_matrices(
        s5_lam_re.astype(F32), s5_lam_im.astype(F32), s5_log_step.astype(F32),
        s5_b_re.astype(F32), s5_b_im.astype(F32), s5_c_re.astype(F32), s5_c_im.astype(F32))
    w_router = jnp.concatenate([moe_w_group, moe_w_expert.transpose(0, 2, 1, 3).reshape(depth, d, N_EXPERTS)], axis=-1)
    w_router = jnp.pad(w_router, ((0, 0), (0, 0), (0, LANES - w_router.shape[-1])))
    b_router = jnp.concatenate([moe_b_group, moe_b_expert.reshape(depth, N_EXPERTS)], axis=-1)
    b_router = jnp.pad(b_router, ((0, 0), (0, LANES - b_router.shape[-1])))
    wg_b, wu_b, wd_b = moe_w_gate.astype(BF16), moe_w_up.astype(BF16), moe_w_down.astype(BF16)
    lam_all = (jnp.exp(jnp.sum(diff_lq1.astype(F32) * diff_lk1.astype(F32), axis=-1))
               - jnp.exp(jnp.sum(diff_lq2.astype(F32) * diff_lk2.astype(F32), axis=-1)))

    n5 = l // S5_CHUNK
    xs = jnp.concatenate([ctx, x], axis=1)
    for i in range(depth):
        lam_init = 0.8 - 0.6 * math.exp(-0.3 * i)
        qm, km, vm, qd, kd, vd, ssd_raw, u5 = _front(
            xs, modtab[i], norm1[i][None], w_in_p[i], rope_m, rope_d,
            _pad_lanes(mla_q_norm[i], LANES), _pad_lanes(mla_k_norm[i], LANES), mla_kv_norm[i][None],
            wuk[i], wuv[i], jnp.tile(diff_q_norm[i], 4)[None], jnp.tile(diff_k_norm[i], 4)[None], seg)
        a = _flash(jnp.stack([bounded_m[i], jnp.zeros((), F32)]), qm, km, vm, jnp.ones((1, LANES), F32),
                   n_sub=1, n_ctx=n_ctx, post=1.0)
        f = _flash(jnp.stack([bounded_d[i], lam_all[i] + lam_init]), qd, kd, vd, jnp.tile(diff_subln[i], 2)[None],
                   n_sub=2, n_ctx=n_ctx, post=1.0 - lam_init)
        yf, yb = _ssd(ssd_raw, ssd_conv_w[i], ssd_conv_b[i][None],
                      _pad_lanes(ssd_dt_bias[i].reshape(-1), LANES), _pad_lanes(a_vec[i], LANES),
                      jnp.repeat(ssd_d[i], SSD_HEAD_DIM)[None], n_ctx)
        u5g = u5.reshape(bsz, n5, S5_CHUNK, S5_GROUPS, S5_GROUP).transpose(3, 0, 1, 2, 4)
        u5g = u5g.reshape(S5_GROUPS, bsz * n5, S5_CHUNK * S5_GROUP)
        y1, s_loc = _s5_local(u5g, s5_m[i], s5_w[i])
        s_loc = s_loc.reshape(bsz * n5, S5_GROUPS, 4 * S5_STATE)
        s_prev = _s5_scan(s_loc, s5_ca[i], s5_cb[i], bsz, n_ctx // S5_CHUNK)
        y5 = _s5_read(y1, s_prev.reshape(1, bsz * n5, S5_GROUPS * 4 * S5_STATE), s5_v[i])
        y5 = y5.reshape(S5_GROUPS, bsz, n5, S5_CHUNK, S5_GROUP).transpose(1, 2, 3, 0, 4).reshape(bsz, l, S5_COLS)
        xs = _mix(xs, modtab[i], a, yf, yb, ssd_raw, ssd_norm[i][None], y5, u5, s5_d[i][None],
                  s5_w_glu[i].astype(BF16), s5_b_glu[i][None], f, w_out_b[i])
        xs = _moe(xs, modtab[i], norm2[i][None], w_router[i], b_router[i][None], wg_b[i], wu_b[i], wd_b[i], n_ctx)
    return xs[:, n_ctx:]
```

```python
import functools
import math

import jax
import jax.numpy as jnp
from jax import lax
from jax.experimental import pallas as pl
from jax.experimental.pallas import tpu as pltpu

F32 = jnp.float32
BF16 = jnp.bfloat16
HIGHEST = lax.Precision.HIGHEST

LANES = 128
EPS = 1e-6
ROPE_BASE = 10000.0
GRID_W = 64
N_MOD = 6

GROUP_WIDTH = 256
MLA_HEADS, MLA_NOPE, MLA_ROPE, MLA_V, MLA_KV_RANK = 4, 64, 32, 64, 128
MLA_QK = MLA_NOPE + MLA_ROPE
SSD_HEADS, SSD_HEAD_DIM, SSD_GROUPS, SSD_STATE, SSD_INNER = 4, 64, 2, 64, 256
SSD_GN = SSD_GROUPS * SSD_STATE
SSD_XBC = SSD_INNER + 2 * SSD_GN
S5_GROUPS, S5_GROUP, S5_STATE = 16, 16, 64
DIFF_HEADS, DIFF_HEAD, DIFF_V = 4, 32, 64
MOE_GROUPS, MOE_PER_GROUP, N_EXPERTS, EXPERT_HIDDEN = 4, 8, 32, 256

MLA_COLS = MLA_HEADS * MLA_QK + MLA_KV_RANK + MLA_ROPE
SSD_COLS = SSD_INNER + SSD_XBC + 2 * SSD_HEADS
S5_COLS = 256
DIFF_COLS = 768

P_MLA_Q = 0
P_MLA_CKV = 512
P_MLA_KR = 640
P_SSD = 768
P_S5 = 1664
P_DIFF = 1920
P_COLS = 2944
V_ONE = 64
SCORE_BOUND = 40.0
LOG2E = 1.4426950408889634

TOK_TILE = 256
SSD_CHUNK = 256
S5_CHUNK = 32
FLASH_KV_CHUNK = 1024
MOE_TILE = 512
VMEM_LIMIT = 56 * 1024 * 1024


def _cparams(sem):
    return pltpu.CompilerParams(dimension_semantics=sem, vmem_limit_bytes=VMEM_LIMIT)


def _sigmoid(x):
    return 1.0 / (1.0 + jnp.exp(-x))


def _silu(x):
    return x * _sigmoid(x)


def _rms(x, n):
    return x * lax.rsqrt(jnp.sum(x * x, axis=-1, keepdims=True) * (1.0 / n) + EPS)


def _dot(a, b):
    return jnp.dot(a, b, preferred_element_type=F32)


def _dot_hi(a, b):
    return jnp.dot(a, b, preferred_element_type=F32, precision=HIGHEST)


def _dot_nt(a, b):
    return lax.dot_general(a, b, (((1,), (1,)), ((), ())), preferred_element_type=F32)


def _dot_tn(a, b):
    return lax.dot_general(a, b, (((0,), (0,)), ((), ())), preferred_element_type=F32)


def _dot_tn_hi(a, b):
    return lax.dot_general(a, b, (((0,), (0,)), ((), ())), preferred_element_type=F32, precision=HIGHEST)


def _ada_kernel(c_ref, w_ref, b_ref, o_ref):
    o_ref[0] = _dot_hi(_silu(c_ref[...]), w_ref[0]) + b_ref[0]


def _ada(cvec, w_ada, b_ada):
    depth, d, nd = w_ada.shape
    blk = 1024
    return pl.pallas_call(
        _ada_kernel,
        grid=(depth, nd // blk),
        in_specs=[pl.BlockSpec((8, d), lambda i, j: (0, 0)),
                  pl.BlockSpec((1, d, blk), lambda i, j: (i, 0, j)),
                  pl.BlockSpec((1, 1, blk), lambda i, j: (i, 0, j))],
        out_specs=pl.BlockSpec((1, 8, blk), lambda i, j: (i, 0, j)),
        out_shape=jax.ShapeDtypeStruct((depth, 8, nd), F32),
        compiler_params=_cparams(("arbitrary", "arbitrary")),
        name="ada",
    )(cvec, w_ada, b_ada.reshape(depth, 1, nd))


def _rope(x, cos, sin_up, sin_dn):
    n = x.shape[-1]
    return x * cos + pltpu.roll(x, n - 8, axis=1) * sin_up + pltpu.roll(x, 8, axis=1) * sin_dn


def _front_kernel(x_ref, mod_ref, g1_ref, w_ref, rm_ref, rd_ref, gq_ref, gk_ref, gkv_ref, wuk_ref, wuv_ref,
                  gdq_ref, gdk_ref, seg_ref, one_ref,
                  qm_ref, km_ref, vm_ref, qd_ref, kd_ref, vd_ref, ssd_ref, s5_ref):
    x = x_ref[...]
    mod = mod_ref[0]
    h = _rms(x, x.shape[-1]) * g1_ref[...]
    h = h * (1.0 + mod[1:2]) + mod[0:1]
    p = _dot(h.astype(BF16), w_ref[...])

    ssd_ref[0] = p[:, P_SSD:P_S5]
    s5_ref[0] = p[:, P_S5:P_DIFF]

    cos_m, up_m, dn_m = rm_ref[0], rm_ref[1], rm_ref[2]
    ckv = _rms(p[:, P_MLA_CKV:P_MLA_KR], MLA_KV_RANK) * gkv_ref[...]
    ckv = ckv.astype(BF16)
    k_nope = _dot(ckv, wuk_ref[...])
    vm_ref[0] = (_dot(ckv, wuv_ref[...]) + one_ref[...]).astype(BF16)
    k_rope = p[:, P_MLA_KR:P_SSD]
    q_scale = MLA_QK ** -0.5 * LOG2E
    for hd in range(MLA_HEADS):
        sl = slice(hd * LANES, (hd + 1) * LANES)
        q = _rms(p[:, sl], MLA_QK) * gq_ref[...]
        qm_ref[0, :, sl] = (_rope(q, cos_m, up_m, dn_m) * q_scale).astype(BF16)
        k = _rms(k_nope[:, sl] + k_rope, MLA_QK) * gk_ref[...]
        km_ref[0, :, sl] = _rope(k, cos_m, up_m, dn_m).astype(BF16)

    cos_d, up_d, dn_d = rd_ref[0], rd_ref[1], rd_ref[2]
    d_scale = DIFF_HEAD ** -0.5 * LOG2E
    for blk, (g_ref, o_ref, scale) in enumerate(((gdq_ref, qd_ref, d_scale), (gdk_ref, kd_ref, 1.0))):
        for half in range(2):
            sl = slice(P_DIFF + blk * 256 + half * LANES, P_DIFF + blk * 256 + (half + 1) * LANES)
            t = p[:, sl]
            ss = _dot_hi(t * t, seg_ref[...])
            t = t * lax.rsqrt(ss * (1.0 / DIFF_HEAD) + EPS) * g_ref[...]
            o_ref[0, :, half * LANES:(half + 1) * LANES] = (_rope(t, cos_d, up_d, dn_d) * scale).astype(BF16)
    vd_ref[0] = (p[:, P_DIFF + 512:P_COLS] + one_ref[...]).astype(BF16)


def _front(x, b, layer, modtab, g1, w_in_p, rope_m, rope_d, gq, gk, gkv, wuk, wuv, gdq, gdk, seg):
    l, d = rope_m.shape[1], x.shape[1]
    nt = l // TOK_TILE
    tok = lambda c: pl.BlockSpec((1, TOK_TILE, c), lambda i, j: (i, j, 0))
    full = lambda a: pl.BlockSpec(a.shape, lambda i, j: (0,) * a.ndim)
    rope_spec = pl.BlockSpec((3, TOK_TILE, LANES), lambda i, j: (0, j, 0))
    one = (jnp.arange(4 * LANES) % LANES == V_ONE).astype(F32)[None]
    outs = [(512, BF16), (512, BF16), (512, BF16), (256, BF16), (256, BF16), (512, BF16),
            (P_S5 - P_SSD, F32), (256, F32)]
    return pl.pallas_call(
        _front_kernel,
        grid=(b, nt),
        in_specs=[pl.BlockSpec((TOK_TILE, d), lambda i, j: (i * nt + j, 0)),
                  pl.BlockSpec((1, 8, d), lambda i, j: (2 * i + jnp.minimum(j, 1), 0, 0)),
                  full(g1), pl.BlockSpec((None,) + w_in_p.shape[1:], lambda i, j: (layer, 0, 0)), rope_spec, rope_spec,
                  full(gq), full(gk), full(gkv), full(wuk), full(wuv), full(gdq), full(gdk), full(seg), full(one)],
        out_specs=[tok(c) for c, _ in outs],
        out_shape=[jax.ShapeDtypeStruct((b, l, c), dt) for c, dt in outs],
        compiler_params=_cparams(("parallel", "arbitrary")),
        name="front",
    )(x, modtab, g1, w_in_p, rope_m, rope_d, gq, gk, gkv, wuk, wuv, gdq, gdk, seg, one)


def _flash_kernel(scal_ref, q_ref, k_ref, v_ref, sub_ref, o_ref, acc_scr, m_scr, *, n_sub, n_ctx, ck, post):
    qi = pl.program_id(2)
    lk = k_ref.shape[1]
    lane = lax.broadcasted_iota(jnp.int32, (1, LANES), 1)
    lo = lane < V_ONE
    q = q_ref[0]
    maps = [(a, c) for a in range(2) for c in range(n_sub)]

    def q_of(a, c):
        if n_sub == 1:
            return q[:, a * LANES:(a + 1) * LANES]
        s = (a * n_sub + c) * DIFF_HEAD
        return jnp.where((lane >= s) & (lane < s + DIFF_HEAD), q, jnp.zeros_like(q))

    qs = [q_of(a, c) for a, c in maps]

    def scores(mi, start, size):
        a = maps[mi][0]
        kk = k_ref[0, pl.ds(start, size), a * LANES:(a + 1) * LANES] if n_sub == 1 else k_ref[0, pl.ds(start, size), :]
        return _dot_nt(qs[mi], kk)

    def values(mi, start, size):
        a = maps[mi][0]
        return v_ref[0, pl.ds(start, size), a * LANES:(a + 1) * LANES]

    def step_bounded(start, size):
        for mi in range(len(maps)):
            p = jnp.exp2(scores(mi, start, size)).astype(BF16)
            acc_scr[mi] += _dot(p, values(mi, start, size))

    def step_online(start, size):
        for mi in range(len(maps)):
            s = scores(mi, start, size)
            m_prev = m_scr[mi]
            m_new = jnp.maximum(m_prev, jnp.max(s, axis=-1, keepdims=True))
            p = jnp.exp2(s - m_new).astype(BF16)
            acc_scr[mi] = acc_scr[mi] * jnp.exp2(m_prev - m_new) + _dot(p, values(mi, start, size))
            m_scr[mi] = m_new

    def sweep(step):
        step(0, n_ctx)

        @pl.when(qi > 0)
        def _():
            def body(i, carry):
                step(pl.multiple_of(n_ctx + i * ck, n_ctx), ck)
                return carry
            lax.fori_loop(0, (lk - n_ctx) // ck, body, 0)

    acc_scr[...] = jnp.zeros(acc_scr.shape, F32)
    bounded = scal_ref[0] != 0.0

    @pl.when(bounded)
    def _():
        sweep(step_bounded)

    @pl.when(jnp.logical_not(bounded))
    def _():
        m_scr[...] = jnp.full(m_scr.shape, -jnp.inf, F32)
        sweep(step_online)

    def normalised(mi):
        acc = acc_scr[mi]
        return acc * (1.0 / acc[:, V_ONE:V_ONE + 1])

    heads = []
    for a in range(2):
        if n_sub == 1:
            o = normalised(a)
        else:
            o = normalised(a * n_sub) - scal_ref[1] * normalised(a * n_sub + 1)
            ss = jnp.sum(jnp.where(lo, o * o, 0.0), axis=-1, keepdims=True)
            o = o * lax.rsqrt(ss * (1.0 / V_ONE) + EPS)
        heads.append(o)
    pair = jnp.where(lo, heads[0], pltpu.roll(heads[1], V_ONE, axis=1))
    o_ref[0] = pair if n_sub == 1 else pair * sub_ref[...] * post


def _flash(scal, q, k, v, subln, *, n_sub, n_ctx, post):
    b, l, qw = q.shape
    wq = qw // 2
    nq = l // TOK_TILE
    kern = functools.partial(_flash_kernel, n_sub=n_sub, n_ctx=n_ctx, ck=FLASH_KV_CHUNK, post=post)
    return pl.pallas_call(
        kern,
        grid=(b, 2, nq),
        in_specs=[pl.BlockSpec(memory_space=pltpu.SMEM),
                  pl.BlockSpec((1, TOK_TILE, wq), lambda i, h, j: (i, j, h)),
                  pl.BlockSpec((1, l, wq), lambda i, h, j: (i, 0, h)),
                  pl.BlockSpec((1, l, 2 * LANES), lambda i, h, j: (i, 0, h)),
                  pl.BlockSpec((1, LANES), lambda i, h, j: (0, 0))],
        out_specs=pl.BlockSpec((1, TOK_TILE, LANES), lambda i, h, j: (i, j, h)),
        out_shape=jax.ShapeDtypeStruct((b, l, 2 * LANES), F32),
        scratch_shapes=[pltpu.VMEM((2 * n_sub, TOK_TILE, LANES), F32),
                        pltpu.VMEM((2 * n_sub, TOK_TILE, 1), F32)],
        compiler_params=_cparams(("parallel", "parallel", "arbitrary")),
        name="flash_diff" if n_sub == 2 else "flash_mla",
    )(scal, q, k, v, subln)


def _ssd_dir(d, first, xbc_ref, prev_ref, next_ref, edge_lo, edge_hi, cw_ref, cb_ref, dtb_ref, a_ref, dsk_ref,
             y_ref, st_scr):
    t = xbc_ref.shape[1]
    raw = xbc_ref[0]
    x = raw[:, SSD_INNER:SSD_INNER + SSD_XBC]
    row = lax.broadcasted_iota(jnp.int32, (t, 1), 0)
    prev_row = jnp.where(edge_lo, 0.0, prev_ref[0, 7:8, SSD_INNER:SSD_INNER + SSD_XBC])
    next_row = jnp.where(edge_hi, 0.0, next_ref[0, 0:1, SSD_INNER:SSD_INNER + SSD_XBC])
    xm1 = jnp.where(row == 0, prev_row, pltpu.roll(x, 1, axis=0))
    xp1 = jnp.where(row == t - 1, next_row, pltpu.roll(x, t - 1, axis=0))
    xc = _silu(cw_ref[0:1] * xm1 + cw_ref[1:2] * x + cw_ref[2:3] * xp1 + cb_ref[...])
    xs = xc[:, :SSD_INNER]
    bm = xc[:, SSD_INNER:SSD_INNER + SSD_GN]
    cm = xc[:, SSD_INNER + SSD_GN:]

    z = raw[:, SSD_INNER + SSD_XBC:] + dtb_ref[...]
    dt = jnp.maximum(z, 0.0) + jnp.log1p(jnp.exp(-jnp.abs(z)))
    da = dt * a_ref[...]

    ri = lax.broadcasted_iota(jnp.int32, (t, t), 0)
    ci = lax.broadcasted_iota(jnp.int32, (t, t), 1)
    mask = (ci <= ri) if d == 0 else (ci >= ri)
    mask_t = (ri <= ci) if d == 0 else (ri >= ci)
    cum = _dot_hi(mask.astype(F32), da)
    cum_t = _dot_tn_hi(da, mask_t.astype(F32))
    tot = jnp.sum(da, axis=0, keepdims=True)

    lane = lax.broadcasted_iota(jnp.int32, (1, LANES), 1)
    lo = lane < (LANES // 2)

    @pl.when(first)
    def _():
        st_scr[d] = jnp.zeros(st_scr.shape[1:], F32)

    ys = []
    for g in range(SSD_GROUPS):
        gmask = lo if g == 0 else jnp.logical_not(lo)
        cg = jnp.where(gmask, cm, 0.0)
        gram = _dot_nt(cg.astype(BF16), bm.astype(BF16))
        xpair = xs[:, g * LANES:(g + 1) * LANES]
        yh = []
        for hh in range(2):
            hd = g * 2 + hh
            col = d * SSD_HEADS + hd
            cum_c = cum[:, col:col + 1]
            cum_r = cum_t[col:col + 1, :]
            dt_c = dt[:, col:col + 1]
            tot_c = tot[:, col:col + 1]
            decay = jnp.exp(jnp.where(mask, cum_c - cum_r, -jnp.inf))
            xdt = (xpair * dt_c).astype(BF16)
            st = st_scr[d, hd]
            y = (_dot((gram * decay).astype(BF16), xdt)
                 + _dot((cg * jnp.exp(cum_c)).astype(BF16), st.astype(BF16)))
            st_scr[d, hd] = st * jnp.exp(tot_c) + _dot_tn((bm * jnp.exp(tot_c - cum_c)).astype(BF16), xdt)
            yh.append(y)
        ys.append(jnp.where(lo, yh[0], yh[1]))
    y = jnp.concatenate(ys, axis=1)
    if d == 0:
        y = y + xs * dsk_ref[...]
    y_ref[0] = y


def _ssd_kernel(xf_ref, pf_ref, nf_ref, xb_ref, pb_ref, nb_ref, cw_ref, cb_ref, dtb_ref, a_ref, dsk_ref,
                yf_ref, yb_ref, st_scr, *, n_ctx_chunks):
    s = pl.program_id(1)
    nc = pl.num_programs(1)
    cf = s
    cb = jnp.where(s < n_ctx_chunks, n_ctx_chunks - 1 - s, nc - 1 + n_ctx_chunks - s)
    seg_lo = lambda c: (c == 0) | (c == n_ctx_chunks)
    seg_hi = lambda c: (c == n_ctx_chunks - 1) | (c == nc - 1)
    _ssd_dir(0, s == 0, xf_ref, pf_ref, nf_ref, seg_lo(cf), seg_hi(cf), cw_ref, cb_ref, dtb_ref, a_ref, dsk_ref,
             yf_ref, st_scr)
    _ssd_dir(1, s == 0, xb_ref, pb_ref, nb_ref, seg_lo(cb), seg_hi(cb), cw_ref, cb_ref, dtb_ref, a_ref, dsk_ref,
             yb_ref, st_scr)


def _ssd(raw, conv_w, conv_b, dt_bias, a_vec, dskip, n_ctx):
    b, l, w = raw.shape
    t = SSD_CHUNK
    nc = l // t
    ncc = n_ctx // t
    r8 = t // 8
    nb8 = l // 8
    fwd = lambda s: s
    bwd = lambda s: jnp.where(s < ncc, ncc - 1 - s, nc - 1 + ncc - s)
    chunk = lambda f: pl.BlockSpec((1, t, w), lambda i, s: (i, f(s), 0))
    prev = lambda f: pl.BlockSpec((1, 8, w), lambda i, s: (i, jnp.maximum(f(s) * r8 - 1, 0), 0))
    nxt = lambda f: pl.BlockSpec((1, 8, w), lambda i, s: (i, jnp.minimum((f(s) + 1) * r8, nb8 - 1), 0))
    full = lambda a: pl.BlockSpec(a.shape, lambda i, s: (0,) * a.ndim)
    out = lambda f: pl.BlockSpec((1, t, SSD_INNER), lambda i, s: (i, f(s), 0))
    return pl.pallas_call(
        functools.partial(_ssd_kernel, n_ctx_chunks=ncc),
        grid=(b, nc),
        in_specs=[chunk(fwd), prev(fwd), nxt(fwd), chunk(bwd), prev(bwd), nxt(bwd),
                  full(conv_w), full(conv_b), full(dt_bias), full(a_vec), full(dskip)],
        out_specs=[out(fwd), out(bwd)],
        out_shape=[jax.ShapeDtypeStruct((b, l, SSD_INNER), F32)] * 2,
        scratch_shapes=[pltpu.VMEM((2, SSD_HEADS, LANES, LANES), F32)],
        compiler_params=_cparams(("parallel", "arbitrary")),
        name="ssd",
    )(raw, raw, raw, raw, raw, raw, conv_w, conv_b, dt_bias, a_vec, dskip)


def _s5_local_kernel(u_ref, m_ref, w_ref, y_ref, s_ref):
    y_ref[0] = _dot_hi(u_ref[0], m_ref[0])
    s_ref[0] = _dot_hi(u_ref[0], w_ref[0])


def _s5_local(u, m, w, layer):
    g, nch, wd = u.shape
    nw = wd
    ns = w.shape[3]
    return pl.pallas_call(
        _s5_local_kernel,
        grid=(g,),
        in_specs=[pl.BlockSpec((1, nch, wd), lambda i: (i, 0, 0)),
                  pl.BlockSpec((None, 1, wd, nw), lambda i: (layer, i, 0, 0)),
                  pl.BlockSpec((None, 1, wd, ns), lambda i: (layer, i, 0, 0))],
        out_specs=[pl.BlockSpec((1, nch, nw), lambda i: (i, 0, 0)),
                   pl.BlockSpec((1, nch, ns), lambda i: (0, 0, i))],
        out_shape=[jax.ShapeDtypeStruct((g, nch, nw), F32), jax.ShapeDtypeStruct((1, nch, g * ns), F32)],
        compiler_params=_cparams(("parallel",)),
        name="s5_local",
    )(u, m, w)


def _s5_scan_kernel(s_ref, ca_ref, cb_ref, o_ref, *, n_ctx_chunks):
    n = s_ref.shape[0]
    ca = ca_ref[...]
    cb = cb_ref[...]
    lane = lax.broadcasted_iota(jnp.int32, (1, 2 * LANES), 1)
    is_f = lane < LANES

    def swap(v):
        return jnp.concatenate([pltpu.roll(v[:, :LANES], LANES // 2, axis=1),
                                pltpu.roll(v[:, LANES:], LANES // 2, axis=1)], axis=1)

    def body(i, st):
        jf = i
        jb = jnp.where(i < n_ctx_chunks, n_ctx_chunks - 1 - i, n - 1 + n_ctx_chunks - i)
        cur_f = s_ref[jf]
        cur_b = s_ref[jb]
        o_ref[jf, :, :LANES] = st[:, :LANES]
        o_ref[jb, :, LANES:] = st[:, LANES:]
        return st * ca + swap(st) * cb + jnp.where(is_f, cur_f, cur_b)

    lax.fori_loop(0, n, body, jnp.zeros(ca.shape, F32))


def _s5_scan(s_loc, ca, cb, n_batch, n_ctx_chunks):
    nch, g, ns = s_loc.shape
    n = nch // n_batch
    return pl.pallas_call(
        functools.partial(_s5_scan_kernel, n_ctx_chunks=n_ctx_chunks),
        grid=(n_batch,),
        in_specs=[pl.BlockSpec((n, g, ns), lambda i: (i, 0, 0)),
                  pl.BlockSpec(ca.shape, lambda i: (0, 0)),
                  pl.BlockSpec(cb.shape, lambda i: (0, 0))],
        out_specs=pl.BlockSpec((n, g, ns), lambda i: (i, 0, 0)),
        out_shape=jax.ShapeDtypeStruct((nch, g, ns), F32),
        compiler_params=_cparams(("parallel",)),
        name="s5_scan",
    )(s_loc, ca, cb)


def _s5_read_kernel(y1_ref, sp_ref, v_ref, y_ref):
    y_ref[0] = y1_ref[0] + _dot_hi(sp_ref[0], v_ref[0])


def _s5_read(y1, s_prev, vmat, layer):
    g, nch, nw = y1.shape
    ns = vmat.shape[2]
    return pl.pallas_call(
        _s5_read_kernel,
        grid=(g,),
        in_specs=[pl.BlockSpec((1, nch, nw), lambda i: (i, 0, 0)),
                  pl.BlockSpec((1, nch, ns), lambda i: (0, 0, i)),
                  pl.BlockSpec((None, 1, ns, nw), lambda i: (layer, i, 0, 0))],
        out_specs=pl.BlockSpec((1, nch, nw), lambda i: (i, 0, 0)),
        out_shape=jax.ShapeDtypeStruct((g, nch, nw), F32),
        compiler_params=_cparams(("parallel",)),
        name="s5_read",
    )(y1, s_prev, vmat)


def _s5_matrices(lam_re, lam_im, log_step, b_re, b_im, c_re, c_im):
    t = S5_CHUNK
    step = jnp.exp(log_step)[..., None]
    den = lam_re * lam_re + lam_im * lam_im
    er = jnp.exp(lam_re * step)
    ar, ai = er * jnp.cos(lam_im * step), er * jnp.sin(lam_im * step)
    nr, ni = ar - 1.0, ai
    fr = (nr * lam_re + ni * lam_im) / den
    fi = (ni * lam_re - nr * lam_im) / den
    bre, bim = b_re[:, None], b_im[:, None]
    br = fr[..., None] * bre - fi[..., None] * bim
    bi = fr[..., None] * bim + fi[..., None] * bre
    kk = jnp.arange(t + 1, dtype=F32)[:, None, None, None, None]
    ek = jnp.exp(kk * (lam_re * step)[None])
    pr, pi = ek * jnp.cos(kk * (lam_im * step)[None]), ek * jnp.sin(kk * (lam_im * step)[None])
    abr = pr[..., None] * br[None] - pi[..., None] * bi[None]
    abi = pr[..., None] * bi[None] + pi[..., None] * br[None]
    ein = functools.partial(jnp.einsum, precision=HIGHEST)
    kern = ein('dsgop,kdsgpj->kdsgoj', c_re, abr[:t]) - ein('dsgop,kdsgpj->kdsgoj', c_im, abi[:t])
    kf, kb = kern[:, :, 0], kern[:, :, 1]
    lag = jnp.concatenate([kb[:0:-1], (kf[0] + kb[0])[None], kf[1:]], axis=0)
    nd = lag.shape[1]
    flat = lag.transpose(1, 2, 4, 0, 3).reshape(nd, S5_GROUPS, S5_GROUP, (2 * t - 1) * S5_GROUP)
    m = jnp.stack([flat[..., (t - 1 - j) * S5_GROUP:(2 * t - 1 - j) * S5_GROUP] for j in range(t)], axis=2)
    m = m.reshape(nd, S5_GROUPS, t * S5_GROUP, t * S5_GROUP)
    wf = jnp.stack([abr[:t][::-1, :, 0], abi[:t][::-1, :, 0]], axis=0)
    wb = jnp.stack([abr[:t, :, 1], abi[:t, :, 1]], axis=0)
    w = jnp.concatenate([wf, wb], axis=0)
    w = w.transpose(2, 3, 1, 5, 0, 4).reshape(w.shape[2], S5_GROUPS, t * S5_GROUP, 4 * S5_STATE)
    cr, ci = c_re[None], c_im[None]
    vre = cr * pr[:, :, :, :, None, :] - ci * pi[:, :, :, :, None, :]
    vim = -(cr * pi[:, :, :, :, None, :] + ci * pr[:, :, :, :, None, :])
    vf = jnp.stack([vre[1:, :, 0], vim[1:, :, 0]], axis=0)
    vb = jnp.stack([vre[:0:-1, :, 1], vim[:0:-1, :, 1]], axis=0)
    v = jnp.concatenate([vf, vb], axis=0)
    v = v.transpose(2, 3, 0, 5, 1, 4).reshape(v.shape[2], S5_GROUPS, 4 * S5_STATE, t * S5_GROUP)
    atr, ati = pr[t], pi[t]
    ca = jnp.concatenate([atr[:, 0], atr[:, 0], atr[:, 1], atr[:, 1]], axis=-1)
    cb = jnp.concatenate([-ati[:, 0], ati[:, 0], -ati[:, 1], ati[:, 1]], axis=-1)
    return m, w, v, ca, cb


def _mix_kernel(x_ref, mod_ref, a_ref, yf_ref, yb_ref, z_ref, gs_ref, y5_ref, u5_ref, d5_ref, wg_ref, bg_ref,
                f_ref, wo_ref, g2_ref, wr_ref, br_ref, o_ref):
    mod = mod_ref[0]
    d = x_ref.shape[1]
    ssd = _rms((yf_ref[0] + yb_ref[0]) * _silu(z_ref[0]), SSD_INNER) * gs_ref[...]
    y5 = y5_ref[0] + d5_ref[...] * u5_ref[0]
    g5 = jax.nn.gelu(y5)
    s5 = g5 * _sigmoid(_dot(g5.astype(BF16), wg_ref[...]) + bg_ref[...])
    acc = _dot(a_ref[0].astype(BF16), wo_ref[0:256, :])
    acc += _dot(ssd.astype(BF16), wo_ref[256:512, :])
    acc += _dot(s5.astype(BF16), wo_ref[512:768, :])
    acc += _dot(f_ref[0].astype(BF16), wo_ref[768:1024, :])
    xn = x_ref[...] + mod[2:3] * acc
    h = _rms(xn, d) * g2_ref[...]
    h = h * (1.0 + mod[4:5]) + mod[3:4]
    o_ref[0, :, :d] = xn
    o_ref[0, :, d:] = _route(h, wr_ref, br_ref)


def _mix(x, layer, modtab, a, yf, yb, ssd_raw, gs, y5, u5, d5, wglu, bglu, f, wo, g2, wr, br):
    (b, l, _), d = a.shape, x.shape[1]
    nt = l // TOK_TILE
    tok = lambda c: pl.BlockSpec((1, TOK_TILE, c), lambda i, j: (i, j, 0))
    full = lambda arr: pl.BlockSpec(arr.shape, lambda i, j: (0,) * arr.ndim)
    return pl.pallas_call(
        _mix_kernel,
        grid=(b, nt),
        in_specs=[pl.BlockSpec((TOK_TILE, d), lambda i, j: (i * nt + j, 0)),
                  pl.BlockSpec((1, 8, d), lambda i, j: (2 * i + jnp.minimum(j, 1), 0, 0)),
                  tok(256), tok(256), tok(256), tok(256), full(gs), tok(256), tok(256), full(d5), full(wglu),
                  full(bglu), tok(256), pl.BlockSpec((None,) + wo.shape[1:], lambda i, j: (layer, 0, 0)),
                  full(g2), full(wr), full(br)],
        out_specs=tok(d + LANES),
        out_shape=jax.ShapeDtypeStruct((b, l, d + LANES), F32),
        compiler_params=_cparams(("parallel", "arbitrary")),
        name="mix_out",
    )(x, modtab, a, yf, yb, ssd_raw, gs, y5, u5, d5, wglu, bglu, f, wo, g2, wr, br)


def _route(h, wr_ref, br_ref):
    lane = lax.broadcasted_iota(jnp.int32, (1, LANES), 1)
    lanef = lane.astype(F32)
    logits = _dot_hi(h, wr_ref[...]) + br_ref[...]
    neg = -jnp.inf
    big = float(4 * LANES)
    first = lambda hit: jnp.min(jnp.where(hit, lanef, big), axis=-1, keepdims=True)
    glog = jnp.where(lane < MOE_GROUPS, logits, neg)
    gmax = jnp.max(glog, axis=-1, keepdims=True)
    gidx = first(glog == gmax)
    p_group = 1.0 / jnp.sum(jnp.exp(glog - gmax), axis=-1, keepdims=True)
    e0 = MOE_GROUPS + gidx * MOE_PER_GROUP
    elog = jnp.where((lanef >= e0) & (lanef < e0 + MOE_PER_GROUP), logits, neg)
    v1 = jnp.max(elog, axis=-1, keepdims=True)
    i1 = first(elog == v1)
    elog2 = jnp.where(lanef == i1, neg, elog)
    v2 = jnp.max(elog2, axis=-1, keepdims=True)
    i2 = first(elog2 == v2)
    e2 = jnp.exp(v2 - v1)
    w1 = p_group / (1.0 + e2)
    w2 = p_group * e2 / (1.0 + e2)
    return (jnp.where(lanef == i1 - e0, w1, 0.0) + jnp.where(lanef == i2 - e0, w2, 0.0)
            + jnp.where(lane == MOE_PER_GROUP, gidx, 0.0))


def _route_plan(gid, tm, nt):
    t = gid.shape[0]
    onehot = (gid[:, None] == jnp.arange(MOE_GROUPS, dtype=jnp.int32)[None]).astype(jnp.int32)
    csum = jnp.cumsum(onehot, axis=0)
    rank = jnp.sum((csum - onehot) * onehot, axis=1)
    padded = (csum[-1] + tm - 1) // tm * tm
    ends = jnp.cumsum(padded)
    slot = jnp.sum(onehot * (ends - padded)[None], axis=1) + rank
    src = jnp.full((nt * tm,), -1, jnp.int32).at[slot].set(jnp.arange(t, dtype=jnp.int32))
    pos = jnp.arange(nt * tm, dtype=jnp.int32)
    spare = t + (pos // tm % 2) * tm + pos % tm
    dst = jnp.where(src >= 0, src, spare)
    starts = jnp.arange(nt, dtype=jnp.int32) * tm
    tile_group = jnp.minimum(jnp.sum((starts[:, None] >= ends[None]).astype(jnp.int32), axis=1), MOE_GROUPS - 1)
    return src, dst, tile_group, (ends[-1] // tm).reshape(1)


def _moe_kernel(src_s, dst_s, tg_s, nu_s, srcv_ref, xg_hbm, mod_ref, g2_ref, wg_ref, wu_ref, wd_ref, out_hbm,
                gbuf, ybuf, gsem, ssem, *, seq_len, n_ctx):
    del tg_s
    i = pl.program_id(0)
    n_used = nu_s[0]
    tm = gbuf.shape[1]
    d = ybuf.shape[2]
    slot = i % 2

    def gather_copy(tile, sl, r):
        t = jnp.maximum(src_s[tile * tm + r], 0)
        return pltpu.make_async_copy(xg_hbm.at[pl.ds(t, 1)], gbuf.at[sl, pl.ds(r, 1)], gsem.at[sl])

    def scatter_copy(tile, sl, r):
        t = dst_s[tile * tm + r]
        return pltpu.make_async_copy(ybuf.at[sl, pl.ds(r, 1)], out_hbm.at[pl.ds(t, 1)], ssem.at[sl])

    def each_row(make_copy, tile, sl, wait):
        def body(r, c):
            cp = make_copy(tile, sl, r)
            cp.wait() if wait else cp.start()
            return c
        lax.fori_loop(0, tm, body, 0, unroll=8)

    gather = functools.partial(each_row, gather_copy)
    scatter = functools.partial(each_row, scatter_copy)

    @pl.when(i == 0)
    def _():
        gather(0, 0, False)
        ybuf[...] = jnp.zeros(ybuf.shape, F32)
        n_tok = out_hbm.shape[0] - 2 * tm
        fills = [pltpu.make_async_copy(ybuf.at[sl], out_hbm.at[pl.ds(n_tok + sl * tm, tm)], ssem.at[sl])
                 for sl in range(2)]
        for cp in fills:
            cp.start()
        for cp in fills:
            cp.wait()

    @pl.when(i + 1 < n_used)
    def _():
        gather(i + 1, 1 - slot, False)

    @pl.when(i < n_used)
    def _():
        gather(i, slot, True)

        @pl.when(i >= 2)
        def _():
            scatter(i - 2, slot, True)

        xg = gbuf[slot]
        x = xg[:, :d]
        gates = xg[:, d:]
        tok = jnp.maximum(srcv_ref[...], 0)
        n_seg = mod_ref.shape[0]

        def modrow(r):
            out = jnp.zeros((tm, d), F32)
            for sg in range(n_seg):
                lo_t = (sg // 2) * seq_len + (0 if sg % 2 == 0 else n_ctx)
                hi_t = (sg // 2) * seq_len + (n_ctx if sg % 2 == 0 else seq_len)
                out = jnp.where((tok >= lo_t) & (tok < hi_t), mod_ref[sg, r:r + 1, :], out)
            return out

        h = _rms(x, d) * g2_ref[...]
        h = (h * (1.0 + modrow(4)) + modrow(3)).astype(BF16)
        hid = _silu(_dot(h, wg_ref[0, 0])) * _dot(h, wu_ref[0, 0])
        hid = jnp.concatenate([hid[:, e * EXPERT_HIDDEN:(e + 1) * EXPERT_HIDDEN] * gates[:, e:e + 1]
                               for e in range(MOE_PER_GROUP)], axis=1)
        ybuf[slot] = x + modrow(5) * _dot(hid.astype(BF16), wd_ref[0, 0])
        scatter(i, slot, False)

    @pl.when(i == pl.num_programs(0) - 1)
    def _():
        last = n_used - 1
        scatter(last - 1, (last - 1) % 2, True)
        scatter(last, last % 2, True)


def _moe(xg, src, dst, tile_group, n_used, modtab, g2, wg, wu, wd, layer, seq_len, n_ctx):
    t, wx = xg.shape
    d = wx - LANES
    tm = MOE_TILE
    nt = src.shape[0] // tm
    full = lambda arr: pl.BlockSpec(arr.shape, lambda i, s, ds, g, n: (0,) * arr.ndim)
    wspec = lambda arr: pl.BlockSpec((1, 1) + arr.shape[2:], lambda i, s, ds, g, n: (layer, g[i], 0, 0))
    grid_spec = pltpu.PrefetchScalarGridSpec(
        num_scalar_prefetch=4,
        grid=(nt,),
        in_specs=[pl.BlockSpec((tm, 1), lambda i, s, ds, g, n: (i, 0)),
                  pl.BlockSpec(memory_space=pl.ANY),
                  full(modtab), full(g2), wspec(wg), wspec(wu), wspec(wd)],
        out_specs=pl.BlockSpec(memory_space=pl.ANY),
        scratch_shapes=[pltpu.VMEM((2, tm, wx), F32), pltpu.VMEM((2, tm, d), F32),
                        pltpu.SemaphoreType.DMA((2,)), pltpu.SemaphoreType.DMA((2,))])
    return pl.pallas_call(
        functools.partial(_moe_kernel, seq_len=seq_len, n_ctx=n_ctx),
        grid_spec=grid_spec,
        out_shape=jax.ShapeDtypeStruct((t + 2 * tm, d), F32),
        compiler_params=_cparams(("arbitrary",)),
        name="moe",
    )(src, dst, tile_group, n_used, src.reshape(-1, 1), xg, modtab, g2, wg, wu, wd)


def _pad_in_proj(w_in):
    depth, d, _ = w_in.shape
    pieces = [(MLA_QK, LANES - MLA_QK)] * MLA_HEADS
    pieces += [(MLA_KV_RANK, MLA_NOPE), (MLA_ROPE, LANES - MLA_QK)]
    pieces += [(SSD_COLS, P_S5 - P_SSD - SSD_COLS), (S5_COLS, 0), (512, 0)]
    pieces += [(DIFF_V, LANES - DIFF_V)] * DIFF_HEADS
    cols, o = [], 0
    for width, gap in pieces:
        cols.append(w_in[:, :, o:o + width])
        o += width
        if gap:
            cols.append(jnp.zeros((depth, d, gap), w_in.dtype))
    out = jnp.concatenate(cols, axis=-1)
    assert o == w_in.shape[2] and out.shape[2] == P_COLS
    return out


def _rope_tables(seq, n_ctx):
    pos = jnp.arange(seq, dtype=jnp.int32)
    rows = (pos // GRID_W).astype(F32)
    cols = (pos % GRID_W).astype(F32)
    half = MLA_ROPE // 2
    inv = ROPE_BASE ** (-jnp.arange(0, half, 2, dtype=F32) / half)
    r = jnp.arange(MLA_ROPE)
    freq = inv[r % (half // 2)]
    ang = jnp.where((r < half)[None, :], rows[:, None], cols[:, None]) * freq[None, :]
    cos, sin = jnp.cos(ang), jnp.sin(ang)
    first = ((r % half) < half // 2)[None, :]
    up = jnp.where(first, -sin, 0.0)
    dn = jnp.where(first, 0.0, sin)
    t32 = jnp.stack([cos, up, dn])
    ident = jnp.stack([jnp.ones((n_ctx, MLA_ROPE), F32), jnp.zeros((n_ctx, MLA_ROPE), F32),
                       jnp.zeros((n_ctx, MLA_ROPE), F32)])
    t32 = jnp.concatenate([ident, t32], axis=1)
    n = t32.shape[1]
    base = jnp.stack([jnp.ones((n, LANES), F32), jnp.zeros((n, LANES), F32), jnp.zeros((n, LANES), F32)])
    rope_m = base.at[:, :, MLA_NOPE:MLA_QK].set(t32)
    rope_d = jnp.tile(t32, (1, 1, LANES // MLA_ROPE))
    return rope_m, rope_d


def _pad_lanes(v, n):
    return jnp.zeros((1, n), F32).at[0, :v.shape[0]].set(v)


def kernel(x, c, ctx, c_ctx, w_ada, b_ada, norm1, norm2, w_in, w_out, mla_kv_norm, mla_w_uk, mla_w_uv, mla_q_norm, mla_k_norm, ssd_conv_w, ssd_conv_b, ssd_a_log, ssd_dt_bias, ssd_d, ssd_norm, s5_lam_re, s5_lam_im, s5_log_step, s5_b_re, s5_b_im, s5_c_re, s5_c_im, s5_d, s5_w_glu, s5_b_glu, diff_q_norm, diff_k_norm, diff_lq1, diff_lk1, diff_lq2, diff_lk2, diff_subln, moe_w_group, moe_b_group, moe_w_expert, moe_b_expert, moe_w_gate, moe_w_up, moe_w_down):
    bsz, seq, d = x.shape
    n_ctx = ctx.shape[1]
    depth = w_in.shape[0]
    assert n_ctx == TOK_TILE and seq % FLASH_KV_CHUNK == 0 and d == 1024
    l = n_ctx + seq
    assert (bsz * l) % MOE_TILE == 0 and bsz * l >= 2 * MOE_TILE and l % SSD_CHUNK == 0 and n_ctx % SSD_CHUNK == 0

    cvec = jnp.zeros((8, d), F32).at[:bsz].set(c).at[bsz].set(c_ctx)
    ada = _ada(cvec, w_ada, b_ada).reshape(depth, 8, N_MOD, d)
    mod_lat = ada[:, :bsz]
    mod_ctx = jnp.broadcast_to(ada[:, bsz][:, None], mod_lat.shape)
    modtab = jnp.stack([mod_ctx, mod_lat], axis=2)
    modtab = jnp.pad(modtab, ((0, 0), (0, 0), (0, 0), (0, 8 - N_MOD), (0, 0))).reshape(depth, bsz * 2, 8, d)

    w_in_p = _pad_in_proj(w_in).astype(BF16)
    w_out_b = w_out.astype(BF16)
    rope_m, rope_d = _rope_tables(seq, n_ctx)
    wuk = jnp.zeros((depth, MLA_KV_RANK, MLA_HEADS * LANES), F32)
    for hd in range(MLA_HEADS):
        wuk = wuk.at[:, :, hd * LANES:hd * LANES + MLA_NOPE].set(mla_w_uk[:, :, hd * MLA_NOPE:(hd + 1) * MLA_NOPE])
    wuk = wuk.astype(BF16)
    wuv = jnp.zeros((depth, MLA_KV_RANK, MLA_HEADS * LANES), F32)
    for hd in range(MLA_HEADS):
        wuv = wuv.at[:, :, hd * LANES:hd * LANES + MLA_V].set(mla_w_uv[:, :, hd * MLA_V:(hd + 1) * MLA_V])
    wuv = wuv.astype(BF16)
    amax = lambda g: jnp.max(jnp.abs(g.astype(F32)), axis=-1)
    bounded_m = (math.sqrt(MLA_QK) * amax(mla_q_norm) * amax(mla_k_norm) <= SCORE_BOUND).astype(F32)
    bounded_d = (math.sqrt(DIFF_HEAD) * amax(diff_q_norm) * amax(diff_k_norm) <= SCORE_BOUND).astype(F32)
    lane = jnp.arange(LANES)
    seg = (lane[:, None] // DIFF_HEAD == lane[None, :] // DIFF_HEAD).astype(F32)
    a_vec = -jnp.exp(ssd_a_log.astype(F32)).reshape(depth, 2 * SSD_HEADS)
    s5_m, s5_w, s5_v, s5_ca, s5_cb = _s5_matrices(
        s5_lam_re.astype(F32), s5_lam_im.astype(F32), s5_log_step.astype(F32),
        s5_b_re.astype(F32), s5_b_im.astype(F32), s5_c_re.astype(F32), s5_c_im.astype(F32))
    w_router = jnp.concatenate([moe_w_group, moe_w_expert.transpose(0, 2, 1, 3).reshape(depth, d, N_EXPERTS)], axis=-1)
    w_router = jnp.pad(w_router, ((0, 0), (0, 0), (0, LANES - w_router.shape[-1])))
    b_router = jnp.concatenate([moe_b_group, moe_b_expert.reshape(depth, N_EXPERTS)], axis=-1)
    b_router = jnp.pad(b_router, ((0, 0), (0, LANES - b_router.shape[-1])))
    by_group = lambda w: w.reshape(depth, MOE_GROUPS, MOE_PER_GROUP, d, EXPERT_HIDDEN).transpose(0, 1, 3, 2, 4) \
        .reshape(depth, MOE_GROUPS, d, MOE_PER_GROUP * EXPERT_HIDDEN).astype(BF16)
    wg_b, wu_b = by_group(moe_w_gate), by_group(moe_w_up)
    wd_b = moe_w_down.reshape(depth, MOE_GROUPS, MOE_PER_GROUP * EXPERT_HIDDEN, d).astype(BF16)
    lam_all = (jnp.exp(jnp.sum(diff_lq1.astype(F32) * diff_lk1.astype(F32), axis=-1))
               - jnp.exp(jnp.sum(diff_lq2.astype(F32) * diff_lk2.astype(F32), axis=-1)))

    n5 = l // S5_CHUNK
    xs = jnp.concatenate([ctx, x], axis=1).reshape(bsz * l, d)
    for i in range(depth):
        lam_init = 0.8 - 0.6 * math.exp(-0.3 * i)
        qm, km, vm, qd, kd, vd, ssd_raw, u5 = _front(
            xs, bsz, i, modtab[i], norm1[i][None], w_in_p, rope_m, rope_d,
            _pad_lanes(mla_q_norm[i], LANES), _pad_lanes(mla_k_norm[i], LANES), mla_kv_norm[i][None],
            wuk[i], wuv[i], jnp.tile(diff_q_norm[i], 4)[None], jnp.tile(diff_k_norm[i], 4)[None], seg)
        a = _flash(jnp.stack([bounded_m[i], jnp.zeros((), F32)]), qm, km, vm, jnp.ones((1, LANES), F32),
                   n_sub=1, n_ctx=n_ctx, post=1.0)
        f = _flash(jnp.stack([bounded_d[i], lam_all[i] + lam_init]), qd, kd, vd, jnp.tile(diff_subln[i], 2)[None],
                   n_sub=2, n_ctx=n_ctx, post=1.0 - lam_init)
        yf, yb = _ssd(ssd_raw, ssd_conv_w[i], ssd_conv_b[i][None],
                      _pad_lanes(ssd_dt_bias[i].reshape(-1), LANES), _pad_lanes(a_vec[i], LANES),
                      jnp.repeat(ssd_d[i], SSD_HEAD_DIM)[None], n_ctx)
        u5g = u5.reshape(bsz, n5, S5_CHUNK, S5_GROUPS, S5_GROUP).transpose(3, 0, 1, 2, 4)
        u5g = u5g.reshape(S5_GROUPS, bsz * n5, S5_CHUNK * S5_GROUP)
        y1, s_loc = _s5_local(u5g, s5_m, s5_w, i)
        s_loc = s_loc.reshape(bsz * n5, S5_GROUPS, 4 * S5_STATE)
        s_prev = _s5_scan(s_loc, s5_ca[i], s5_cb[i], bsz, n_ctx // S5_CHUNK)
        y5 = _s5_read(y1, s_prev.reshape(1, bsz * n5, S5_GROUPS * 4 * S5_STATE), s5_v, i)
        y5 = y5.reshape(S5_GROUPS, bsz, n5, S5_CHUNK, S5_GROUP).transpose(1, 2, 3, 0, 4).reshape(bsz, l, S5_COLS)
        xg = _mix(xs, i, modtab[i], a, yf, yb, ssd_raw, ssd_norm[i][None], y5, u5, s5_d[i][None],
                  s5_w_glu[i].astype(BF16), s5_b_glu[i][None], f, w_out_b,
                  norm2[i][None], w_router[i], b_router[i][None])
        xg = xg.reshape(bsz * l, d + LANES)
        gid = xg[:, d + MOE_PER_GROUP].astype(jnp.int32)
        src, dst, tile_group, n_used = _route_plan(gid, MOE_TILE, bsz * l // MOE_TILE + MOE_GROUPS)
        xs = _moe(xg, src, dst, tile_group, n_used, modtab[i], norm2[i][None], wg_b, wu_b, wd_b, i, l, n_ctx)
    return xs[:bsz * l].reshape(bsz, l, d)[:, n_ctx:]
```

```python
import functools
import math

import jax
import jax.numpy as jnp
from jax import lax
from jax.experimental import pallas as pl
from jax.experimental.pallas import tpu as pltpu

F32 = jnp.float32
BF16 = jnp.bfloat16
HIGHEST = lax.Precision.HIGHEST

LANES = 128
EPS = 1e-6
ROPE_BASE = 10000.0
GRID_W = 64
N_MOD = 6

GROUP_WIDTH = 256
MLA_HEADS, MLA_NOPE, MLA_ROPE, MLA_V, MLA_KV_RANK = 4, 64, 32, 64, 128
MLA_QK = MLA_NOPE + MLA_ROPE
SSD_HEADS, SSD_HEAD_DIM, SSD_GROUPS, SSD_STATE, SSD_INNER = 4, 64, 2, 64, 256
SSD_GN = SSD_GROUPS * SSD_STATE
SSD_XBC = SSD_INNER + 2 * SSD_GN
S5_GROUPS, S5_GROUP, S5_STATE = 16, 16, 64
DIFF_HEADS, DIFF_HEAD, DIFF_V = 4, 32, 64
MOE_GROUPS, MOE_PER_GROUP, N_EXPERTS, EXPERT_HIDDEN = 4, 8, 32, 256

MLA_COLS = MLA_HEADS * MLA_QK + MLA_KV_RANK + MLA_ROPE
SSD_COLS = SSD_INNER + SSD_XBC + 2 * SSD_HEADS
S5_COLS = 256
DIFF_COLS = 768

P_MLA_Q = 0
P_MLA_CKV = 512
P_MLA_KR = 640
P_SSD = 768
P_S5 = 1664
P_DIFF = 1920
P_COLS = 2944
V_ONE = 64
SCORE_BOUND = 40.0
LOG2E = 1.4426950408889634

TOK_TILE = 256
SSD_CHUNK = 256
S5_CHUNK = 32
FLASH_KV_CHUNK = 768
MOE_TILE = 512
VMEM_LIMIT = 56 * 1024 * 1024


def _cparams(sem):
    return pltpu.CompilerParams(dimension_semantics=sem, vmem_limit_bytes=VMEM_LIMIT)


def _sigmoid(x):
    return 1.0 / (1.0 + jnp.exp(-x))


def _silu(x):
    return x * _sigmoid(x)


def _rms(x, n):
    return x * lax.rsqrt(jnp.sum(x * x, axis=-1, keepdims=True) * (1.0 / n) + EPS)


def _dot(a, b):
    return jnp.dot(a, b, preferred_element_type=F32)


def _dot_hi(a, b):
    return jnp.dot(a, b, preferred_element_type=F32, precision=HIGHEST)


def _dot_nt(a, b):
    return lax.dot_general(a, b, (((1,), (1,)), ((), ())), preferred_element_type=F32)


def _dot_tn(a, b):
    return lax.dot_general(a, b, (((0,), (0,)), ((), ())), preferred_element_type=F32)


def _dot_tn_hi(a, b):
    return lax.dot_general(a, b, (((0,), (0,)), ((), ())), preferred_element_type=F32, precision=HIGHEST)


def _ada_kernel(c_ref, w_ref, b_ref, o_ref):
    o_ref[0] = _dot_hi(_silu(c_ref[...]), w_ref[0]) + b_ref[0]


def _ada(cvec, w_ada, b_ada):
    depth, d, nd = w_ada.shape
    blk = 1024
    return pl.pallas_call(
        _ada_kernel,
        grid=(depth, nd // blk),
        in_specs=[pl.BlockSpec((8, d), lambda i, j: (0, 0)),
                  pl.BlockSpec((1, d, blk), lambda i, j: (i, 0, j)),
                  pl.BlockSpec((1, 1, blk), lambda i, j: (i, 0, j))],
        out_specs=pl.BlockSpec((1, 8, blk), lambda i, j: (i, 0, j)),
        out_shape=jax.ShapeDtypeStruct((depth, 8, nd), F32),
        compiler_params=_cparams(("arbitrary", "arbitrary")),
        name="ada",
    )(cvec, w_ada, b_ada.reshape(depth, 1, nd))


def _rope(x, cos, sin_up, sin_dn):
    n = x.shape[-1]
    return x * cos + pltpu.roll(x, n - 8, axis=1) * sin_up + pltpu.roll(x, 8, axis=1) * sin_dn


def _front_kernel(x_ref, mod_ref, g1_ref, w_ref, rm_ref, rd_ref, gq_ref, gk_ref, gkv_ref, wuk_ref, wuv_ref,
                  gdq_ref, gdk_ref, seg_ref, one_ref,
                  qm_ref, km_ref, vm_ref, qd_ref, kd_ref, vd_ref, ssd_ref, s5_ref):
    x = x_ref[...]
    mod = mod_ref[0]
    h = _rms(x, x.shape[-1]) * g1_ref[...]
    h = h * (1.0 + mod[1:2]) + mod[0:1]
    p = _dot(h.astype(BF16), w_ref[...])

    ssd_ref[0] = p[:, P_SSD:P_S5]
    s5_ref[0] = p[:, P_S5:P_DIFF]

    cos_m, up_m, dn_m = rm_ref[0], rm_ref[1], rm_ref[2]
    ckv = _rms(p[:, P_MLA_CKV:P_MLA_KR], MLA_KV_RANK) * gkv_ref[...]
    ckv = ckv.astype(BF16)
    k_nope = _dot(ckv, wuk_ref[...])
    vm_ref[0] = (_dot(ckv, wuv_ref[...]) + one_ref[...]).astype(BF16)
    k_rope = p[:, P_MLA_KR:P_SSD]
    q_scale = MLA_QK ** -0.5 * LOG2E
    for hd in range(MLA_HEADS):
        sl = slice(hd * LANES, (hd + 1) * LANES)
        q = _rms(p[:, sl], MLA_QK) * gq_ref[...]
        qm_ref[0, :, sl] = (_rope(q, cos_m, up_m, dn_m) * q_scale).astype(BF16)
        k = _rms(k_nope[:, sl] + k_rope, MLA_QK) * gk_ref[...]
        km_ref[0, :, sl] = _rope(k, cos_m, up_m, dn_m).astype(BF16)

    cos_d, up_d, dn_d = rd_ref[0], rd_ref[1], rd_ref[2]
    d_scale = DIFF_HEAD ** -0.5 * LOG2E
    for blk, (g_ref, o_ref, scale) in enumerate(((gdq_ref, qd_ref, d_scale), (gdk_ref, kd_ref, 1.0))):
        for half in range(2):
            sl = slice(P_DIFF + blk * 256 + half * LANES, P_DIFF + blk * 256 + (half + 1) * LANES)
            t = p[:, sl]
            ss = _dot_hi(t * t, seg_ref[...])
            t = t * lax.rsqrt(ss * (1.0 / DIFF_HEAD) + EPS) * g_ref[...]
            o_ref[0, :, half * LANES:(half + 1) * LANES] = (_rope(t, cos_d, up_d, dn_d) * scale).astype(BF16)
    vd_ref[0] = (p[:, P_DIFF + 512:P_COLS] + one_ref[...]).astype(BF16)


def _front(x, b, layer, modtab, g1, w_in_p, rope_m, rope_d, gq, gk, gkv, wuk, wuv, gdq, gdk, seg):
    l, d = rope_m.shape[1], x.shape[1]
    nt = l // TOK_TILE
    tok = lambda c: pl.BlockSpec((1, TOK_TILE, c), lambda i, j: (i, j, 0))
    full = lambda a: pl.BlockSpec(a.shape, lambda i, j: (0,) * a.ndim)
    rope_spec = pl.BlockSpec((3, TOK_TILE, LANES), lambda i, j: (0, j, 0))
    one = (jnp.arange(4 * LANES) % LANES == V_ONE).astype(F32)[None]
    outs = [(512, BF16), (512, BF16), (512, BF16), (256, BF16), (256, BF16), (512, BF16),
            (P_S5 - P_SSD, F32), (256, F32)]
    return pl.pallas_call(
        _front_kernel,
        grid=(b, nt),
        in_specs=[pl.BlockSpec((TOK_TILE, d), lambda i, j: (i * nt + j, 0)),
                  pl.BlockSpec((1, 8, d), lambda i, j: (2 * i + jnp.minimum(j, 1), 0, 0)),
                  full(g1), pl.BlockSpec((None,) + w_in_p.shape[1:], lambda i, j: (layer, 0, 0)), rope_spec, rope_spec,
                  full(gq), full(gk), full(gkv), full(wuk), full(wuv), full(gdq), full(gdk), full(seg), full(one)],
        out_specs=[tok(c) for c, _ in outs],
        out_shape=[jax.ShapeDtypeStruct((b, l, c), dt) for c, dt in outs],
        compiler_params=_cparams(("parallel", "arbitrary")),
        name="front",
    )(x, modtab, g1, w_in_p, rope_m, rope_d, gq, gk, gkv, wuk, wuv, gdq, gdk, seg, one)


def _flash_kernel(scal_ref, q_ref, k_ref, v_ref, sub_ref, o_ref, acc_scr, m_scr, *, n_sub, n_ctx, ck, post):
    qi = pl.program_id(2)
    lk = k_ref.shape[1]
    lane = lax.broadcasted_iota(jnp.int32, (1, LANES), 1)
    lo = lane < V_ONE
    q = q_ref[0]
    maps = [(a, c) for a in range(2) for c in range(n_sub)]

    def q_of(a, c):
        if n_sub == 1:
            return q[:, a * LANES:(a + 1) * LANES]
        s = (a * n_sub + c) * DIFF_HEAD
        return jnp.where((lane >= s) & (lane < s + DIFF_HEAD), q, jnp.zeros_like(q))

    qs = [q_of(a, c) for a, c in maps]

    def scores(mi, start, size):
        a = maps[mi][0]
        kk = k_ref[0, pl.ds(start, size), a * LANES:(a + 1) * LANES] if n_sub == 1 else k_ref[0, pl.ds(start, size), :]
        return _dot_nt(qs[mi], kk)

    def values(mi, start, size):
        a = maps[mi][0]
        return v_ref[0, pl.ds(start, size), a * LANES:(a + 1) * LANES]

    def step_bounded(start, size):
        for mi in range(len(maps)):
            p = jnp.exp2(scores(mi, start, size)).astype(BF16)
            acc_scr[mi] += _dot(p, values(mi, start, size))

    def step_online(start, size):
        for mi in range(len(maps)):
            s = scores(mi, start, size)
            m_prev = m_scr[mi]
            m_new = jnp.maximum(m_prev, jnp.max(s, axis=-1, keepdims=True))
            p = jnp.exp2(s - m_new).astype(BF16)
            acc_scr[mi] = acc_scr[mi] * jnp.exp2(m_prev - m_new) + _dot(p, values(mi, start, size))
            m_scr[mi] = m_new

    def sweep(step):
        @pl.when(qi == 0)
        def _():
            step(0, n_ctx)

        @pl.when(qi > 0)
        def _():
            def body(i, carry):
                step(pl.multiple_of(i * ck, ck), ck)
                return carry
            lax.fori_loop(0, lk // ck, body, 0)

    acc_scr[...] = jnp.zeros(acc_scr.shape, F32)
    bounded = scal_ref[0] != 0.0

    @pl.when(bounded)
    def _():
        sweep(step_bounded)

    @pl.when(jnp.logical_not(bounded))
    def _():
        m_scr[...] = jnp.full(m_scr.shape, -jnp.inf, F32)
        sweep(step_online)

    def normalised(mi):
        acc = acc_scr[mi]
        return acc * (1.0 / acc[:, V_ONE:V_ONE + 1])

    heads = []
    for a in range(2):
        if n_sub == 1:
            o = normalised(a)
        else:
            o = normalised(a * n_sub) - scal_ref[1] * normalised(a * n_sub + 1)
            ss = jnp.sum(jnp.where(lo, o * o, 0.0), axis=-1, keepdims=True)
            o = o * lax.rsqrt(ss * (1.0 / V_ONE) + EPS)
        heads.append(o)
    pair = jnp.where(lo, heads[0], pltpu.roll(heads[1], V_ONE, axis=1))
    o_ref[0] = pair if n_sub == 1 else pair * sub_ref[...] * post


def _flash(scal, q, k, v, subln, *, n_sub, n_ctx, post):
    b, l, qw = q.shape
    wq = qw // 2
    nq = l // TOK_TILE
    kern = functools.partial(_flash_kernel, n_sub=n_sub, n_ctx=n_ctx, ck=FLASH_KV_CHUNK, post=post)
    return pl.pallas_call(
        kern,
        grid=(b, 2, nq),
        in_specs=[pl.BlockSpec(memory_space=pltpu.SMEM),
                  pl.BlockSpec((1, TOK_TILE, wq), lambda i, h, j: (i, j, h)),
                  pl.BlockSpec((1, l, wq), lambda i, h, j: (i, 0, h)),
                  pl.BlockSpec((1, l, 2 * LANES), lambda i, h, j: (i, 0, h)),
                  pl.BlockSpec((1, LANES), lambda i, h, j: (0, 0))],
        out_specs=pl.BlockSpec((1, TOK_TILE, LANES), lambda i, h, j: (i, j, h)),
        out_shape=jax.ShapeDtypeStruct((b, l, 2 * LANES), F32),
        scratch_shapes=[pltpu.VMEM((2 * n_sub, TOK_TILE, LANES), F32),
                        pltpu.VMEM((2 * n_sub, TOK_TILE, 1), F32)],
        compiler_params=_cparams(("parallel", "parallel", "arbitrary")),
        name="flash_diff" if n_sub == 2 else "flash_mla",
    )(scal, q, k, v, subln)


def _ssd_dir(d, first, xbc_ref, prev_ref, next_ref, edge_lo, edge_hi, cw_ref, cb_ref, dtb_ref, a_ref, dsk_ref,
             y_ref, st_scr):
    t = xbc_ref.shape[1]
    raw = xbc_ref[0]
    x = raw[:, SSD_INNER:SSD_INNER + SSD_XBC]
    row = lax.broadcasted_iota(jnp.int32, (t, 1), 0)
    prev_row = jnp.where(edge_lo, 0.0, prev_ref[0, 7:8, SSD_INNER:SSD_INNER + SSD_XBC])
    next_row = jnp.where(edge_hi, 0.0, next_ref[0, 0:1, SSD_INNER:SSD_INNER + SSD_XBC])
    xm1 = jnp.where(row == 0, prev_row, pltpu.roll(x, 1, axis=0))
    xp1 = jnp.where(row == t - 1, next_row, pltpu.roll(x, t - 1, axis=0))
    xc = _silu(cw_ref[0:1] * xm1 + cw_ref[1:2] * x + cw_ref[2:3] * xp1 + cb_ref[...])
    xs = xc[:, :SSD_INNER]
    bm = xc[:, SSD_INNER:SSD_INNER + SSD_GN]
    cm = xc[:, SSD_INNER + SSD_GN:]

    z = raw[:, SSD_INNER + SSD_XBC:] + dtb_ref[...]
    dt = jnp.maximum(z, 0.0) + jnp.log1p(jnp.exp(-jnp.abs(z)))
    da = dt * a_ref[...]

    ri = lax.broadcasted_iota(jnp.int32, (t, t), 0)
    ci = lax.broadcasted_iota(jnp.int32, (t, t), 1)
    mask = (ci <= ri) if d == 0 else (ci >= ri)
    mask_t = (ri <= ci) if d == 0 else (ri >= ci)
    cum = _dot_hi(mask.astype(F32), da)
    cum_t = _dot_tn_hi(da, mask_t.astype(F32))
    tot = jnp.sum(da, axis=0, keepdims=True)

    lane = lax.broadcasted_iota(jnp.int32, (1, LANES), 1)
    lo = lane < (LANES // 2)

    @pl.when(first)
    def _():
        st_scr[d] = jnp.zeros(st_scr.shape[1:], F32)

    ys = []
    for g in range(SSD_GROUPS):
        gmask = lo if g == 0 else jnp.logical_not(lo)
        cg = jnp.where(gmask, cm, 0.0)
        gram = _dot_nt(cg.astype(BF16), bm.astype(BF16))
        xpair = xs[:, g * LANES:(g + 1) * LANES]
        yh = []
        for hh in range(2):
            hd = g * 2 + hh
            col = d * SSD_HEADS + hd
            cum_c = cum[:, col:col + 1]
            cum_r = cum_t[col:col + 1, :]
            dt_c = dt[:, col:col + 1]
            tot_c = tot[:, col:col + 1]
            decay = jnp.exp(jnp.where(mask, cum_c - cum_r, -jnp.inf))
            xdt = (xpair * dt_c).astype(BF16)
            st = st_scr[d, hd]
            y = (_dot((gram * decay).astype(BF16), xdt)
                 + _dot((cg * jnp.exp(cum_c)).astype(BF16), st.astype(BF16)))
            st_scr[d, hd] = st * jnp.exp(tot_c) + _dot_tn((bm * jnp.exp(tot_c - cum_c)).astype(BF16), xdt)
            yh.append(y)
        ys.append(jnp.where(lo, yh[0], yh[1]))
    y = jnp.concatenate(ys, axis=1)
    if d == 0:
        y = y + xs * dsk_ref[...]
    y_ref[0] = y


def _ssd_kernel(xf_ref, pf_ref, nf_ref, xb_ref, pb_ref, nb_ref, cw_ref, cb_ref, dtb_ref, a_ref, dsk_ref,
                yf_ref, yb_ref, st_scr, *, n_ctx_chunks):
    s = pl.program_id(1)
    nc = pl.num_programs(1)
    cf = s
    cb = jnp.where(s < n_ctx_chunks, n_ctx_chunks - 1 - s, nc - 1 + n_ctx_chunks - s)
    seg_lo = lambda c: (c == 0) | (c == n_ctx_chunks)
    seg_hi = lambda c: (c == n_ctx_chunks - 1) | (c == nc - 1)
    _ssd_dir(0, s == 0, xf_ref, pf_ref, nf_ref, seg_lo(cf), seg_hi(cf), cw_ref, cb_ref, dtb_ref, a_ref, dsk_ref,
             yf_ref, st_scr)
    _ssd_dir(1, s == 0, xb_ref, pb_ref, nb_ref, seg_lo(cb), seg_hi(cb), cw_ref, cb_ref, dtb_ref, a_ref, dsk_ref,
             yb_ref, st_scr)


def _ssd(raw, conv_w, conv_b, dt_bias, a_vec, dskip, n_ctx):
    b, l, w = raw.shape
    t = SSD_CHUNK
    nc = l // t
    ncc = n_ctx // t
    r8 = t // 8
    nb8 = l // 8
    fwd = lambda s: s
    bwd = lambda s: jnp.where(s < ncc, ncc - 1 - s, nc - 1 + ncc - s)
    chunk = lambda f: pl.BlockSpec((1, t, w), lambda i, s: (i, f(s), 0))
    prev = lambda f: pl.BlockSpec((1, 8, w), lambda i, s: (i, jnp.maximum(f(s) * r8 - 1, 0), 0))
    nxt = lambda f: pl.BlockSpec((1, 8, w), lambda i, s: (i, jnp.minimum((f(s) + 1) * r8, nb8 - 1), 0))
    full = lambda a: pl.BlockSpec(a.shape, lambda i, s: (0,) * a.ndim)
    out = lambda f: pl.BlockSpec((1, t, SSD_INNER), lambda i, s: (i, f(s), 0))
    return pl.pallas_call(
        functools.partial(_ssd_kernel, n_ctx_chunks=ncc),
        grid=(b, nc),
        in_specs=[chunk(fwd), prev(fwd), nxt(fwd), chunk(bwd), prev(bwd), nxt(bwd),
                  full(conv_w), full(conv_b), full(dt_bias), full(a_vec), full(dskip)],
        out_specs=[out(fwd), out(bwd)],
        out_shape=[jax.ShapeDtypeStruct((b, l, SSD_INNER), F32)] * 2,
        scratch_shapes=[pltpu.VMEM((2, SSD_HEADS, LANES, LANES), F32)],
        compiler_params=_cparams(("parallel", "arbitrary")),
        name="ssd",
    )(raw, raw, raw, raw, raw, raw, conv_w, conv_b, dt_bias, a_vec, dskip)


def _s5_local_kernel(u_ref, m_ref, w_ref, y_ref, s_ref):
    u = u_ref[0].astype(BF16)
    y_ref[0] = _dot(u, m_ref[0])
    s_ref[0] = _dot(u, w_ref[0])


def _s5_local(u, m, w, layer):
    g, nch, wd = u.shape
    nw = wd
    ns = w.shape[3]
    return pl.pallas_call(
        _s5_local_kernel,
        grid=(g,),
        in_specs=[pl.BlockSpec((1, nch, wd), lambda i: (i, 0, 0)),
                  pl.BlockSpec((None, 1, wd, nw), lambda i: (layer, i, 0, 0)),
                  pl.BlockSpec((None, 1, wd, ns), lambda i: (layer, i, 0, 0))],
        out_specs=[pl.BlockSpec((1, nch, nw), lambda i: (i, 0, 0)),
                   pl.BlockSpec((1, nch, ns), lambda i: (0, 0, i))],
        out_shape=[jax.ShapeDtypeStruct((g, nch, nw), F32), jax.ShapeDtypeStruct((1, nch, g * ns), F32)],
        compiler_params=_cparams(("parallel",)),
        name="s5_local",
    )(u, m, w)


def _s5_scan_kernel(s_ref, ca_ref, cb_ref, o_ref, *, n_ctx_chunks):
    n = s_ref.shape[0]
    ca = ca_ref[...]
    cb = cb_ref[...]
    lane = lax.broadcasted_iota(jnp.int32, (1, 2 * LANES), 1)
    is_f = lane < LANES

    def swap(v):
        return jnp.concatenate([pltpu.roll(v[:, :LANES], LANES // 2, axis=1),
                                pltpu.roll(v[:, LANES:], LANES // 2, axis=1)], axis=1)

    def body(i, st):
        jf = i
        jb = jnp.where(i < n_ctx_chunks, n_ctx_chunks - 1 - i, n - 1 + n_ctx_chunks - i)
        cur_f = s_ref[jf]
        cur_b = s_ref[jb]
        o_ref[jf, :, :LANES] = st[:, :LANES]
        o_ref[jb, :, LANES:] = st[:, LANES:]
        return st * ca + swap(st) * cb + jnp.where(is_f, cur_f, cur_b)

    lax.fori_loop(0, n, body, jnp.zeros(ca.shape, F32))


def _s5_scan(s_loc, ca, cb, n_batch, n_ctx_chunks):
    nch, g, ns = s_loc.shape
    n = nch // n_batch
    return pl.pallas_call(
        functools.partial(_s5_scan_kernel, n_ctx_chunks=n_ctx_chunks),
        grid=(n_batch,),
        in_specs=[pl.BlockSpec((n, g, ns), lambda i: (i, 0, 0)),
                  pl.BlockSpec(ca.shape, lambda i: (0, 0)),
                  pl.BlockSpec(cb.shape, lambda i: (0, 0))],
        out_specs=pl.BlockSpec((n, g, ns), lambda i: (i, 0, 0)),
        out_shape=jax.ShapeDtypeStruct((nch, g, ns), F32),
        compiler_params=_cparams(("parallel",)),
        name="s5_scan",
    )(s_loc, ca, cb)


def _s5_read_kernel(y1_ref, sp_ref, v_ref, y_ref):
    y_ref[0] = y1_ref[0] + _dot(sp_ref[0].astype(BF16), v_ref[0])


def _s5_read(y1, s_prev, vmat, layer):
    g, nch, nw = y1.shape
    ns = vmat.shape[2]
    return pl.pallas_call(
        _s5_read_kernel,
        grid=(g,),
        in_specs=[pl.BlockSpec((1, nch, nw), lambda i: (i, 0, 0)),
                  pl.BlockSpec((1, nch, ns), lambda i: (0, 0, i)),
                  pl.BlockSpec((None, 1, ns, nw), lambda i: (layer, i, 0, 0))],
        out_specs=pl.BlockSpec((1, nch, nw), lambda i: (i, 0, 0)),
        out_shape=jax.ShapeDtypeStruct((g, nch, nw), F32),
        compiler_params=_cparams(("parallel",)),
        name="s5_read",
    )(y1, s_prev, vmat)


def _s5_matrices(lam_re, lam_im, log_step, b_re, b_im, c_re, c_im):
    t = S5_CHUNK
    step = jnp.exp(log_step)[..., None]
    den = lam_re * lam_re + lam_im * lam_im
    er = jnp.exp(lam_re * step)
    ar, ai = er * jnp.cos(lam_im * step), er * jnp.sin(lam_im * step)
    nr, ni = ar - 1.0, ai
    fr = (nr * lam_re + ni * lam_im) / den
    fi = (ni * lam_re - nr * lam_im) / den
    bre, bim = b_re[:, None], b_im[:, None]
    br = fr[..., None] * bre - fi[..., None] * bim
    bi = fr[..., None] * bim + fi[..., None] * bre
    kk = jnp.arange(t + 1, dtype=F32)[:, None, None, None, None]
    ek = jnp.exp(kk * (lam_re * step)[None])
    pr, pi = ek * jnp.cos(kk * (lam_im * step)[None]), ek * jnp.sin(kk * (lam_im * step)[None])
    abr = pr[..., None] * br[None] - pi[..., None] * bi[None]
    abi = pr[..., None] * bi[None] + pi[..., None] * br[None]
    ein = functools.partial(jnp.einsum, precision=HIGHEST)
    kern = ein('dsgop,kdsgpj->kdsgoj', c_re, abr[:t]) - ein('dsgop,kdsgpj->kdsgoj', c_im, abi[:t])
    kf, kb = kern[:, :, 0], kern[:, :, 1]
    lag = jnp.concatenate([kb[:0:-1], (kf[0] + kb[0])[None], kf[1:]], axis=0)
    nd = lag.shape[1]
    flat = lag.transpose(1, 2, 4, 0, 3).reshape(nd, S5_GROUPS, S5_GROUP, (2 * t - 1) * S5_GROUP)
    m = jnp.stack([flat[..., (t - 1 - j) * S5_GROUP:(2 * t - 1 - j) * S5_GROUP] for j in range(t)], axis=2)
    m = m.reshape(nd, S5_GROUPS, t * S5_GROUP, t * S5_GROUP)
    wf = jnp.stack([abr[:t][::-1, :, 0], abi[:t][::-1, :, 0]], axis=0)
    wb = jnp.stack([abr[:t, :, 1], abi[:t, :, 1]], axis=0)
    w = jnp.concatenate([wf, wb], axis=0)
    w = w.transpose(2, 3, 1, 5, 0, 4).reshape(w.shape[2], S5_GROUPS, t * S5_GROUP, 4 * S5_STATE)
    cr, ci = c_re[None], c_im[None]
    vre = cr * pr[:, :, :, :, None, :] - ci * pi[:, :, :, :, None, :]
    vim = -(cr * pi[:, :, :, :, None, :] + ci * pr[:, :, :, :, None, :])
    vf = jnp.stack([vre[1:, :, 0], vim[1:, :, 0]], axis=0)
    vb = jnp.stack([vre[:0:-1, :, 1], vim[:0:-1, :, 1]], axis=0)
    v = jnp.concatenate([vf, vb], axis=0)
    v = v.transpose(2, 3, 0, 5, 1, 4).reshape(v.shape[2], S5_GROUPS, 4 * S5_STATE, t * S5_GROUP)
    atr, ati = pr[t], pi[t]
    ca = jnp.concatenate([atr[:, 0], atr[:, 0], atr[:, 1], atr[:, 1]], axis=-1)
    cb = jnp.concatenate([-ati[:, 0], ati[:, 0], -ati[:, 1], ati[:, 1]], axis=-1)
    return m.astype(BF16), w.astype(BF16), v.astype(BF16), ca, cb


def _mix_kernel(x_ref, mod_ref, a_ref, yf_ref, yb_ref, z_ref, gs_ref, y5_ref, u5_ref, d5_ref, wg_ref, bg_ref,
                f_ref, wo_ref, g2_ref, wr_ref, br_ref, o_ref, r_ref):
    mod = mod_ref[0]
    d = x_ref.shape[1]
    ssd = _rms((yf_ref[0] + yb_ref[0]) * _silu(z_ref[0]), SSD_INNER) * gs_ref[...]
    y5 = y5_ref[0] + d5_ref[...] * u5_ref[0]
    g5 = jax.nn.gelu(y5)
    s5 = g5 * _sigmoid(_dot(g5.astype(BF16), wg_ref[...]) + bg_ref[...])
    acc = _dot(a_ref[0].astype(BF16), wo_ref[0:256, :])
    acc += _dot(ssd.astype(BF16), wo_ref[256:512, :])
    acc += _dot(s5.astype(BF16), wo_ref[512:768, :])
    acc += _dot(f_ref[0].astype(BF16), wo_ref[768:1024, :])
    xn = x_ref[...] + mod[2:3] * acc
    h = _rms(xn, d) * g2_ref[...]
    h = h * (1.0 + mod[4:5]) + mod[3:4]
    route = _route(h, wr_ref, br_ref)
    o_ref[0, :, :d] = xn
    o_ref[0, :, d:] = route
    r_ref[0] = route


def _mix(x, layer, modtab, a, yf, yb, ssd_raw, gs, y5, u5, d5, wglu, bglu, f, wo, g2, wr, br):
    (b, l, _), d = a.shape, x.shape[1]
    nt = l // TOK_TILE
    tok = lambda c: pl.BlockSpec((1, TOK_TILE, c), lambda i, j: (i, j, 0))
    full = lambda arr: pl.BlockSpec(arr.shape, lambda i, j: (0,) * arr.ndim)
    return pl.pallas_call(
        _mix_kernel,
        grid=(b, nt),
        in_specs=[pl.BlockSpec((TOK_TILE, d), lambda i, j: (i * nt + j, 0)),
                  pl.BlockSpec((1, 8, d), lambda i, j: (2 * i + jnp.minimum(j, 1), 0, 0)),
                  tok(256), tok(256), tok(256), tok(256), full(gs), tok(256), tok(256), full(d5), full(wglu),
                  full(bglu), tok(256), pl.BlockSpec((None,) + wo.shape[1:], lambda i, j: (layer, 0, 0)),
                  full(g2), full(wr), full(br)],
        out_specs=[tok(d + LANES), tok(LANES)],
        out_shape=[jax.ShapeDtypeStruct((b, l, d + LANES), F32), jax.ShapeDtypeStruct((b, l, LANES), F32)],
        compiler_params=_cparams(("parallel", "arbitrary")),
        name="mix_out",
    )(x, modtab, a, yf, yb, ssd_raw, gs, y5, u5, d5, wglu, bglu, f, wo, g2, wr, br)


def _route(h, wr_ref, br_ref):
    lane = lax.broadcasted_iota(jnp.int32, (1, LANES), 1)
    lanef = lane.astype(F32)
    logits = _dot_hi(h, wr_ref[...]) + br_ref[...]
    neg = -jnp.inf
    big = float(4 * LANES)
    first = lambda hit: jnp.min(jnp.where(hit, lanef, big), axis=-1, keepdims=True)
    glog = jnp.where(lane < MOE_GROUPS, logits, neg)
    gmax = jnp.max(glog, axis=-1, keepdims=True)
    gidx = first(glog == gmax)
    p_group = 1.0 / jnp.sum(jnp.exp(glog - gmax), axis=-1, keepdims=True)
    e0 = MOE_GROUPS + gidx * MOE_PER_GROUP
    elog = jnp.where((lanef >= e0) & (lanef < e0 + MOE_PER_GROUP), logits, neg)
    v1 = jnp.max(elog, axis=-1, keepdims=True)
    i1 = first(elog == v1)
    elog2 = jnp.where(lanef == i1, neg, elog)
    v2 = jnp.max(elog2, axis=-1, keepdims=True)
    i2 = first(elog2 == v2)
    e2 = jnp.exp(v2 - v1)
    w1 = p_group / (1.0 + e2)
    w2 = p_group * e2 / (1.0 + e2)
    return (jnp.where(lanef == i1 - e0, w1, 0.0) + jnp.where(lanef == i2 - e0, w2, 0.0)
            + jnp.where(lane == MOE_PER_GROUP, gidx, 0.0))


def _route_plan(gid, tm, nt):
    t = gid.shape[0]
    onehot = (gid[:, None] == jnp.arange(MOE_GROUPS, dtype=jnp.int32)[None]).astype(jnp.int32)
    csum = jnp.cumsum(onehot, axis=0)
    rank = jnp.sum((csum - onehot) * onehot, axis=1)
    padded = (csum[-1] + tm - 1) // tm * tm
    ends = jnp.cumsum(padded)
    slot = jnp.sum(onehot * (ends - padded)[None], axis=1) + rank
    src = jnp.full((nt * tm,), -1, jnp.int32).at[slot].set(jnp.arange(t, dtype=jnp.int32))
    pos = jnp.arange(nt * tm, dtype=jnp.int32)
    spare = t + (pos // tm % 2) * tm + pos % tm
    dst = jnp.concatenate([t + tm + pos[:tm], jnp.where(src >= 0, src, spare)])
    starts = jnp.arange(nt, dtype=jnp.int32) * tm
    tile_group = jnp.minimum(jnp.sum((starts[:, None] >= ends[None]).astype(jnp.int32), axis=1), MOE_GROUPS - 1)
    return src, dst, tile_group


def _moe_kernel(src_s, dst_s, tg_s, srcv_ref, xg_hbm, mod_ref, g2_ref, wg_ref, wu_ref, wd_ref, out_hbm,
                gbuf0, gbuf1, ybuf0, ybuf1, gsem, ssem, *, seq_len, n_ctx):
    del tg_s
    i = pl.program_id(0)
    nt = pl.num_programs(0)
    gbufs, ybufs = (gbuf0, gbuf1), (ybuf0, ybuf1)
    tm, d = ybuf0.shape
    nxt = jnp.minimum(i + 1, nt - 1)

    def gather_copy(tile, sl, r):
        t = jnp.maximum(src_s[tile * tm + r], 0)
        return pltpu.make_async_copy(xg_hbm.at[pl.ds(t, 1)], gbufs[sl].at[pl.ds(r, 1)], gsem.at[sl])

    def scatter_copy(tile, sl, r):
        t = dst_s[(tile + 1) * tm + r]
        return pltpu.make_async_copy(ybufs[sl].at[pl.ds(r, 1)], out_hbm.at[pl.ds(t, 1)], ssem.at[sl])

    def each_row(make_copy, tile, sl, wait):
        def body(r, c):
            cp = make_copy(tile, sl, r)
            cp.wait() if wait else cp.start()
            return c
        lax.fori_loop(0, tm, body, 0, unroll=8)

    @pl.when(i == 0)
    def _():
        each_row(gather_copy, 0, 0, False)
        n_tok = out_hbm.shape[0] - 2 * tm
        for sl in range(2):
            ybufs[sl][...] = jnp.zeros((tm, d), F32)
        fills = [pltpu.make_async_copy(ybufs[sl], out_hbm.at[pl.ds(n_tok + sl * tm, tm)], ssem.at[sl])
                 for sl in range(2)]
        for cp in fills:
            cp.start()
        for cp in fills:
            cp.wait()

    def modrow(tok, r):
        out = jnp.zeros((tm, d), F32)
        for sg in range(mod_ref.shape[0]):
            lo_t = (sg // 2) * seq_len + (0 if sg % 2 == 0 else n_ctx)
            hi_t = (sg // 2) * seq_len + (n_ctx if sg % 2 == 0 else seq_len)
            out = jnp.where((tok >= lo_t) & (tok < hi_t), mod_ref[sg, r:r + 1, :], out)
        return out

    def tile_step(sl):
        ot = 1 - sl
        each_row(gather_copy, i, sl, True)

        @pl.when(i >= 1)
        def _():
            each_row(scatter_copy, i - 2, sl, True)

        x = gbufs[sl][:, :d]
        gates = gbufs[sl][:, d:]
        tok = jnp.maximum(srcv_ref[...], 0)
        h = _rms(x, d) * g2_ref[...]
        h = (h * (1.0 + modrow(tok, 4)) + modrow(tok, 3)).astype(BF16)
        y = jnp.zeros((tm, d), F32)
        rows_per_expert = tm // MOE_PER_GROUP
        for e in range(MOE_PER_GROUP):
            for r in range(e * rows_per_expert, (e + 1) * rows_per_expert):
                gather_copy(nxt, ot, r).start()
                scatter_copy(i - 1, ot, r).start()
            hid = _silu(_dot(h, wg_ref[e])) * _dot(h, wu_ref[e]) * gates[:, e:e + 1]
            y += _dot(hid.astype(BF16), wd_ref[e])
        ybufs[sl][...] = x + modrow(tok, 5) * y

        @pl.when(i == nt - 1)
        def _():
            each_row(scatter_copy, i, sl, False)
            each_row(scatter_copy, i - 1, ot, True)
            each_row(scatter_copy, i, sl, True)
            each_row(gather_copy, nxt, ot, True)

    for sl in range(2):
        pl.when(i % 2 == sl)(functools.partial(tile_step, sl))


def _moe(xg, src, dst, tile_group, modtab, g2, wg, wu, wd, layer, seq_len, n_ctx):
    t, wx = xg.shape
    d = wx - LANES
    tm = MOE_TILE
    nt = src.shape[0] // tm
    full = lambda arr: pl.BlockSpec(arr.shape, lambda i, s, ds, g: (0,) * arr.ndim)
    wspec = lambda arr: pl.BlockSpec((None,) + arr.shape[1:], lambda i, s, ds, g: (layer * MOE_GROUPS + g[i], 0, 0, 0))
    grid_spec = pltpu.PrefetchScalarGridSpec(
        num_scalar_prefetch=3,
        grid=(nt,),
        in_specs=[pl.BlockSpec((tm, 1), lambda i, s, ds, g: (i, 0)),
                  pl.BlockSpec(memory_space=pl.ANY),
                  full(modtab), full(g2), wspec(wg), wspec(wu), wspec(wd)],
        out_specs=pl.BlockSpec(memory_space=pl.ANY),
        scratch_shapes=[pltpu.VMEM((tm, wx), F32), pltpu.VMEM((tm, wx), F32),
                        pltpu.VMEM((tm, d), F32), pltpu.VMEM((tm, d), F32),
                        pltpu.SemaphoreType.DMA((2,)), pltpu.SemaphoreType.DMA((2,))])
    return pl.pallas_call(
        functools.partial(_moe_kernel, seq_len=seq_len, n_ctx=n_ctx),
        grid_spec=grid_spec,
        out_shape=jax.ShapeDtypeStruct((t + 2 * tm, d), F32),
        compiler_params=_cparams(("arbitrary",)),
        name="moe",
    )(src, dst, tile_group, src.reshape(-1, 1), xg, modtab, g2, wg, wu, wd)


def _pad_in_proj(w_in):
    depth, d, _ = w_in.shape
    pieces = [(MLA_QK, LANES - MLA_QK)] * MLA_HEADS
    pieces += [(MLA_KV_RANK, MLA_NOPE), (MLA_ROPE, LANES - MLA_QK)]
    pieces += [(SSD_COLS, P_S5 - P_SSD - SSD_COLS), (S5_COLS, 0), (512, 0)]
    pieces += [(DIFF_V, LANES - DIFF_V)] * DIFF_HEADS
    cols, o = [], 0
    for width, gap in pieces:
        cols.append(w_in[:, :, o:o + width])
        o += width
        if gap:
            cols.append(jnp.zeros((depth, d, gap), w_in.dtype))
    out = jnp.concatenate(cols, axis=-1)
    assert o == w_in.shape[2] and out.shape[2] == P_COLS
    return out


def _rope_tables(seq, n_ctx):
    pos = jnp.arange(seq, dtype=jnp.int32)
    rows = (pos // GRID_W).astype(F32)
    cols = (pos % GRID_W).astype(F32)
    half = MLA_ROPE // 2
    inv = ROPE_BASE ** (-jnp.arange(0, half, 2, dtype=F32) / half)
    r = jnp.arange(MLA_ROPE)
    freq = inv[r % (half // 2)]
    ang = jnp.where((r < half)[None, :], rows[:, None], cols[:, None]) * freq[None, :]
    cos, sin = jnp.cos(ang), jnp.sin(ang)
    first = ((r % half) < half // 2)[None, :]
    up = jnp.where(first, -sin, 0.0)
    dn = jnp.where(first, 0.0, sin)
    t32 = jnp.stack([cos, up, dn])
    ident = jnp.stack([jnp.ones((n_ctx, MLA_ROPE), F32), jnp.zeros((n_ctx, MLA_ROPE), F32),
                       jnp.zeros((n_ctx, MLA_ROPE), F32)])
    t32 = jnp.concatenate([ident, t32], axis=1)
    n = t32.shape[1]
    base = jnp.stack([jnp.ones((n, LANES), F32), jnp.zeros((n, LANES), F32), jnp.zeros((n, LANES), F32)])
    rope_m = base.at[:, :, MLA_NOPE:MLA_QK].set(t32)
    rope_d = jnp.tile(t32, (1, 1, LANES // MLA_ROPE))
    return rope_m, rope_d


def _pad_lanes(v, n):
    return jnp.zeros((1, n), F32).at[0, :v.shape[0]].set(v)


def kernel(x, c, ctx, c_ctx, w_ada, b_ada, norm1, norm2, w_in, w_out, mla_kv_norm, mla_w_uk, mla_w_uv, mla_q_norm, mla_k_norm, ssd_conv_w, ssd_conv_b, ssd_a_log, ssd_dt_bias, ssd_d, ssd_norm, s5_lam_re, s5_lam_im, s5_log_step, s5_b_re, s5_b_im, s5_c_re, s5_c_im, s5_d, s5_w_glu, s5_b_glu, diff_q_norm, diff_k_norm, diff_lq1, diff_lk1, diff_lq2, diff_lk2, diff_subln, moe_w_group, moe_b_group, moe_w_expert, moe_b_expert, moe_w_gate, moe_w_up, moe_w_down):
    bsz, seq, d = x.shape
    n_ctx = ctx.shape[1]
    depth = w_in.shape[0]
    assert n_ctx == TOK_TILE and (n_ctx + seq) % FLASH_KV_CHUNK == 0 and d == 1024
    l = n_ctx + seq
    assert (bsz * l) % MOE_TILE == 0 and bsz * l >= 2 * MOE_TILE and l % SSD_CHUNK == 0 and n_ctx % SSD_CHUNK == 0

    cvec = jnp.zeros((8, d), F32).at[:bsz].set(c).at[bsz].set(c_ctx)
    ada = _ada(cvec, w_ada, b_ada).reshape(depth, 8, N_MOD, d)
    mod_lat = ada[:, :bsz]
    mod_ctx = jnp.broadcast_to(ada[:, bsz][:, None], mod_lat.shape)
    modtab = jnp.stack([mod_ctx, mod_lat], axis=2)
    modtab = jnp.pad(modtab, ((0, 0), (0, 0), (0, 0), (0, 8 - N_MOD), (0, 0))).reshape(depth, bsz * 2, 8, d)

    w_in_p = _pad_in_proj(w_in).astype(BF16)
    w_out_b = w_out.astype(BF16)
    rope_m, rope_d = _rope_tables(seq, n_ctx)
    wuk = jnp.zeros((depth, MLA_KV_RANK, MLA_HEADS * LANES), F32)
    for hd in range(MLA_HEADS):
        wuk = wuk.at[:, :, hd * LANES:hd * LANES + MLA_NOPE].set(mla_w_uk[:, :, hd * MLA_NOPE:(hd + 1) * MLA_NOPE])
    wuk = wuk.astype(BF16)
    wuv = jnp.zeros((depth, MLA_KV_RANK, MLA_HEADS * LANES), F32)
    for hd in range(MLA_HEADS):
        wuv = wuv.at[:, :, hd * LANES:hd * LANES + MLA_V].set(mla_w_uv[:, :, hd * MLA_V:(hd + 1) * MLA_V])
    wuv = wuv.astype(BF16)
    amax = lambda g: jnp.max(jnp.abs(g.astype(F32)), axis=-1)
    bounded_m = (math.sqrt(MLA_QK) * amax(mla_q_norm) * amax(mla_k_norm) <= SCORE_BOUND).astype(F32)
    bounded_d = (math.sqrt(DIFF_HEAD) * amax(diff_q_norm) * amax(diff_k_norm) <= SCORE_BOUND).astype(F32)
    lane = jnp.arange(LANES)
    seg = (lane[:, None] // DIFF_HEAD == lane[None, :] // DIFF_HEAD).astype(F32)
    a_vec = -jnp.exp(ssd_a_log.astype(F32)).reshape(depth, 2 * SSD_HEADS)
    s5_m, s5_w, s5_v, s5_ca, s5_cb = _s5_matrices(
        s5_lam_re.astype(F32), s5_lam_im.astype(F32), s5_log_step.astype(F32),
        s5_b_re.astype(F32), s5_b_im.astype(F32), s5_c_re.astype(F32), s5_c_im.astype(F32))
    w_router = jnp.concatenate([moe_w_group, moe_w_expert.transpose(0, 2, 1, 3).reshape(depth, d, N_EXPERTS)], axis=-1)
    w_router = jnp.pad(w_router, ((0, 0), (0, 0), (0, LANES - w_router.shape[-1])))
    b_router = jnp.concatenate([moe_b_group, moe_b_expert.reshape(depth, N_EXPERTS)], axis=-1)
    b_router = jnp.pad(b_router, ((0, 0), (0, LANES - b_router.shape[-1])))
    by_group = lambda w: w.astype(BF16).reshape((depth * MOE_GROUPS, MOE_PER_GROUP) + w.shape[2:])
    wg_b, wu_b, wd_b = by_group(moe_w_gate), by_group(moe_w_up), by_group(moe_w_down)
    lam_all = (jnp.exp(jnp.sum(diff_lq1.astype(F32) * diff_lk1.astype(F32), axis=-1))
               - jnp.exp(jnp.sum(diff_lq2.astype(F32) * diff_lk2.astype(F32), axis=-1)))

    n5 = l // S5_CHUNK
    xs = jnp.concatenate([ctx, x], axis=1).reshape(bsz * l, d)
    for i in range(depth):
        lam_init = 0.8 - 0.6 * math.exp(-0.3 * i)
        qm, km, vm, qd, kd, vd, ssd_raw, u5 = _front(
            xs, bsz, i, modtab[i], norm1[i][None], w_in_p, rope_m, rope_d,
            _pad_lanes(mla_q_norm[i], LANES), _pad_lanes(mla_k_norm[i], LANES), mla_kv_norm[i][None],
            wuk[i], wuv[i], jnp.tile(diff_q_norm[i], 4)[None], jnp.tile(diff_k_norm[i], 4)[None], seg)
        a = _flash(jnp.stack([bounded_m[i], jnp.zeros((), F32)]), qm, km, vm, jnp.ones((1, LANES), F32),
                   n_sub=1, n_ctx=n_ctx, post=1.0)
        f = _flash(jnp.stack([bounded_d[i], lam_all[i] + lam_init]), qd, kd, vd, jnp.tile(diff_subln[i], 2)[None],
                   n_sub=2, n_ctx=n_ctx, post=1.0 - lam_init)
        yf, yb = _ssd(ssd_raw, ssd_conv_w[i], ssd_conv_b[i][None],
                      _pad_lanes(ssd_dt_bias[i].reshape(-1), LANES), _pad_lanes(a_vec[i], LANES),
                      jnp.repeat(ssd_d[i], SSD_HEAD_DIM)[None], n_ctx)
        u5g = u5.reshape(bsz, n5, S5_CHUNK, S5_GROUPS, S5_GROUP).transpose(3, 0, 1, 2, 4)
        u5g = u5g.reshape(S5_GROUPS, bsz * n5, S5_CHUNK * S5_GROUP)
        y1, s_loc = _s5_local(u5g, s5_m, s5_w, i)
        s_loc = s_loc.reshape(bsz * n5, S5_GROUPS, 4 * S5_STATE)
        s_prev = _s5_scan(s_loc, s5_ca[i], s5_cb[i], bsz, n_ctx // S5_CHUNK)
        y5 = _s5_read(y1, s_prev.reshape(1, bsz * n5, S5_GROUPS * 4 * S5_STATE), s5_v, i)
        y5 = y5.reshape(S5_GROUPS, bsz, n5, S5_CHUNK, S5_GROUP).transpose(1, 2, 3, 0, 4).reshape(bsz, l, S5_COLS)
        xg, ginfo = _mix(xs, i, modtab[i], a, yf, yb, ssd_raw, ssd_norm[i][None], y5, u5, s5_d[i][None],
                  s5_w_glu[i].astype(BF16), s5_b_glu[i][None], f, w_out_b,
                  norm2[i][None], w_router[i], b_router[i][None])
        xg = xg.reshape(bsz * l, d + LANES)
        gid = ginfo.reshape(bsz * l, LANES)[:, MOE_PER_GROUP].astype(jnp.int32)
        src, dst, tile_group = _route_plan(gid, MOE_TILE, bsz * l // MOE_TILE + MOE_GROUPS)
        xs = _moe(xg, src, dst, tile_group, modtab[i], norm2[i][None], wg_b, wu_b, wd_b, i, l, n_ctx)
    return xs[:bsz * l].reshape(bsz, l, d)[:, n_ctx:]
```

```python
import functools
import math

import jax
import jax.numpy as jnp
from jax import lax
from jax.experimental import pallas as pl
from jax.experimental.pallas import tpu as pltpu

F32 = jnp.float32
BF16 = jnp.bfloat16
HIGHEST = lax.Precision.HIGHEST

LANES = 128
EPS = 1e-6
ROPE_BASE = 10000.0
GRID_W = 64
N_MOD = 6

GROUP_WIDTH = 256
MLA_HEADS, MLA_NOPE, MLA_ROPE, MLA_V, MLA_KV_RANK = 4, 64, 32, 64, 128
MLA_QK = MLA_NOPE + MLA_ROPE
SSD_HEADS, SSD_HEAD_DIM, SSD_GROUPS, SSD_STATE, SSD_INNER = 4, 64, 2, 64, 256
SSD_GN = SSD_GROUPS * SSD_STATE
SSD_XBC = SSD_INNER + 2 * SSD_GN
S5_GROUPS, S5_GROUP, S5_STATE = 16, 16, 64
DIFF_HEADS, DIFF_HEAD, DIFF_V = 4, 32, 64
MOE_GROUPS, MOE_PER_GROUP, N_EXPERTS, EXPERT_HIDDEN = 4, 8, 32, 256

MLA_COLS = MLA_HEADS * MLA_QK + MLA_KV_RANK + MLA_ROPE
SSD_COLS = SSD_INNER + SSD_XBC + 2 * SSD_HEADS
S5_COLS = 256
DIFF_COLS = 768

P_MLA_Q = 0
P_MLA_CKV = 512
P_MLA_KR = 640
P_SSD = 768
P_S5 = 1664
P_DIFF = 1920
P_COLS = 2944
V_ONE = 64
SCORE_BOUND = 40.0
LOG2E = 1.4426950408889634

TOK_TILE = 256
SSD_CHUNK = 256
S5_CHUNK = 32
FLASH_KV_CHUNK = 1024
FLASH_Q_TILE = 768
MOE_TILE = 512
VMEM_LIMIT = 56 * 1024 * 1024


def _cparams(sem):
    return pltpu.CompilerParams(dimension_semantics=sem, vmem_limit_bytes=VMEM_LIMIT)


def _sigmoid(x):
    return 1.0 / (1.0 + jnp.exp(-x))


def _silu(x):
    return x * _sigmoid(x)


def _rms(x, n):
    return x * lax.rsqrt(jnp.sum(x * x, axis=-1, keepdims=True) * (1.0 / n) + EPS)


def _dot(a, b):
    return jnp.dot(a, b, preferred_element_type=F32)


def _dot_hi(a, b):
    return jnp.dot(a, b, preferred_element_type=F32, precision=HIGHEST)


def _dot_3pass(a, b):
    a_hi, b_hi = a.astype(BF16), b.astype(BF16)
    a_lo = (a - a_hi.astype(F32)).astype(BF16)
    b_lo = (b - b_hi.astype(F32)).astype(BF16)
    return _dot(a_hi, b_hi) + (_dot(a_hi, b_lo) + _dot(a_lo, b_hi))


def _dot_nt(a, b):
    return lax.dot_general(a, b, (((1,), (1,)), ((), ())), preferred_element_type=F32)


def _dot_tn(a, b):
    return lax.dot_general(a, b, (((0,), (0,)), ((), ())), preferred_element_type=F32)


def _dot_tn_hi(a, b):
    return lax.dot_general(a, b, (((0,), (0,)), ((), ())), preferred_element_type=F32, precision=HIGHEST)


def _ada_kernel(c_ref, w_ref, b_ref, o_ref):
    o_ref[0] = _dot_hi(_silu(c_ref[...]), w_ref[0]) + b_ref[0]


def _ada(cvec, w_ada, b_ada):
    depth, d, nd = w_ada.shape
    blk = 1024
    return pl.pallas_call(
        _ada_kernel,
        grid=(depth, nd // blk),
        in_specs=[pl.BlockSpec((8, d), lambda i, j: (0, 0)),
                  pl.BlockSpec((1, d, blk), lambda i, j: (i, 0, j)),
                  pl.BlockSpec((1, 1, blk), lambda i, j: (i, 0, j))],
        out_specs=pl.BlockSpec((1, 8, blk), lambda i, j: (i, 0, j)),
        out_shape=jax.ShapeDtypeStruct((depth, 8, nd), F32),
        compiler_params=_cparams(("arbitrary", "arbitrary")),
        name="ada",
    )(cvec, w_ada, b_ada.reshape(depth, 1, nd))


def _rope(x, cos, sin_up, sin_dn):
    n = x.shape[-1]
    return x * cos + pltpu.roll(x, n - 8, axis=1) * sin_up + pltpu.roll(x, 8, axis=1) * sin_dn


def _front_kernel(x_ref, mod_ref, g1_ref, w_ref, rm_ref, rd_ref, gq_ref, gk_ref, gkv_ref, wuk_ref, wuv_ref,
                  gdq_ref, gdk_ref, seg_ref, one_ref,
                  qm_ref, km_ref, vm_ref, qd_ref, kd_ref, vd_ref, ssd_ref, s5_ref):
    x = x_ref[...]
    mod = mod_ref[0]
    h = _rms(x, x.shape[-1]) * g1_ref[...]
    h = h * (1.0 + mod[1:2]) + mod[0:1]
    p = _dot(h.astype(BF16), w_ref[...])

    ssd_ref[0] = p[:, P_SSD:P_S5]
    s5_ref[0] = p[:, P_S5:P_DIFF]

    cos_m, up_m, dn_m = rm_ref[0], rm_ref[1], rm_ref[2]
    ckv = _rms(p[:, P_MLA_CKV:P_MLA_KR], MLA_KV_RANK) * gkv_ref[...]
    ckv = ckv.astype(BF16)
    k_nope = _dot(ckv, wuk_ref[...])
    vm_ref[0] = (_dot(ckv, wuv_ref[...]) + one_ref[...]).astype(BF16)
    k_rope = p[:, P_MLA_KR:P_SSD]
    q_scale = MLA_QK ** -0.5 * LOG2E
    for hd in range(MLA_HEADS):
        sl = slice(hd * LANES, (hd + 1) * LANES)
        q = _rms(p[:, sl], MLA_QK) * gq_ref[...]
        qm_ref[0, :, sl] = (_rope(q, cos_m, up_m, dn_m) * q_scale).astype(BF16)
        k = _rms(k_nope[:, sl] + k_rope, MLA_QK) * gk_ref[...]
        km_ref[0, :, sl] = _rope(k, cos_m, up_m, dn_m).astype(BF16)

    cos_d, up_d, dn_d = rd_ref[0], rd_ref[1], rd_ref[2]
    d_scale = DIFF_HEAD ** -0.5 * LOG2E
    for blk, (g_ref, o_ref, scale) in enumerate(((gdq_ref, qd_ref, d_scale), (gdk_ref, kd_ref, 1.0))):
        for half in range(2):
            sl = slice(P_DIFF + blk * 256 + half * LANES, P_DIFF + blk * 256 + (half + 1) * LANES)
            t = p[:, sl]
            ss = _dot_hi(t * t, seg_ref[...])
            t = t * lax.rsqrt(ss * (1.0 / DIFF_HEAD) + EPS) * g_ref[...]
            o_ref[0, :, half * LANES:(half + 1) * LANES] = (_rope(t, cos_d, up_d, dn_d) * scale).astype(BF16)
    vd_ref[0] = (p[:, P_DIFF + 512:P_COLS] + one_ref[...]).astype(BF16)


def _front(x, b, layer, modtab, g1, w_in_p, rope_m, rope_d, gq, gk, gkv, wuk, wuv, gdq, gdk, seg):
    l, d = rope_m.shape[1], x.shape[1]
    nt = l // TOK_TILE
    tok = lambda c: pl.BlockSpec((1, TOK_TILE, c), lambda i, j: (i, j, 0))
    full = lambda a: pl.BlockSpec(a.shape, lambda i, j: (0,) * a.ndim)
    rope_spec = pl.BlockSpec((3, TOK_TILE, LANES), lambda i, j: (0, j, 0))
    one = (jnp.arange(4 * LANES) % LANES == V_ONE).astype(F32)[None]
    outs = [(512, BF16), (512, BF16), (512, BF16), (256, BF16), (256, BF16), (512, BF16),
            (P_S5 - P_SSD, F32), (256, F32)]
    return pl.pallas_call(
        _front_kernel,
        grid=(b, nt),
        in_specs=[pl.BlockSpec((TOK_TILE, d), lambda i, j: (i * nt + j, 0)),
                  pl.BlockSpec((1, 8, d), lambda i, j: (2 * i + jnp.minimum(j, 1), 0, 0)),
                  full(g1), pl.BlockSpec((None,) + w_in_p.shape[1:], lambda i, j: (layer, 0, 0)), rope_spec, rope_spec,
                  full(gq), full(gk), full(gkv), full(wuk), full(wuv), full(gdq), full(gdk), full(seg), full(one)],
        out_specs=[tok(c) for c, _ in outs],
        out_shape=[jax.ShapeDtypeStruct((b, l, c), dt) for c, dt in outs],
        compiler_params=_cparams(("parallel", "arbitrary")),
        name="front",
    )(x, modtab, g1, w_in_p, rope_m, rope_d, gq, gk, gkv, wuk, wuv, gdq, gdk, seg, one)


def _flash_kernel(scal_ref, q_ref, k_ref, v_ref, sub_ref, o_ref, acc_scr, m_scr, *, n_sub, n_ctx, ck, post):
    qi = pl.program_id(2)
    lk = k_ref.shape[1]
    lane = lax.broadcasted_iota(jnp.int32, (1, LANES), 1)
    lo = lane < V_ONE
    q = q_ref[0]
    maps = [(a, c) for a in range(2) for c in range(n_sub)]

    def q_of(a, c):
        if n_sub == 1:
            return q[:, a * LANES:(a + 1) * LANES]
        s = (a * n_sub + c) * DIFF_HEAD
        return jnp.where((lane >= s) & (lane < s + DIFF_HEAD), q, jnp.zeros_like(q))

    qs = [q_of(a, c) for a, c in maps]

    def scores(mi, start, size):
        a = maps[mi][0]
        kk = k_ref[0, pl.ds(start, size), a * LANES:(a + 1) * LANES] if n_sub == 1 else k_ref[0, pl.ds(start, size), :]
        return _dot_nt(qs[mi], kk)

    def values(mi, start, size):
        a = maps[mi][0]
        return v_ref[0, pl.ds(start, size), a * LANES:(a + 1) * LANES]

    row = lax.broadcasted_iota(jnp.int32, (q.shape[0], 1), 0)
    sees_latent = (qi > 0) | (row >= n_ctx)

    def step_bounded(start, size, latent_keys):
        for mi in range(len(maps)):
            p = jnp.exp2(scores(mi, start, size))
            if latent_keys:
                p = jnp.where(sees_latent, p, 0.0)
            acc_scr[mi] += _dot(p.astype(BF16), values(mi, start, size))

    def step_online(start, size, latent_keys):
        for mi in range(len(maps)):
            s = scores(mi, start, size)
            if latent_keys:
                s = jnp.where(sees_latent, s, -jnp.inf)
            m_prev = m_scr[mi]
            m_new = jnp.maximum(m_prev, jnp.max(s, axis=-1, keepdims=True))
            p = jnp.exp2(s - m_new).astype(BF16)
            acc_scr[mi] = acc_scr[mi] * jnp.exp2(m_prev - m_new) + _dot(p, values(mi, start, size))
            m_scr[mi] = m_new

    def sweep(step):
        step(0, n_ctx, False)

        def body(i, carry):
            step(pl.multiple_of(n_ctx + i * ck, n_ctx), ck, True)
            return carry
        lax.fori_loop(0, (lk - n_ctx) // ck, body, 0)

    acc_scr[...] = jnp.zeros(acc_scr.shape, F32)
    bounded = scal_ref[0] != 0.0

    @pl.when(bounded)
    def _():
        sweep(step_bounded)

    @pl.when(jnp.logical_not(bounded))
    def _():
        m_scr[...] = jnp.full(m_scr.shape, -jnp.inf, F32)
        sweep(step_online)

    def normalised(mi):
        acc = acc_scr[mi]
        return acc * (1.0 / acc[:, V_ONE:V_ONE + 1])

    heads = []
    for a in range(2):
        if n_sub == 1:
            o = normalised(a)
        else:
            o = normalised(a * n_sub) - scal_ref[1] * normalised(a * n_sub + 1)
            ss = jnp.sum(jnp.where(lo, o * o, 0.0), axis=-1, keepdims=True)
            o = o * lax.rsqrt(ss * (1.0 / V_ONE) + EPS)
        heads.append(o)
    pair = jnp.where(lo, heads[0], pltpu.roll(heads[1], V_ONE, axis=1))
    o_ref[0] = pair if n_sub == 1 else pair * sub_ref[...] * post


def _flash(scal, q, k, v, subln, *, n_sub, n_ctx, post):
    b, l, qw = q.shape
    wq = qw // 2
    tq = FLASH_Q_TILE
    nq = l // tq
    kern = functools.partial(_flash_kernel, n_sub=n_sub, n_ctx=n_ctx, ck=FLASH_KV_CHUNK, post=post)
    return pl.pallas_call(
        kern,
        grid=(b, 2, nq),
        in_specs=[pl.BlockSpec(memory_space=pltpu.SMEM),
                  pl.BlockSpec((1, tq, wq), lambda i, h, j: (i, j, h)),
                  pl.BlockSpec((1, l, wq), lambda i, h, j: (i, 0, h)),
                  pl.BlockSpec((1, l, 2 * LANES), lambda i, h, j: (i, 0, h)),
                  pl.BlockSpec((1, LANES), lambda i, h, j: (0, 0))],
        out_specs=pl.BlockSpec((1, tq, LANES), lambda i, h, j: (i, j, h)),
        out_shape=jax.ShapeDtypeStruct((b, l, 2 * LANES), F32),
        scratch_shapes=[pltpu.VMEM((2 * n_sub, tq, LANES), F32),
                        pltpu.VMEM((2 * n_sub, tq, 1), F32)],
        compiler_params=_cparams(("parallel", "parallel", "arbitrary")),
        name="flash_diff" if n_sub == 2 else "flash_mla",
    )(scal, q, k, v, subln)


def _ssd_dir(d, first, xbc_ref, prev_ref, next_ref, edge_lo, edge_hi, cw_ref, cb_ref, dtb_ref, a_ref, dsk_ref,
             y_ref, st_scr):
    t = xbc_ref.shape[1]
    raw = xbc_ref[0]
    x = raw[:, SSD_INNER:SSD_INNER + SSD_XBC]
    row = lax.broadcasted_iota(jnp.int32, (t, 1), 0)
    prev_row = jnp.where(edge_lo, 0.0, prev_ref[0, 7:8, SSD_INNER:SSD_INNER + SSD_XBC])
    next_row = jnp.where(edge_hi, 0.0, next_ref[0, 0:1, SSD_INNER:SSD_INNER + SSD_XBC])
    xm1 = jnp.where(row == 0, prev_row, pltpu.roll(x, 1, axis=0))
    xp1 = jnp.where(row == t - 1, next_row, pltpu.roll(x, t - 1, axis=0))
    xc = _silu(cw_ref[0:1] * xm1 + cw_ref[1:2] * x + cw_ref[2:3] * xp1 + cb_ref[...])
    xs = xc[:, :SSD_INNER]
    bm = xc[:, SSD_INNER:SSD_INNER + SSD_GN]
    cm = xc[:, SSD_INNER + SSD_GN:]

    z = raw[:, SSD_INNER + SSD_XBC:] + dtb_ref[...]
    dt = jnp.maximum(z, 0.0) + jnp.log1p(jnp.exp(-jnp.abs(z)))
    da = dt * a_ref[...]

    ri = lax.broadcasted_iota(jnp.int32, (t, t), 0)
    ci = lax.broadcasted_iota(jnp.int32, (t, t), 1)
    mask = (ci <= ri) if d == 0 else (ci >= ri)
    mask_t = (ri <= ci) if d == 0 else (ri >= ci)
    cum = _dot_hi(mask.astype(F32), da)
    cum_t = _dot_tn_hi(da, mask_t.astype(F32))
    tot = jnp.sum(da, axis=0, keepdims=True)

    lane = lax.broadcasted_iota(jnp.int32, (1, LANES), 1)
    lo = lane < (LANES // 2)

    @pl.when(first)
    def _():
        st_scr[d] = jnp.zeros(st_scr.shape[1:], F32)

    ys = []
    for g in range(SSD_GROUPS):
        gmask = lo if g == 0 else jnp.logical_not(lo)
        cg = jnp.where(gmask, cm, 0.0)
        gram = _dot_nt(cg.astype(BF16), bm.astype(BF16))
        xpair = xs[:, g * LANES:(g + 1) * LANES]
        yh = []
        for hh in range(2):
            hd = g * 2 + hh
            col = d * SSD_HEADS + hd
            cum_c = cum[:, col:col + 1]
            cum_r = cum_t[col:col + 1, :]
            dt_c = dt[:, col:col + 1]
            tot_c = tot[:, col:col + 1]
            decay = jnp.exp(jnp.where(mask, cum_c - cum_r, -jnp.inf))
            xdt = (xpair * dt_c).astype(BF16)
            st = st_scr[d, hd]
            y = (_dot((gram * decay).astype(BF16), xdt)
                 + _dot((cg * jnp.exp(cum_c)).astype(BF16), st.astype(BF16)))
            st_scr[d, hd] = st * jnp.exp(tot_c) + _dot_tn((bm * jnp.exp(tot_c - cum_c)).astype(BF16), xdt)
            yh.append(y)
        ys.append(jnp.where(lo, yh[0], yh[1]))
    y = jnp.concatenate(ys, axis=1)
    if d == 0:
        y = y + xs * dsk_ref[...]
    y_ref[0] = y


def _ssd_kernel(xf_ref, pf_ref, nf_ref, xb_ref, pb_ref, nb_ref, cw_ref, cb_ref, dtb_ref, a_ref, dsk_ref,
                yf_ref, yb_ref, st_scr, *, n_ctx_chunks):
    s = pl.program_id(1)
    nc = pl.num_programs(1)
    cf = s
    cb = jnp.where(s < n_ctx_chunks, n_ctx_chunks - 1 - s, nc - 1 + n_ctx_chunks - s)
    seg_lo = lambda c: (c == 0) | (c == n_ctx_chunks)
    seg_hi = lambda c: (c == n_ctx_chunks - 1) | (c == nc - 1)
    _ssd_dir(0, s == 0, xf_ref, pf_ref, nf_ref, seg_lo(cf), seg_hi(cf), cw_ref, cb_ref, dtb_ref, a_ref, dsk_ref,
             yf_ref, st_scr)
    _ssd_dir(1, s == 0, xb_ref, pb_ref, nb_ref, seg_lo(cb), seg_hi(cb), cw_ref, cb_ref, dtb_ref, a_ref, dsk_ref,
             yb_ref, st_scr)


def _ssd(raw, conv_w, conv_b, dt_bias, a_vec, dskip, n_ctx):
    b, l, w = raw.shape
    t = SSD_CHUNK
    nc = l // t
    ncc = n_ctx // t
    r8 = t // 8
    nb8 = l // 8
    fwd = lambda s: s
    bwd = lambda s: jnp.where(s < ncc, ncc - 1 - s, nc - 1 + ncc - s)
    chunk = lambda f: pl.BlockSpec((1, t, w), lambda i, s: (i, f(s), 0))
    prev = lambda f: pl.BlockSpec((1, 8, w), lambda i, s: (i, jnp.maximum(f(s) * r8 - 1, 0), 0))
    nxt = lambda f: pl.BlockSpec((1, 8, w), lambda i, s: (i, jnp.minimum((f(s) + 1) * r8, nb8 - 1), 0))
    full = lambda a: pl.BlockSpec(a.shape, lambda i, s: (0,) * a.ndim)
    out = lambda f: pl.BlockSpec((1, t, SSD_INNER), lambda i, s: (i, f(s), 0))
    return pl.pallas_call(
        functools.partial(_ssd_kernel, n_ctx_chunks=ncc),
        grid=(b, nc),
        in_specs=[chunk(fwd), prev(fwd), nxt(fwd), chunk(bwd), prev(bwd), nxt(bwd),
                  full(conv_w), full(conv_b), full(dt_bias), full(a_vec), full(dskip)],
        out_specs=[out(fwd), out(bwd)],
        out_shape=[jax.ShapeDtypeStruct((b, l, SSD_INNER), F32)] * 2,
        scratch_shapes=[pltpu.VMEM((2, SSD_HEADS, LANES, LANES), F32)],
        compiler_params=_cparams(("parallel", "arbitrary")),
        name="ssd",
    )(raw, raw, raw, raw, raw, raw, conv_w, conv_b, dt_bias, a_vec, dskip)


def _s5_local_kernel(u_ref, m_ref, w_ref, y_ref, s_ref):
    u = u_ref[0].astype(BF16)
    y_ref[0] = _dot(u, m_ref[0])
    s_ref[0] = _dot(u, w_ref[0])


def _s5_local(u, m, w, layer):
    g, nch, wd = u.shape
    nw = wd
    ns = w.shape[3]
    return pl.pallas_call(
        _s5_local_kernel,
        grid=(g,),
        in_specs=[pl.BlockSpec((1, nch, wd), lambda i: (i, 0, 0)),
                  pl.BlockSpec((None, 1, wd, nw), lambda i: (layer, i, 0, 0)),
                  pl.BlockSpec((None, 1, wd, ns), lambda i: (layer, i, 0, 0))],
        out_specs=[pl.BlockSpec((1, nch, nw), lambda i: (i, 0, 0)),
                   pl.BlockSpec((1, nch, ns), lambda i: (0, 0, i))],
        out_shape=[jax.ShapeDtypeStruct((g, nch, nw), F32), jax.ShapeDtypeStruct((1, nch, g * ns), F32)],
        compiler_params=_cparams(("parallel",)),
        name="s5_local",
    )(u, m, w)


def _s5_scan_kernel(s_ref, ca_ref, cb_ref, o_ref, *, n_ctx_chunks):
    n = s_ref.shape[0]
    ca = ca_ref[...]
    cb = cb_ref[...]
    lane = lax.broadcasted_iota(jnp.int32, (1, 2 * LANES), 1)
    is_f = lane < LANES

    def swap(v):
        return jnp.concatenate([pltpu.roll(v[:, :LANES], LANES // 2, axis=1),
                                pltpu.roll(v[:, LANES:], LANES // 2, axis=1)], axis=1)

    def body(i, st):
        jf = i
        jb = jnp.where(i < n_ctx_chunks, n_ctx_chunks - 1 - i, n - 1 + n_ctx_chunks - i)
        cur_f = s_ref[jf]
        cur_b = s_ref[jb]
        o_ref[jf, :, :LANES] = st[:, :LANES]
        o_ref[jb, :, LANES:] = st[:, LANES:]
        return st * ca + swap(st) * cb + jnp.where(is_f, cur_f, cur_b)

    lax.fori_loop(0, n, body, jnp.zeros(ca.shape, F32))


def _s5_scan(s_loc, ca, cb, n_batch, n_ctx_chunks):
    nch, g, ns = s_loc.shape
    n = nch // n_batch
    return pl.pallas_call(
        functools.partial(_s5_scan_kernel, n_ctx_chunks=n_ctx_chunks),
        grid=(n_batch,),
        in_specs=[pl.BlockSpec((n, g, ns), lambda i: (i, 0, 0)),
                  pl.BlockSpec(ca.shape, lambda i: (0, 0)),
                  pl.BlockSpec(cb.shape, lambda i: (0, 0))],
        out_specs=pl.BlockSpec((n, g, ns), lambda i: (i, 0, 0)),
        out_shape=jax.ShapeDtypeStruct((nch, g, ns), F32),
        compiler_params=_cparams(("parallel",)),
        name="s5_scan",
    )(s_loc, ca, cb)


def _s5_read_kernel(y1_ref, sp_ref, v_ref, y_ref):
    y_ref[0] = y1_ref[0] + _dot(sp_ref[0].astype(BF16), v_ref[0])


def _s5_read(y1, s_prev, vmat, layer):
    g, nch, nw = y1.shape
    ns = vmat.shape[2]
    return pl.pallas_call(
        _s5_read_kernel,
        grid=(g,),
        in_specs=[pl.BlockSpec((1, nch, nw), lambda i: (i, 0, 0)),
                  pl.BlockSpec((1, nch, ns), lambda i: (0, 0, i)),
                  pl.BlockSpec((None, 1, ns, nw), lambda i: (layer, i, 0, 0))],
        out_specs=pl.BlockSpec((1, nch, nw), lambda i: (i, 0, 0)),
        out_shape=jax.ShapeDtypeStruct((g, nch, nw), F32),
        compiler_params=_cparams(("parallel",)),
        name="s5_read",
    )(y1, s_prev, vmat)


def _s5_matrices(lam_re, lam_im, log_step, b_re, b_im, c_re, c_im):
    t = S5_CHUNK
    step = jnp.exp(log_step)[..., None]
    den = lam_re * lam_re + lam_im * lam_im
    er = jnp.exp(lam_re * step)
    ar, ai = er * jnp.cos(lam_im * step), er * jnp.sin(lam_im * step)
    nr, ni = ar - 1.0, ai
    fr = (nr * lam_re + ni * lam_im) / den
    fi = (ni * lam_re - nr * lam_im) / den
    bre, bim = b_re[:, None], b_im[:, None]
    br = fr[..., None] * bre - fi[..., None] * bim
    bi = fr[..., None] * bim + fi[..., None] * bre
    kk = jnp.arange(t + 1, dtype=F32)[:, None, None, None, None]
    ek = jnp.exp(kk * (lam_re * step)[None])
    pr, pi = ek * jnp.cos(kk * (lam_im * step)[None]), ek * jnp.sin(kk * (lam_im * step)[None])
    abr = pr[..., None] * br[None] - pi[..., None] * bi[None]
    abi = pr[..., None] * bi[None] + pi[..., None] * br[None]
    ein = functools.partial(jnp.einsum, precision=HIGHEST)
    kern = ein('dsgop,kdsgpj->kdsgoj', c_re, abr[:t]) - ein('dsgop,kdsgpj->kdsgoj', c_im, abi[:t])
    kf, kb = kern[:, :, 0], kern[:, :, 1]
    lag = jnp.concatenate([kb[:0:-1], (kf[0] + kb[0])[None], kf[1:]], axis=0)
    nd = lag.shape[1]
    flat = lag.transpose(1, 2, 4, 0, 3).reshape(nd, S5_GROUPS, S5_GROUP, (2 * t - 1) * S5_GROUP)
    m = jnp.stack([flat[..., (t - 1 - j) * S5_GROUP:(2 * t - 1 - j) * S5_GROUP] for j in range(t)], axis=2)
    m = m.reshape(nd, S5_GROUPS, t * S5_GROUP, t * S5_GROUP)
    wf = jnp.stack([abr[:t][::-1, :, 0], abi[:t][::-1, :, 0]], axis=0)
    wb = jnp.stack([abr[:t, :, 1], abi[:t, :, 1]], axis=0)
    w = jnp.concatenate([wf, wb], axis=0)
    w = w.transpose(2, 3, 1, 5, 0, 4).reshape(w.shape[2], S5_GROUPS, t * S5_GROUP, 4 * S5_STATE)
    cr, ci = c_re[None], c_im[None]
    vre = cr * pr[:, :, :, :, None, :] - ci * pi[:, :, :, :, None, :]
    vim = -(cr * pi[:, :, :, :, None, :] + ci * pr[:, :, :, :, None, :])
    vf = jnp.stack([vre[1:, :, 0], vim[1:, :, 0]], axis=0)
    vb = jnp.stack([vre[:0:-1, :, 1], vim[:0:-1, :, 1]], axis=0)
    v = jnp.concatenate([vf, vb], axis=0)
    v = v.transpose(2, 3, 0, 5, 1, 4).reshape(v.shape[2], S5_GROUPS, 4 * S5_STATE, t * S5_GROUP)
    atr, ati = pr[t], pi[t]
    ca = jnp.concatenate([atr[:, 0], atr[:, 0], atr[:, 1], atr[:, 1]], axis=-1)
    cb = jnp.concatenate([-ati[:, 0], ati[:, 0], -ati[:, 1], ati[:, 1]], axis=-1)
    return m.astype(BF16), w.astype(BF16), v.astype(BF16), ca, cb


def _mix_kernel(x_ref, mod_ref, a_ref, yf_ref, yb_ref, z_ref, gs_ref, y5_ref, u5_ref, d5_ref, wg_ref, bg_ref,
                f_ref, wo_ref, g2_ref, wr_ref, br_ref, o_ref, r_ref):
    mod = mod_ref[0]
    d = x_ref.shape[1]
    ssd = _rms((yf_ref[0] + yb_ref[0]) * _silu(z_ref[0]), SSD_INNER) * gs_ref[...]
    y5 = y5_ref[0] + d5_ref[...] * u5_ref[0]
    g5 = jax.nn.gelu(y5)
    s5 = g5 * _sigmoid(_dot(g5.astype(BF16), wg_ref[...]) + bg_ref[...])
    acc = _dot(a_ref[0].astype(BF16), wo_ref[0:256, :])
    acc += _dot(ssd.astype(BF16), wo_ref[256:512, :])
    acc += _dot(s5.astype(BF16), wo_ref[512:768, :])
    acc += _dot(f_ref[0].astype(BF16), wo_ref[768:1024, :])
    xn = x_ref[...] + mod[2:3] * acc
    h = _rms(xn, d) * g2_ref[...]
    h = h * (1.0 + mod[4:5]) + mod[3:4]
    route = _route(h, wr_ref, br_ref)
    o_ref[0, :, :d] = xn
    o_ref[0, :, d:] = route
    r_ref[0] = route


def _mix(x, layer, modtab, a, yf, yb, ssd_raw, gs, y5, u5, d5, wglu, bglu, f, wo, g2, wr, br):
    (b, l, _), d = a.shape, x.shape[1]
    nt = l // TOK_TILE
    tok = lambda c: pl.BlockSpec((1, TOK_TILE, c), lambda i, j: (i, j, 0))
    full = lambda arr: pl.BlockSpec(arr.shape, lambda i, j: (0,) * arr.ndim)
    return pl.pallas_call(
        _mix_kernel,
        grid=(b, nt),
        in_specs=[pl.BlockSpec((TOK_TILE, d), lambda i, j: (i * nt + j, 0)),
                  pl.BlockSpec((1, 8, d), lambda i, j: (2 * i + jnp.minimum(j, 1), 0, 0)),
                  tok(256), tok(256), tok(256), tok(256), full(gs), tok(256), tok(256), full(d5), full(wglu),
                  full(bglu), tok(256), pl.BlockSpec((None,) + wo.shape[1:], lambda i, j: (layer, 0, 0)),
                  full(g2), full(wr), full(br)],
        out_specs=[tok(d + LANES), tok(LANES)],
        out_shape=[jax.ShapeDtypeStruct((b, l, d + LANES), F32), jax.ShapeDtypeStruct((b, l, LANES), F32)],
        compiler_params=_cparams(("parallel", "arbitrary")),
        name="mix_out",
    )(x, modtab, a, yf, yb, ssd_raw, gs, y5, u5, d5, wglu, bglu, f, wo, g2, wr, br)


def _route(h, wr_ref, br_ref):
    lane = lax.broadcasted_iota(jnp.int32, (1, LANES), 1)
    lanef = lane.astype(F32)
    logits = _dot_3pass(h, wr_ref[...]) + br_ref[...]
    neg = -jnp.inf
    big = float(4 * LANES)
    first = lambda hit: jnp.min(jnp.where(hit, lanef, big), axis=-1, keepdims=True)
    glog = jnp.where(lane < MOE_GROUPS, logits, neg)
    gmax = jnp.max(glog, axis=-1, keepdims=True)
    gidx = first(glog == gmax)
    p_group = 1.0 / jnp.sum(jnp.exp(glog - gmax), axis=-1, keepdims=True)
    e0 = MOE_GROUPS + gidx * MOE_PER_GROUP
    elog = jnp.where((lanef >= e0) & (lanef < e0 + MOE_PER_GROUP), logits, neg)
    v1 = jnp.max(elog, axis=-1, keepdims=True)
    i1 = first(elog == v1)
    elog2 = jnp.where(lanef == i1, neg, elog)
    v2 = jnp.max(elog2, axis=-1, keepdims=True)
    i2 = first(elog2 == v2)
    e2 = jnp.exp(v2 - v1)
    w1 = p_group / (1.0 + e2)
    w2 = p_group * e2 / (1.0 + e2)
    return (jnp.where(lanef == i1 - e0, w1, 0.0) + jnp.where(lanef == i2 - e0, w2, 0.0)
            + jnp.where(lane == MOE_PER_GROUP, gidx, 0.0))


def _route_plan(gid, tm, nt):
    t = gid.shape[0]
    onehot = (gid[:, None] == jnp.arange(MOE_GROUPS, dtype=jnp.int32)[None]).astype(jnp.int32)
    csum = jnp.cumsum(onehot, axis=0)
    rank = jnp.sum((csum - onehot) * onehot, axis=1)
    padded = (csum[-1] + tm - 1) // tm * tm
    ends = jnp.cumsum(padded)
    slot = jnp.sum(onehot * (ends - padded)[None], axis=1) + rank
    src = jnp.full((nt * tm,), -1, jnp.int32).at[slot].set(jnp.arange(t, dtype=jnp.int32))
    pos = jnp.arange(nt * tm, dtype=jnp.int32)
    spare = t + (pos // tm % 2) * tm + pos % tm
    dst = jnp.concatenate([t + tm + pos[:tm], jnp.where(src >= 0, src, spare)])
    starts = jnp.arange(nt, dtype=jnp.int32) * tm
    tile_group = jnp.minimum(jnp.sum((starts[:, None] >= ends[None]).astype(jnp.int32), axis=1), MOE_GROUPS - 1)
    return src, dst, tile_group


def _moe_kernel(src_s, dst_s, tg_s, srcv_ref, xg_hbm, mod_ref, g2_ref, wg_ref, wu_ref, wd_ref, out_hbm,
                gbuf0, gbuf1, ybuf0, ybuf1, gsem, ssem, *, seq_len, n_ctx):
    del tg_s
    i = pl.program_id(0)
    nt = pl.num_programs(0)
    gbufs, ybufs = (gbuf0, gbuf1), (ybuf0, ybuf1)
    tm, d = ybuf0.shape
    nxt = jnp.minimum(i + 1, nt - 1)

    def gather_copy(tile, sl, r):
        t = jnp.maximum(src_s[tile * tm + r], 0)
        return pltpu.make_async_copy(xg_hbm.at[pl.ds(t, 1)], gbufs[sl].at[pl.ds(r, 1)], gsem.at[sl])

    def scatter_copy(tile, sl, r):
        t = dst_s[(tile + 1) * tm + r]
        return pltpu.make_async_copy(ybufs[sl].at[pl.ds(r, 1)], out_hbm.at[pl.ds(t, 1)], ssem.at[sl])

    def each_row(make_copy, tile, sl, wait):
        def body(r, c):
            cp = make_copy(tile, sl, r)
            cp.wait() if wait else cp.start()
            return c
        lax.fori_loop(0, tm, body, 0, unroll=8)

    @pl.when(i == 0)
    def _():
        each_row(gather_copy, 0, 0, False)
        n_tok = out_hbm.shape[0] - 2 * tm
        for sl in range(2):
            ybufs[sl][...] = jnp.zeros((tm, d), F32)
        fills = [pltpu.make_async_copy(ybufs[sl], out_hbm.at[pl.ds(n_tok + sl * tm, tm)], ssem.at[sl])
                 for sl in range(2)]
        for cp in fills:
            cp.start()
        for cp in fills:
            cp.wait()

    def modrow(tok, r):
        out = jnp.zeros((tm, d), F32)
        for sg in range(mod_ref.shape[0]):
            lo_t = (sg // 2) * seq_len + (0 if sg % 2 == 0 else n_ctx)
            hi_t = (sg // 2) * seq_len + (n_ctx if sg % 2 == 0 else seq_len)
            out = jnp.where((tok >= lo_t) & (tok < hi_t), mod_ref[sg, r:r + 1, :], out)
        return out

    def tile_step(sl):
        ot = 1 - sl
        each_row(gather_copy, i, sl, True)

        @pl.when(i >= 1)
        def _():
            each_row(scatter_copy, i - 2, sl, True)

        x = gbufs[sl][:, :d]
        gates = gbufs[sl][:, d:]
        tok = jnp.maximum(srcv_ref[...], 0)
        h = _rms(x, d) * g2_ref[...]
        h = (h * (1.0 + modrow(tok, 4)) + modrow(tok, 3)).astype(BF16)
        y = jnp.zeros((tm, d), F32)
        rows_per_expert = tm // MOE_PER_GROUP
        for e in range(MOE_PER_GROUP):
            for r in range(e * rows_per_expert, (e + 1) * rows_per_expert):
                gather_copy(nxt, ot, r).start()
                scatter_copy(i - 1, ot, r).start()
            hid = _silu(_dot(h, wg_ref[e])) * _dot(h, wu_ref[e]) * gates[:, e:e + 1]
            y += _dot(hid.astype(BF16), wd_ref[e])
        ybufs[sl][...] = x + modrow(tok, 5) * y

        @pl.when(i == nt - 1)
        def _():
            each_row(scatter_copy, i, sl, False)
            each_row(scatter_copy, i - 1, ot, True)
            each_row(scatter_copy, i, sl, True)
            each_row(gather_copy, nxt, ot, True)

    for sl in range(2):
        pl.when(i % 2 == sl)(functools.partial(tile_step, sl))


def _moe(xg, src, dst, tile_group, modtab, g2, wg, wu, wd, layer, seq_len, n_ctx):
    t, wx = xg.shape
    d = wx - LANES
    tm = MOE_TILE
    nt = src.shape[0] // tm
    full = lambda arr: pl.BlockSpec(arr.shape, lambda i, s, ds, g: (0,) * arr.ndim)
    wspec = lambda arr: pl.BlockSpec((None,) + arr.shape[1:], lambda i, s, ds, g: (layer * MOE_GROUPS + g[i], 0, 0, 0))
    grid_spec = pltpu.PrefetchScalarGridSpec(
        num_scalar_prefetch=3,
        grid=(nt,),
        in_specs=[pl.BlockSpec((tm, 1), lambda i, s, ds, g: (i, 0)),
                  pl.BlockSpec(memory_space=pl.ANY),
                  full(modtab), full(g2), wspec(wg), wspec(wu), wspec(wd)],
        out_specs=pl.BlockSpec(memory_space=pl.ANY),
        scratch_shapes=[pltpu.VMEM((tm, wx), F32), pltpu.VMEM((tm, wx), F32),
                        pltpu.VMEM((tm, d), F32), pltpu.VMEM((tm, d), F32),
                        pltpu.SemaphoreType.DMA((2,)), pltpu.SemaphoreType.DMA((2,))])
    return pl.pallas_call(
        functools.partial(_moe_kernel, seq_len=seq_len, n_ctx=n_ctx),
        grid_spec=grid_spec,
        out_shape=jax.ShapeDtypeStruct((t + 2 * tm, d), F32),
        compiler_params=_cparams(("arbitrary",)),
        name="moe",
    )(src, dst, tile_group, src.reshape(-1, 1), xg, modtab, g2, wg, wu, wd)


def _pad_in_proj(w_in):
    depth, d, _ = w_in.shape
    pieces = [(MLA_QK, LANES - MLA_QK)] * MLA_HEADS
    pieces += [(MLA_KV_RANK, MLA_NOPE), (MLA_ROPE, LANES - MLA_QK)]
    pieces += [(SSD_COLS, P_S5 - P_SSD - SSD_COLS), (S5_COLS, 0), (512, 0)]
    pieces += [(DIFF_V, LANES - DIFF_V)] * DIFF_HEADS
    cols, o = [], 0
    for width, gap in pieces:
        cols.append(w_in[:, :, o:o + width])
        o += width
        if gap:
            cols.append(jnp.zeros((depth, d, gap), w_in.dtype))
    out = jnp.concatenate(cols, axis=-1)
    assert o == w_in.shape[2] and out.shape[2] == P_COLS
    return out


def _rope_tables(seq, n_ctx):
    pos = jnp.arange(seq, dtype=jnp.int32)
    rows = (pos // GRID_W).astype(F32)
    cols = (pos % GRID_W).astype(F32)
    half = MLA_ROPE // 2
    inv = ROPE_BASE ** (-jnp.arange(0, half, 2, dtype=F32) / half)
    r = jnp.arange(MLA_ROPE)
    freq = inv[r % (half // 2)]
    ang = jnp.where((r < half)[None, :], rows[:, None], cols[:, None]) * freq[None, :]
    cos, sin = jnp.cos(ang), jnp.sin(ang)
    first = ((r % half) < half // 2)[None, :]
    up = jnp.where(first, -sin, 0.0)
    dn = jnp.where(first, 0.0, sin)
    t32 = jnp.stack([cos, up, dn])
    ident = jnp.stack([jnp.ones((n_ctx, MLA_ROPE), F32), jnp.zeros((n_ctx, MLA_ROPE), F32),
                       jnp.zeros((n_ctx, MLA_ROPE), F32)])
    t32 = jnp.concatenate([ident, t32], axis=1)
    n = t32.shape[1]
    base = jnp.stack([jnp.ones((n, LANES), F32), jnp.zeros((n, LANES), F32), jnp.zeros((n, LANES), F32)])
    rope_m = base.at[:, :, MLA_NOPE:MLA_QK].set(t32)
    rope_d = jnp.tile(t32, (1, 1, LANES // MLA_ROPE))
    return rope_m, rope_d


def _pad_lanes(v, n):
    return jnp.zeros((1, n), F32).at[0, :v.shape[0]].set(v)


def kernel(x, c, ctx, c_ctx, w_ada, b_ada, norm1, norm2, w_in, w_out, mla_kv_norm, mla_w_uk, mla_w_uv, mla_q_norm, mla_k_norm, ssd_conv_w, ssd_conv_b, ssd_a_log, ssd_dt_bias, ssd_d, ssd_norm, s5_lam_re, s5_lam_im, s5_log_step, s5_b_re, s5_b_im, s5_c_re, s5_c_im, s5_d, s5_w_glu, s5_b_glu, diff_q_norm, diff_k_norm, diff_lq1, diff_lk1, diff_lq2, diff_lk2, diff_subln, moe_w_group, moe_b_group, moe_w_expert, moe_b_expert, moe_w_gate, moe_w_up, moe_w_down):
    bsz, seq, d = x.shape
    n_ctx = ctx.shape[1]
    depth = w_in.shape[0]
    assert n_ctx == TOK_TILE and seq % FLASH_KV_CHUNK == 0 and (n_ctx + seq) % FLASH_Q_TILE == 0 and d == 1024
    assert FLASH_Q_TILE >= n_ctx
    l = n_ctx + seq
    assert (bsz * l) % MOE_TILE == 0 and bsz * l >= 2 * MOE_TILE and l % SSD_CHUNK == 0 and n_ctx % SSD_CHUNK == 0

    cvec = jnp.zeros((8, d), F32).at[:bsz].set(c).at[bsz].set(c_ctx)
    ada = _ada(cvec, w_ada, b_ada).reshape(depth, 8, N_MOD, d)
    mod_lat = ada[:, :bsz]
    mod_ctx = jnp.broadcast_to(ada[:, bsz][:, None], mod_lat.shape)
    modtab = jnp.stack([mod_ctx, mod_lat], axis=2)
    modtab = jnp.pad(modtab, ((0, 0), (0, 0), (0, 0), (0, 8 - N_MOD), (0, 0))).reshape(depth, bsz * 2, 8, d)

    w_in_p = _pad_in_proj(w_in).astype(BF16)
    w_out_b = w_out.astype(BF16)
    rope_m, rope_d = _rope_tables(seq, n_ctx)
    wuk = jnp.zeros((depth, MLA_KV_RANK, MLA_HEADS * LANES), F32)
    for hd in range(MLA_HEADS):
        wuk = wuk.at[:, :, hd * LANES:hd * LANES + MLA_NOPE].set(mla_w_uk[:, :, hd * MLA_NOPE:(hd + 1) * MLA_NOPE])
    wuk = wuk.astype(BF16)
    wuv = jnp.zeros((depth, MLA_KV_RANK, MLA_HEADS * LANES), F32)
    for hd in range(MLA_HEADS):
        wuv = wuv.at[:, :, hd * LANES:hd * LANES + MLA_V].set(mla_w_uv[:, :, hd * MLA_V:(hd + 1) * MLA_V])
    wuv = wuv.astype(BF16)
    amax = lambda g: jnp.max(jnp.abs(g.astype(F32)), axis=-1)
    bounded_m = (math.sqrt(MLA_QK) * amax(mla_q_norm) * amax(mla_k_norm) <= SCORE_BOUND).astype(F32)
    bounded_d = (math.sqrt(DIFF_HEAD) * amax(diff_q_norm) * amax(diff_k_norm) <= SCORE_BOUND).astype(F32)
    lane = jnp.arange(LANES)
    seg = (lane[:, None] // DIFF_HEAD == lane[None, :] // DIFF_HEAD).astype(F32)
    a_vec = -jnp.exp(ssd_a_log.astype(F32)).reshape(depth, 2 * SSD_HEADS)
    s5_m, s5_w, s5_v, s5_ca, s5_cb = _s5_matrices(
        s5_lam_re.astype(F32), s5_lam_im.astype(F32), s5_log_step.astype(F32),
        s5_b_re.astype(F32), s5_b_im.astype(F32), s5_c_re.astype(F32), s5_c_im.astype(F32))
    w_router = jnp.concatenate([moe_w_group, moe_w_expert.transpose(0, 2, 1, 3).reshape(depth, d, N_EXPERTS)], axis=-1)
    w_router = jnp.pad(w_router, ((0, 0), (0, 0), (0, LANES - w_router.shape[-1])))
    b_router = jnp.concatenate([moe_b_group, moe_b_expert.reshape(depth, N_EXPERTS)], axis=-1)
    b_router = jnp.pad(b_router, ((0, 0), (0, LANES - b_router.shape[-1])))
    by_group = lambda w: w.astype(BF16).reshape((depth * MOE_GROUPS, MOE_PER_GROUP) + w.shape[2:])
    wg_b, wu_b, wd_b = by_group(moe_w_gate), by_group(moe_w_up), by_group(moe_w_down)
    lam_all = (jnp.exp(jnp.sum(diff_lq1.astype(F32) * diff_lk1.astype(F32), axis=-1))
               - jnp.exp(jnp.sum(diff_lq2.astype(F32) * diff_lk2.astype(F32), axis=-1)))

    n5 = l // S5_CHUNK
    xs = jnp.concatenate([ctx, x], axis=1).reshape(bsz * l, d)
    for i in range(depth):
        lam_init = 0.8 - 0.6 * math.exp(-0.3 * i)
        qm, km, vm, qd, kd, vd, ssd_raw, u5 = _front(
            xs, bsz, i, modtab[i], norm1[i][None], w_in_p, rope_m, rope_d,
            _pad_lanes(mla_q_norm[i], LANES), _pad_lanes(mla_k_norm[i], LANES), mla_kv_norm[i][None],
            wuk[i], wuv[i], jnp.tile(diff_q_norm[i], 4)[None], jnp.tile(diff_k_norm[i], 4)[None], seg)
        a = _flash(jnp.stack([bounded_m[i], jnp.zeros((), F32)]), qm, km, vm, jnp.ones((1, LANES), F32),
                   n_sub=1, n_ctx=n_ctx, post=1.0)
        f = _flash(jnp.stack([bounded_d[i], lam_all[i] + lam_init]), qd, kd, vd, jnp.tile(diff_subln[i], 2)[None],
                   n_sub=2, n_ctx=n_ctx, post=1.0 - lam_init)
        yf, yb = _ssd(ssd_raw, ssd_conv_w[i], ssd_conv_b[i][None],
                      _pad_lanes(ssd_dt_bias[i].reshape(-1), LANES), _pad_lanes(a_vec[i], LANES),
                      jnp.repeat(ssd_d[i], SSD_HEAD_DIM)[None], n_ctx)
        u5g = u5.reshape(bsz, n5, S5_CHUNK, S5_GROUPS, S5_GROUP).transpose(3, 0, 1, 2, 4)
        u5g = u5g.reshape(S5_GROUPS, bsz * n5, S5_CHUNK * S5_GROUP)
        y1, s_loc = _s5_local(u5g, s5_m, s5_w, i)
        s_loc = s_loc.reshape(bsz * n5, S5_GROUPS, 4 * S5_STATE)
        s_prev = _s5_scan(s_loc, s5_ca[i], s5_cb[i], bsz, n_ctx // S5_CHUNK)
        y5 = _s5_read(y1, s_prev.reshape(1, bsz * n5, S5_GROUPS * 4 * S5_STATE), s5_v, i)
        y5 = y5.reshape(S5_GROUPS, bsz, n5, S5_CHUNK, S5_GROUP).transpose(1, 2, 3, 0, 4).reshape(bsz, l, S5_COLS)
        xg, ginfo = _mix(xs, i, modtab[i], a, yf, yb, ssd_raw, ssd_norm[i][None], y5, u5, s5_d[i][None],
                  s5_w_glu[i].astype(BF16), s5_b_glu[i][None], f, w_out_b,
                  norm2[i][None], w_router[i], b_router[i][None])
        xg = xg.reshape(bsz * l, d + LANES)
        gid = ginfo.reshape(bsz * l, LANES)[:, MOE_PER_GROUP].astype(jnp.int32)
        src, dst, tile_group = _route_plan(gid, MOE_TILE, bsz * l // MOE_TILE + MOE_GROUPS)
        xs = _moe(xg, src, dst, tile_group, modtab[i], norm2[i][None], wg_b, wu_b, wd_b, i, l, n_ctx)
    return xs[:bsz * l].reshape(bsz, l, d)[:, n_ctx:]
```

```python
import functools
import math

import jax
import jax.numpy as jnp
from jax import lax
from jax.experimental import pallas as pl
from jax.experimental.pallas import tpu as pltpu

F32 = jnp.float32
BF16 = jnp.bfloat16
HIGHEST = lax.Precision.HIGHEST

LANES = 128
EPS = 1e-6
ROPE_BASE = 10000.0
GRID_W = 64
N_MOD = 6

GROUP_WIDTH = 256
MLA_HEADS, MLA_NOPE, MLA_ROPE, MLA_V, MLA_KV_RANK = 4, 64, 32, 64, 128
MLA_QK = MLA_NOPE + MLA_ROPE
SSD_HEADS, SSD_HEAD_DIM, SSD_GROUPS, SSD_STATE, SSD_INNER = 4, 64, 2, 64, 256
SSD_GN = SSD_GROUPS * SSD_STATE
SSD_XBC = SSD_INNER + 2 * SSD_GN
S5_GROUPS, S5_GROUP, S5_STATE = 16, 16, 64
DIFF_HEADS, DIFF_HEAD, DIFF_V = 4, 32, 64
MOE_GROUPS, MOE_PER_GROUP, N_EXPERTS, EXPERT_HIDDEN = 4, 8, 32, 256

MLA_COLS = MLA_HEADS * MLA_QK + MLA_KV_RANK + MLA_ROPE
SSD_COLS = SSD_INNER + SSD_XBC + 2 * SSD_HEADS
S5_COLS = 256
DIFF_COLS = 768

P_MLA_Q = 0
P_MLA_CKV = 512
P_MLA_KR = 640
P_SSD = 768
P_S5 = 1664
P_DIFF = 1920
P_COLS = 2944
V_ONE = 64
SCORE_BOUND = 40.0
LOG2E = 1.4426950408889634

TOK_TILE = 256
SSD_CHUNK = 256
S5_CHUNK = 32
FLASH_KV_CHUNK = 1024
FLASH_Q_TILE = 1408
MOE_TILE = 512
VMEM_LIMIT = 56 * 1024 * 1024


def _cparams(sem):
    return pltpu.CompilerParams(dimension_semantics=sem, vmem_limit_bytes=VMEM_LIMIT)


def _sigmoid(x):
    return 1.0 / (1.0 + jnp.exp(-x))


def _silu(x):
    return x * _sigmoid(x)


def _rms(x, n):
    return x * lax.rsqrt(jnp.sum(x * x, axis=-1, keepdims=True) * (1.0 / n) + EPS)


def _dot(a, b):
    return jnp.dot(a, b, preferred_element_type=F32)


def _dot_hi(a, b):
    return jnp.dot(a, b, preferred_element_type=F32, precision=HIGHEST)


def _dot_3pass(a, b):
    a_hi, b_hi = a.astype(BF16), b.astype(BF16)
    a_lo = (a - a_hi.astype(F32)).astype(BF16)
    b_lo = (b - b_hi.astype(F32)).astype(BF16)
    return _dot(a_hi, b_hi) + (_dot(a_hi, b_lo) + _dot(a_lo, b_hi))


def _dot_nt(a, b):
    return lax.dot_general(a, b, (((1,), (1,)), ((), ())), preferred_element_type=F32)


def _dot_tn(a, b):
    return lax.dot_general(a, b, (((0,), (0,)), ((), ())), preferred_element_type=F32)


def _dot_tn_hi(a, b):
    return lax.dot_general(a, b, (((0,), (0,)), ((), ())), preferred_element_type=F32, precision=HIGHEST)


def _ada_kernel(c_ref, w_ref, b_ref, o_ref):
    o_ref[0] = _dot_hi(_silu(c_ref[...]), w_ref[0]) + b_ref[0]


def _ada(cvec, w_ada, b_ada):
    depth, d, nd = w_ada.shape
    blk = 1024
    return pl.pallas_call(
        _ada_kernel,
        grid=(depth, nd // blk),
        in_specs=[pl.BlockSpec((8, d), lambda i, j: (0, 0)),
                  pl.BlockSpec((1, d, blk), lambda i, j: (i, 0, j)),
                  pl.BlockSpec((1, 1, blk), lambda i, j: (i, 0, j))],
        out_specs=pl.BlockSpec((1, 8, blk), lambda i, j: (i, 0, j)),
        out_shape=jax.ShapeDtypeStruct((depth, 8, nd), F32),
        compiler_params=_cparams(("arbitrary", "arbitrary")),
        name="ada",
    )(cvec, w_ada, b_ada.reshape(depth, 1, nd))


def _rope(x, cos, sin_up, sin_dn):
    n = x.shape[-1]
    return x * cos + pltpu.roll(x, n - 8, axis=1) * sin_up + pltpu.roll(x, 8, axis=1) * sin_dn


def _front_kernel(x_ref, mod_ref, g1_ref, w_ref, rm_ref, rd_ref, gq_ref, gk_ref, gkv_ref, wuk_ref, wuv_ref,
                  gdq_ref, gdk_ref, seg_ref, one_ref,
                  qm_ref, km_ref, vm_ref, qd_ref, kd_ref, vd_ref, ssd_ref, s5_ref):
    x = x_ref[...]
    mod = mod_ref[0]
    h = _rms(x, x.shape[-1]) * g1_ref[...]
    h = h * (1.0 + mod[1:2]) + mod[0:1]
    p = _dot(h.astype(BF16), w_ref[...])

    ssd_ref[0] = p[:, P_SSD:P_S5]
    s5_ref[0] = p[:, P_S5:P_DIFF]

    cos_m, up_m, dn_m = rm_ref[0], rm_ref[1], rm_ref[2]
    ckv = _rms(p[:, P_MLA_CKV:P_MLA_KR], MLA_KV_RANK) * gkv_ref[...]
    ckv = ckv.astype(BF16)
    k_nope = _dot(ckv, wuk_ref[...])
    vm_ref[0] = (_dot(ckv, wuv_ref[...]) + one_ref[...]).astype(BF16)
    k_rope = p[:, P_MLA_KR:P_SSD]
    q_scale = MLA_QK ** -0.5 * LOG2E
    for hd in range(MLA_HEADS):
        sl = slice(hd * LANES, (hd + 1) * LANES)
        q = _rms(p[:, sl], MLA_QK) * gq_ref[...]
        qm_ref[0, :, sl] = (_rope(q, cos_m, up_m, dn_m) * q_scale).astype(BF16)
        k = _rms(k_nope[:, sl] + k_rope, MLA_QK) * gk_ref[...]
        km_ref[0, :, sl] = _rope(k, cos_m, up_m, dn_m).astype(BF16)

    cos_d, up_d, dn_d = rd_ref[0], rd_ref[1], rd_ref[2]
    d_scale = DIFF_HEAD ** -0.5 * LOG2E
    for blk, (g_ref, o_ref, scale) in enumerate(((gdq_ref, qd_ref, d_scale), (gdk_ref, kd_ref, 1.0))):
        for half in range(2):
            sl = slice(P_DIFF + blk * 256 + half * LANES, P_DIFF + blk * 256 + (half + 1) * LANES)
            t = p[:, sl]
            ss = _dot_hi(t * t, seg_ref[...])
            t = t * lax.rsqrt(ss * (1.0 / DIFF_HEAD) + EPS) * g_ref[...]
            o_ref[0, :, half * LANES:(half + 1) * LANES] = (_rope(t, cos_d, up_d, dn_d) * scale).astype(BF16)
    vd_ref[0] = (p[:, P_DIFF + 512:P_COLS] + one_ref[...]).astype(BF16)


def _front(x, b, layer, modtab, g1, w_in_p, rope_m, rope_d, gq, gk, gkv, wuk, wuv, gdq, gdk, seg):
    l, d = rope_m.shape[1], x.shape[1]
    nt = l // TOK_TILE
    tok = lambda c: pl.BlockSpec((1, TOK_TILE, c), lambda i, j: (i, j, 0))
    full = lambda a: pl.BlockSpec(a.shape, lambda i, j: (0,) * a.ndim)
    rope_spec = pl.BlockSpec((3, TOK_TILE, LANES), lambda i, j: (0, j, 0))
    one = (jnp.arange(4 * LANES) % LANES == V_ONE).astype(F32)[None]
    outs = [(512, BF16), (512, BF16), (512, BF16), (256, BF16), (256, BF16), (512, BF16),
            (P_S5 - P_SSD, F32), (256, F32)]
    return pl.pallas_call(
        _front_kernel,
        grid=(b, nt),
        in_specs=[pl.BlockSpec((TOK_TILE, d), lambda i, j: (i * nt + j, 0)),
                  pl.BlockSpec((1, 8, d), lambda i, j: (2 * i + jnp.minimum(j, 1), 0, 0)),
                  full(g1), pl.BlockSpec((None,) + w_in_p.shape[1:], lambda i, j: (layer, 0, 0)), rope_spec, rope_spec,
                  full(gq), full(gk), full(gkv), full(wuk), full(wuv), full(gdq), full(gdk), full(seg), full(one)],
        out_specs=[tok(c) for c, _ in outs],
        out_shape=[jax.ShapeDtypeStruct((b, l, c), dt) for c, dt in outs],
        compiler_params=_cparams(("parallel", "arbitrary")),
        name="front",
    )(x, modtab, g1, w_in_p, rope_m, rope_d, gq, gk, gkv, wuk, wuv, gdq, gdk, seg, one)


def _flash_kernel(scal_ref, q_ref, k_ref, v_ref, sub_ref, o_ref, acc_scr, m_scr, *, n_sub, n_ctx, ck, post):
    qi = pl.program_id(2)
    lk = k_ref.shape[1]
    lane = lax.broadcasted_iota(jnp.int32, (1, LANES), 1)
    lo = lane < V_ONE
    q = q_ref[0]
    maps = [(a, c) for a in range(2) for c in range(n_sub)]

    def q_of(a, c):
        if n_sub == 1:
            return q[:, a * LANES:(a + 1) * LANES]
        s = (a * n_sub + c) * DIFF_HEAD
        return jnp.where((lane >= s) & (lane < s + DIFF_HEAD), q, jnp.zeros_like(q))

    qs = [q_of(a, c) for a, c in maps]

    def scores(mi, start, size):
        a = maps[mi][0]
        kk = k_ref[0, pl.ds(start, size), a * LANES:(a + 1) * LANES] if n_sub == 1 else k_ref[0, pl.ds(start, size), :]
        return _dot_nt(qs[mi], kk)

    def values(mi, start, size):
        a = maps[mi][0]
        return v_ref[0, pl.ds(start, size), a * LANES:(a + 1) * LANES]

    row = lax.broadcasted_iota(jnp.int32, (q.shape[0], 1), 0)
    sees_latent = (qi > 0) | (row >= n_ctx)

    def step_bounded(start, size, latent_keys):
        for mi in range(len(maps)):
            p = jnp.exp2(scores(mi, start, size))
            if latent_keys:
                p = jnp.where(sees_latent, p, 0.0)
            acc_scr[mi] += _dot(p.astype(BF16), values(mi, start, size))

    def step_online(start, size, latent_keys):
        for mi in range(len(maps)):
            s = scores(mi, start, size)
            if latent_keys:
                s = jnp.where(sees_latent, s, -jnp.inf)
            m_prev = m_scr[mi]
            m_new = jnp.maximum(m_prev, jnp.max(s, axis=-1, keepdims=True))
            p = jnp.exp2(s - m_new).astype(BF16)
            acc_scr[mi] = acc_scr[mi] * jnp.exp2(m_prev - m_new) + _dot(p, values(mi, start, size))
            m_scr[mi] = m_new

    def sweep(step):
        step(0, n_ctx, False)

        def body(i, carry):
            step(pl.multiple_of(n_ctx + i * ck, n_ctx), ck, True)
            return carry
        lax.fori_loop(0, (lk - n_ctx) // ck, body, 0)

    acc_scr[...] = jnp.zeros(acc_scr.shape, F32)
    bounded = scal_ref[0] != 0.0

    @pl.when(bounded)
    def _():
        sweep(step_bounded)

    @pl.when(jnp.logical_not(bounded))
    def _():
        m_scr[...] = jnp.full(m_scr.shape, -jnp.inf, F32)
        sweep(step_online)

    def normalised(mi):
        acc = acc_scr[mi]
        return acc * (1.0 / acc[:, V_ONE:V_ONE + 1])

    heads = []
    for a in range(2):
        if n_sub == 1:
            o = normalised(a)
        else:
            o = normalised(a * n_sub) - scal_ref[1] * normalised(a * n_sub + 1)
            ss = jnp.sum(jnp.where(lo, o * o, 0.0), axis=-1, keepdims=True)
            o = o * lax.rsqrt(ss * (1.0 / V_ONE) + EPS)
        heads.append(o)
    pair = jnp.where(lo, heads[0], pltpu.roll(heads[1], V_ONE, axis=1))
    o_ref[0] = pair if n_sub == 1 else pair * sub_ref[...] * post


def _flash(scal, q, k, v, subln, *, n_sub, n_ctx, post):
    b, l, qw = q.shape
    wq = qw // 2
    tq = FLASH_Q_TILE
    nq = l // tq
    kern = functools.partial(_flash_kernel, n_sub=n_sub, n_ctx=n_ctx, ck=FLASH_KV_CHUNK, post=post)
    return pl.pallas_call(
        kern,
        grid=(b, 2, nq),
        in_specs=[pl.BlockSpec(memory_space=pltpu.SMEM),
                  pl.BlockSpec((1, tq, wq), lambda i, h, j: (i, j, h)),
                  pl.BlockSpec((1, l, wq), lambda i, h, j: (i, 0, h)),
                  pl.BlockSpec((1, l, 2 * LANES), lambda i, h, j: (i, 0, h)),
                  pl.BlockSpec((1, LANES), lambda i, h, j: (0, 0))],
        out_specs=pl.BlockSpec((1, tq, LANES), lambda i, h, j: (i, j, h)),
        out_shape=jax.ShapeDtypeStruct((b, l, 2 * LANES), F32),
        scratch_shapes=[pltpu.VMEM((2 * n_sub, tq, LANES), F32),
                        pltpu.VMEM((2 * n_sub, tq, 1), F32)],
        compiler_params=_cparams(("parallel", "parallel", "arbitrary")),
        name="flash_diff" if n_sub == 2 else "flash_mla",
    )(scal, q, k, v, subln)


def _ssd_dir(d, first, xbc_ref, prev_ref, next_ref, edge_lo, edge_hi, cw_ref, cb_ref, dtb_ref, a_ref, dsk_ref,
             y_ref, st_scr):
    t = xbc_ref.shape[1]
    raw = xbc_ref[0]
    x = raw[:, SSD_INNER:SSD_INNER + SSD_XBC]
    row = lax.broadcasted_iota(jnp.int32, (t, 1), 0)
    prev_row = jnp.where(edge_lo, 0.0, prev_ref[0, 7:8, SSD_INNER:SSD_INNER + SSD_XBC])
    next_row = jnp.where(edge_hi, 0.0, next_ref[0, 0:1, SSD_INNER:SSD_INNER + SSD_XBC])
    xm1 = jnp.where(row == 0, prev_row, pltpu.roll(x, 1, axis=0))
    xp1 = jnp.where(row == t - 1, next_row, pltpu.roll(x, t - 1, axis=0))
    xc = _silu(cw_ref[0:1] * xm1 + cw_ref[1:2] * x + cw_ref[2:3] * xp1 + cb_ref[...])
    xs = xc[:, :SSD_INNER]
    bm = xc[:, SSD_INNER:SSD_INNER + SSD_GN]
    cm = xc[:, SSD_INNER + SSD_GN:]

    z = raw[:, SSD_INNER + SSD_XBC:] + dtb_ref[...]
    dt = jnp.maximum(z, 0.0) + jnp.log1p(jnp.exp(-jnp.abs(z)))
    da = dt * a_ref[...]

    ri = lax.broadcasted_iota(jnp.int32, (t, t), 0)
    ci = lax.broadcasted_iota(jnp.int32, (t, t), 1)
    mask = (ci <= ri) if d == 0 else (ci >= ri)
    mask_t = (ri <= ci) if d == 0 else (ri >= ci)
    cum = _dot_hi(mask.astype(F32), da)
    cum_t = _dot_tn_hi(da, mask_t.astype(F32))
    tot = jnp.sum(da, axis=0, keepdims=True)

    lane = lax.broadcasted_iota(jnp.int32, (1, LANES), 1)
    lo = lane < (LANES // 2)

    @pl.when(first)
    def _():
        st_scr[d] = jnp.zeros(st_scr.shape[1:], F32)

    ys = []
    for g in range(SSD_GROUPS):
        gmask = lo if g == 0 else jnp.logical_not(lo)
        cg = jnp.where(gmask, cm, 0.0)
        gram = _dot_nt(cg.astype(BF16), bm.astype(BF16))
        xpair = xs[:, g * LANES:(g + 1) * LANES]
        yh = []
        for hh in range(2):
            hd = g * 2 + hh
            col = d * SSD_HEADS + hd
            cum_c = cum[:, col:col + 1]
            cum_r = cum_t[col:col + 1, :]
            dt_c = dt[:, col:col + 1]
            tot_c = tot[:, col:col + 1]
            decay = jnp.exp(jnp.where(mask, cum_c - cum_r, -jnp.inf))
            xdt = (xpair * dt_c).astype(BF16)
            st = st_scr[d, hd]
            y = (_dot((gram * decay).astype(BF16), xdt)
                 + _dot((cg * jnp.exp(cum_c)).astype(BF16), st.astype(BF16)))
            st_scr[d, hd] = st * jnp.exp(tot_c) + _dot_tn((bm * jnp.exp(tot_c - cum_c)).astype(BF16), xdt)
            yh.append(y)
        ys.append(jnp.where(lo, yh[0], yh[1]))
    y = jnp.concatenate(ys, axis=1)
    if d == 0:
        y = y + xs * dsk_ref[...]
    y_ref[0] = y


def _ssd_kernel(xf_ref, pf_ref, nf_ref, xb_ref, pb_ref, nb_ref, cw_ref, cb_ref, dtb_ref, a_ref, dsk_ref,
                yf_ref, yb_ref, st_scr, *, n_ctx_chunks):
    s = pl.program_id(1)
    nc = pl.num_programs(1)
    cf = s
    cb = jnp.where(s < n_ctx_chunks, n_ctx_chunks - 1 - s, nc - 1 + n_ctx_chunks - s)
    seg_lo = lambda c: (c == 0) | (c == n_ctx_chunks)
    seg_hi = lambda c: (c == n_ctx_chunks - 1) | (c == nc - 1)
    _ssd_dir(0, s == 0, xf_ref, pf_ref, nf_ref, seg_lo(cf), seg_hi(cf), cw_ref, cb_ref, dtb_ref, a_ref, dsk_ref,
             yf_ref, st_scr)
    _ssd_dir(1, s == 0, xb_ref, pb_ref, nb_ref, seg_lo(cb), seg_hi(cb), cw_ref, cb_ref, dtb_ref, a_ref, dsk_ref,
             yb_ref, st_scr)


def _ssd(raw, conv_w, conv_b, dt_bias, a_vec, dskip, n_ctx):
    b, l, w = raw.shape
    t = SSD_CHUNK
    nc = l // t
    ncc = n_ctx // t
    r8 = t // 8
    nb8 = l // 8
    fwd = lambda s: s
    bwd = lambda s: jnp.where(s < ncc, ncc - 1 - s, nc - 1 + ncc - s)
    chunk = lambda f: pl.BlockSpec((1, t, w), lambda i, s: (i, f(s), 0))
    prev = lambda f: pl.BlockSpec((1, 8, w), lambda i, s: (i, jnp.maximum(f(s) * r8 - 1, 0), 0))
    nxt = lambda f: pl.BlockSpec((1, 8, w), lambda i, s: (i, jnp.minimum((f(s) + 1) * r8, nb8 - 1), 0))
    full = lambda a: pl.BlockSpec(a.shape, lambda i, s: (0,) * a.ndim)
    out = lambda f: pl.BlockSpec((1, t, SSD_INNER), lambda i, s: (i, f(s), 0))
    return pl.pallas_call(
        functools.partial(_ssd_kernel, n_ctx_chunks=ncc),
        grid=(b, nc),
        in_specs=[chunk(fwd), prev(fwd), nxt(fwd), chunk(bwd), prev(bwd), nxt(bwd),
                  full(conv_w), full(conv_b), full(dt_bias), full(a_vec), full(dskip)],
        out_specs=[out(fwd), out(bwd)],
        out_shape=[jax.ShapeDtypeStruct((b, l, SSD_INNER), F32)] * 2,
        scratch_shapes=[pltpu.VMEM((2, SSD_HEADS, LANES, LANES), F32)],
        compiler_params=_cparams(("parallel", "arbitrary")),
        name="ssd",
    )(raw, raw, raw, raw, raw, raw, conv_w, conv_b, dt_bias, a_vec, dskip)


def _s5_local_kernel(u_ref, m_ref, w_ref, y_ref, sf_ref, sb_ref):
    u = u_ref[0].astype(BF16)
    y_ref[0] = _dot(u, m_ref[0])
    s = _dot(u, w_ref[0])
    sf_ref[...] = s[:, :LANES]
    sb_ref[...] = s[:, LANES:]


def _s5_local(u, m, w, layer):
    g, nch, wd = u.shape
    nw = wd
    ns = w.shape[3]
    state = jax.ShapeDtypeStruct((nch, g * LANES), F32)
    return pl.pallas_call(
        _s5_local_kernel,
        grid=(g,),
        in_specs=[pl.BlockSpec((1, nch, wd), lambda i: (i, 0, 0)),
                  pl.BlockSpec((None, 1, wd, nw), lambda i: (layer, i, 0, 0)),
                  pl.BlockSpec((None, 1, wd, ns), lambda i: (layer, i, 0, 0))],
        out_specs=[pl.BlockSpec((1, nch, nw), lambda i: (i, 0, 0)),
                   pl.BlockSpec((nch, LANES), lambda i: (0, i)),
                   pl.BlockSpec((nch, LANES), lambda i: (0, i))],
        out_shape=[jax.ShapeDtypeStruct((g, nch, nw), F32), state, state],
        compiler_params=_cparams(("parallel",)),
        name="s5_local",
    )(u, m, w)


def _s5_scan_kernel(sf_ref, sb_ref, co_ref, of_ref, ob_ref, *, n_ctx_chunks):
    n, w = sf_ref.shape
    lane = lax.broadcasted_iota(jnp.int32, (1, w), 1)
    re_half = lane % LANES < LANES // 2

    def swap(v):
        return jnp.where(re_half, pltpu.roll(v, w - LANES // 2, axis=1), pltpu.roll(v, LANES // 2, axis=1))

    def body(i, st):
        st_f, st_b = st
        jf = i
        jb = jnp.where(i < n_ctx_chunks, n_ctx_chunks - 1 - i, n - 1 + n_ctx_chunks - i)
        of_ref[pl.ds(jf, 1), :] = st_f
        ob_ref[pl.ds(jb, 1), :] = st_b
        st_f = st_f * co_ref[0:1] + swap(st_f) * co_ref[1:2] + sf_ref[pl.ds(jf, 1), :]
        st_b = st_b * co_ref[2:3] + swap(st_b) * co_ref[3:4] + sb_ref[pl.ds(jb, 1), :]
        return st_f, st_b

    zero = jnp.zeros((1, w), F32)
    lax.fori_loop(0, n, body, (zero, zero))


def _s5_scan(sf, sb, coef, n_batch, n_ctx_chunks):
    nch, w = sf.shape
    n = nch // n_batch
    blk = pl.BlockSpec((n, w), lambda i: (i, 0))
    return pl.pallas_call(
        functools.partial(_s5_scan_kernel, n_ctx_chunks=n_ctx_chunks),
        grid=(n_batch,),
        in_specs=[blk, blk, pl.BlockSpec(coef.shape, lambda i: (0, 0))],
        out_specs=[blk, blk],
        out_shape=[jax.ShapeDtypeStruct((nch, w), F32)] * 2,
        compiler_params=_cparams(("parallel",)),
        name="s5_scan",
    )(sf, sb, coef)


def _s5_read_kernel(y1_ref, pf_ref, pb_ref, v_ref, y_ref):
    y_ref[0] = (y1_ref[0] + _dot(pf_ref[...].astype(BF16), v_ref[0, :LANES, :])
                + _dot(pb_ref[...].astype(BF16), v_ref[0, LANES:, :]))


def _s5_read(y1, prev_f, prev_b, vmat, layer):
    g, nch, nw = y1.shape
    ns = vmat.shape[2]
    state = pl.BlockSpec((nch, LANES), lambda i: (0, i))
    return pl.pallas_call(
        _s5_read_kernel,
        grid=(g,),
        in_specs=[pl.BlockSpec((1, nch, nw), lambda i: (i, 0, 0)), state, state,
                  pl.BlockSpec((None, 1, ns, nw), lambda i: (layer, i, 0, 0))],
        out_specs=pl.BlockSpec((1, nch, nw), lambda i: (i, 0, 0)),
        out_shape=jax.ShapeDtypeStruct((g, nch, nw), F32),
        compiler_params=_cparams(("parallel",)),
        name="s5_read",
    )(y1, prev_f, prev_b, vmat)


def _s5_matrices(lam_re, lam_im, log_step, b_re, b_im, c_re, c_im):
    t = S5_CHUNK
    step = jnp.exp(log_step)[..., None]
    den = lam_re * lam_re + lam_im * lam_im
    er = jnp.exp(lam_re * step)
    ar, ai = er * jnp.cos(lam_im * step), er * jnp.sin(lam_im * step)
    nr, ni = ar - 1.0, ai
    fr = (nr * lam_re + ni * lam_im) / den
    fi = (ni * lam_re - nr * lam_im) / den
    bre, bim = b_re[:, None], b_im[:, None]
    br = fr[..., None] * bre - fi[..., None] * bim
    bi = fr[..., None] * bim + fi[..., None] * bre
    kk = jnp.arange(t + 1, dtype=F32)[:, None, None, None, None]
    ek = jnp.exp(kk * (lam_re * step)[None])
    pr, pi = ek * jnp.cos(kk * (lam_im * step)[None]), ek * jnp.sin(kk * (lam_im * step)[None])
    abr = pr[..., None] * br[None] - pi[..., None] * bi[None]
    abi = pr[..., None] * bi[None] + pi[..., None] * br[None]
    ein = functools.partial(jnp.einsum, precision=HIGHEST)
    kern = ein('dsgop,kdsgpj->kdsgoj', c_re, abr[:t]) - ein('dsgop,kdsgpj->kdsgoj', c_im, abi[:t])
    kf, kb = kern[:, :, 0], kern[:, :, 1]
    lag = jnp.concatenate([kb[:0:-1], (kf[0] + kb[0])[None], kf[1:]], axis=0)
    nd = lag.shape[1]
    flat = lag.transpose(1, 2, 4, 0, 3).reshape(nd, S5_GROUPS, S5_GROUP, (2 * t - 1) * S5_GROUP).astype(BF16)
    m = jnp.stack([flat[..., (t - 1 - j) * S5_GROUP:(2 * t - 1 - j) * S5_GROUP] for j in range(t)], axis=2)
    m = m.reshape(nd, S5_GROUPS, t * S5_GROUP, t * S5_GROUP)
    wf = jnp.stack([abr[:t][::-1, :, 0], abi[:t][::-1, :, 0]], axis=0)
    wb = jnp.stack([abr[:t, :, 1], abi[:t, :, 1]], axis=0)
    w = jnp.concatenate([wf, wb], axis=0).astype(BF16)
    w = w.transpose(2, 3, 1, 5, 0, 4).reshape(w.shape[2], S5_GROUPS, t * S5_GROUP, 4 * S5_STATE)
    cr, ci = c_re[None], c_im[None]
    vre = cr * pr[:, :, :, :, None, :] - ci * pi[:, :, :, :, None, :]
    vim = -(cr * pi[:, :, :, :, None, :] + ci * pr[:, :, :, :, None, :])
    vf = jnp.stack([vre[1:, :, 0], vim[1:, :, 0]], axis=0)
    vb = jnp.stack([vre[:0:-1, :, 1], vim[:0:-1, :, 1]], axis=0)
    v = jnp.concatenate([vf, vb], axis=0).astype(BF16)
    v = v.transpose(2, 3, 0, 5, 1, 4).reshape(v.shape[2], S5_GROUPS, 4 * S5_STATE, t * S5_GROUP)
    atr, ati = pr[t], pi[t]
    row = lambda re_part, im_part: jnp.concatenate([re_part, im_part], axis=-1).reshape(nd, S5_GROUPS * LANES)
    coef = jnp.stack([row(atr[:, 0], atr[:, 0]), row(-ati[:, 0], ati[:, 0]),
                      row(atr[:, 1], atr[:, 1]), row(-ati[:, 1], ati[:, 1])], axis=1)
    return m, w, v, coef


def _mix_kernel(x_ref, mod_ref, a_ref, yf_ref, yb_ref, z_ref, gs_ref, y5_ref, u5_ref, d5_ref, wg_ref, bg_ref,
                f_ref, wo_ref, g2_ref, wr_ref, br_ref, o_ref, r_ref):
    mod = mod_ref[0]
    d = x_ref.shape[1]
    ssd = _rms((yf_ref[0] + yb_ref[0]) * _silu(z_ref[0]), SSD_INNER) * gs_ref[...]
    y5 = y5_ref[0] + d5_ref[...] * u5_ref[0]
    g5 = jax.nn.gelu(y5)
    s5 = g5 * _sigmoid(_dot(g5.astype(BF16), wg_ref[...]) + bg_ref[...])
    acc = _dot(a_ref[0].astype(BF16), wo_ref[0:256, :])
    acc += _dot(ssd.astype(BF16), wo_ref[256:512, :])
    acc += _dot(s5.astype(BF16), wo_ref[512:768, :])
    acc += _dot(f_ref[0].astype(BF16), wo_ref[768:1024, :])
    xn = x_ref[...] + mod[2:3] * acc
    h = _rms(xn, d) * g2_ref[...]
    h = h * (1.0 + mod[4:5]) + mod[3:4]
    route = _route(h, wr_ref, br_ref)
    o_ref[0, :, :d] = xn
    o_ref[0, :, d:] = route
    r_ref[0] = route


def _mix(x, layer, modtab, a, yf, yb, ssd_raw, gs, y5, u5, d5, wglu, bglu, f, wo, g2, wr, br):
    (b, l, _), d = a.shape, x.shape[1]
    nt = l // TOK_TILE
    tok = lambda c: pl.BlockSpec((1, TOK_TILE, c), lambda i, j: (i, j, 0))
    full = lambda arr: pl.BlockSpec(arr.shape, lambda i, j: (0,) * arr.ndim)
    return pl.pallas_call(
        _mix_kernel,
        grid=(b, nt),
        in_specs=[pl.BlockSpec((TOK_TILE, d), lambda i, j: (i * nt + j, 0)),
                  pl.BlockSpec((1, 8, d), lambda i, j: (2 * i + jnp.minimum(j, 1), 0, 0)),
                  tok(256), tok(256), tok(256), tok(256), full(gs), tok(256), tok(256), full(d5), full(wglu),
                  full(bglu), tok(256), pl.BlockSpec((None,) + wo.shape[1:], lambda i, j: (layer, 0, 0)),
                  full(g2), full(wr), full(br)],
        out_specs=[tok(d + LANES), tok(LANES)],
        out_shape=[jax.ShapeDtypeStruct((b, l, d + LANES), F32), jax.ShapeDtypeStruct((b, l, LANES), F32)],
        compiler_params=_cparams(("parallel", "arbitrary")),
        name="mix_out",
    )(x, modtab, a, yf, yb, ssd_raw, gs, y5, u5, d5, wglu, bglu, f, wo, g2, wr, br)


def _route(h, wr_ref, br_ref):
    lane = lax.broadcasted_iota(jnp.int32, (1, LANES), 1)
    lanef = lane.astype(F32)
    logits = _dot_3pass(h, wr_ref[...]) + br_ref[...]
    neg = -jnp.inf
    big = float(4 * LANES)
    first = lambda hit: jnp.min(jnp.where(hit, lanef, big), axis=-1, keepdims=True)
    glog = jnp.where(lane < MOE_GROUPS, logits, neg)
    gmax = jnp.max(glog, axis=-1, keepdims=True)
    gidx = first(glog == gmax)
    p_group = 1.0 / jnp.sum(jnp.exp(glog - gmax), axis=-1, keepdims=True)
    e0 = MOE_GROUPS + gidx * MOE_PER_GROUP
    elog = jnp.where((lanef >= e0) & (lanef < e0 + MOE_PER_GROUP), logits, neg)
    v1 = jnp.max(elog, axis=-1, keepdims=True)
    i1 = first(elog == v1)
    elog2 = jnp.where(lanef == i1, neg, elog)
    v2 = jnp.max(elog2, axis=-1, keepdims=True)
    i2 = first(elog2 == v2)
    e2 = jnp.exp(v2 - v1)
    w1 = p_group / (1.0 + e2)
    w2 = p_group * e2 / (1.0 + e2)
    return (jnp.where(lanef == i1 - e0, w1, 0.0) + jnp.where(lanef == i2 - e0, w2, 0.0)
            + jnp.where(lane == MOE_PER_GROUP, gidx, 0.0))


def _route_plan(gid, tm, nt):
    t = gid.shape[0]
    onehot = (gid[:, None] == jnp.arange(MOE_GROUPS, dtype=jnp.int32)[None]).astype(jnp.int32)
    csum = jnp.cumsum(onehot, axis=0)
    rank = jnp.sum((csum - onehot) * onehot, axis=1)
    padded = (csum[-1] + tm - 1) // tm * tm
    ends = jnp.cumsum(padded)
    slot = jnp.sum(onehot * (ends - padded)[None], axis=1) + rank
    src = jnp.full((nt * tm,), -1, jnp.int32).at[slot].set(jnp.arange(t, dtype=jnp.int32))
    pos = jnp.arange(nt * tm, dtype=jnp.int32)
    spare = t + (pos // tm % 2) * tm + pos % tm
    dst = jnp.concatenate([t + tm + pos[:tm], jnp.where(src >= 0, src, spare)])
    starts = jnp.arange(nt, dtype=jnp.int32) * tm
    tile_group = jnp.minimum(jnp.sum((starts[:, None] >= ends[None]).astype(jnp.int32), axis=1), MOE_GROUPS - 1)
    return src, dst, tile_group


def _moe_kernel(src_s, dst_s, tg_s, srcv_ref, xg_hbm, mod_ref, g2_ref, wg_ref, wu_ref, wd_ref, out_hbm,
                gbuf0, gbuf1, ybuf0, ybuf1, gsem, ssem, *, seq_len, n_ctx):
    del tg_s
    i = pl.program_id(0)
    nt = pl.num_programs(0)
    gbufs, ybufs = (gbuf0, gbuf1), (ybuf0, ybuf1)
    tm, d = ybuf0.shape
    nxt = jnp.minimum(i + 1, nt - 1)

    def gather_copy(tile, sl, r):
        t = jnp.maximum(src_s[tile * tm + r], 0)
        return pltpu.make_async_copy(xg_hbm.at[pl.ds(t, 1)], gbufs[sl].at[pl.ds(r, 1)], gsem.at[sl])

    def scatter_copy(tile, sl, r):
        t = dst_s[(tile + 1) * tm + r]
        return pltpu.make_async_copy(ybufs[sl].at[pl.ds(r, 1)], out_hbm.at[pl.ds(t, 1)], ssem.at[sl])

    def each_row(make_copy, tile, sl, wait):
        def body(r, c):
            cp = make_copy(tile, sl, r)
            cp.wait() if wait else cp.start()
            return c
        lax.fori_loop(0, tm, body, 0, unroll=8)

    @pl.when(i == 0)
    def _():
        each_row(gather_copy, 0, 0, False)
        n_tok = out_hbm.shape[0] - 2 * tm
        for sl in range(2):
            ybufs[sl][...] = jnp.zeros((tm, d), F32)
        fills = [pltpu.make_async_copy(ybufs[sl], out_hbm.at[pl.ds(n_tok + sl * tm, tm)], ssem.at[sl])
                 for sl in range(2)]
        for cp in fills:
            cp.start()
        for cp in fills:
            cp.wait()

    def modrow(tok, r):
        out = jnp.zeros((tm, d), F32)
        for sg in range(mod_ref.shape[0]):
            lo_t = (sg // 2) * seq_len + (0 if sg % 2 == 0 else n_ctx)
            hi_t = (sg // 2) * seq_len + (n_ctx if sg % 2 == 0 else seq_len)
            out = jnp.where((tok >= lo_t) & (tok < hi_t), mod_ref[sg, r:r + 1, :], out)
        return out

    def tile_step(sl):
        ot = 1 - sl
        each_row(gather_copy, i, sl, True)

        @pl.when(i >= 1)
        def _():
            each_row(scatter_copy, i - 2, sl, True)

        x = gbufs[sl][:, :d]
        gates = gbufs[sl][:, d:]
        tok = jnp.maximum(srcv_ref[...], 0)
        h = _rms(x, d) * g2_ref[...]
        h = (h * (1.0 + modrow(tok, 4)) + modrow(tok, 3)).astype(BF16)
        y = jnp.zeros((tm, d), F32)
        rows_per_expert = tm // MOE_PER_GROUP
        for e in range(MOE_PER_GROUP):
            for r in range(e * rows_per_expert, (e + 1) * rows_per_expert):
                gather_copy(nxt, ot, r).start()
                scatter_copy(i - 1, ot, r).start()
            hid = _silu(_dot(h, wg_ref[e])) * _dot(h, wu_ref[e]) * gates[:, e:e + 1]
            y += _dot(hid.astype(BF16), wd_ref[e])
        ybufs[sl][...] = x + modrow(tok, 5) * y

        @pl.when(i == nt - 1)
        def _():
            each_row(scatter_copy, i, sl, False)
            each_row(scatter_copy, i - 1, ot, True)
            each_row(scatter_copy, i, sl, True)
            each_row(gather_copy, nxt, ot, True)

    for sl in range(2):
        pl.when(i % 2 == sl)(functools.partial(tile_step, sl))


def _moe(xg, src, dst, tile_group, modtab, g2, wg, wu, wd, layer, seq_len, n_ctx):
    t, wx = xg.shape
    d = wx - LANES
    tm = MOE_TILE
    nt = src.shape[0] // tm
    full = lambda arr: pl.BlockSpec(arr.shape, lambda i, s, ds, g: (0,) * arr.ndim)
    wspec = lambda arr: pl.BlockSpec((None,) + arr.shape[1:], lambda i, s, ds, g: (layer * MOE_GROUPS + g[i], 0, 0, 0))
    grid_spec = pltpu.PrefetchScalarGridSpec(
        num_scalar_prefetch=3,
        grid=(nt,),
        in_specs=[pl.BlockSpec((tm, 1), lambda i, s, ds, g: (i, 0)),
                  pl.BlockSpec(memory_space=pl.ANY),
                  full(modtab), full(g2), wspec(wg), wspec(wu), wspec(wd)],
        out_specs=pl.BlockSpec(memory_space=pl.ANY),
        scratch_shapes=[pltpu.VMEM((tm, wx), F32), pltpu.VMEM((tm, wx), F32),
                        pltpu.VMEM((tm, d), F32), pltpu.VMEM((tm, d), F32),
                        pltpu.SemaphoreType.DMA((2,)), pltpu.SemaphoreType.DMA((2,))])
    return pl.pallas_call(
        functools.partial(_moe_kernel, seq_len=seq_len, n_ctx=n_ctx),
        grid_spec=grid_spec,
        out_shape=jax.ShapeDtypeStruct((t + 2 * tm, d), F32),
        compiler_params=_cparams(("arbitrary",)),
        name="moe",
    )(src, dst, tile_group, src.reshape(-1, 1), xg, modtab, g2, wg, wu, wd)


def _pad_in_proj(w_in):
    depth, d, _ = w_in.shape
    pieces = [(MLA_QK, LANES - MLA_QK)] * MLA_HEADS
    pieces += [(MLA_KV_RANK, MLA_NOPE), (MLA_ROPE, LANES - MLA_QK)]
    pieces += [(SSD_COLS, P_S5 - P_SSD - SSD_COLS), (S5_COLS, 0), (512, 0)]
    pieces += [(DIFF_V, LANES - DIFF_V)] * DIFF_HEADS
    cols, o = [], 0
    for width, gap in pieces:
        cols.append(w_in[:, :, o:o + width])
        o += width
        if gap:
            cols.append(jnp.zeros((depth, d, gap), w_in.dtype))
    out = jnp.concatenate(cols, axis=-1)
    assert o == w_in.shape[2] and out.shape[2] == P_COLS
    return out


def _rope_tables(seq, n_ctx):
    pos = jnp.arange(seq, dtype=jnp.int32)
    rows = (pos // GRID_W).astype(F32)
    cols = (pos % GRID_W).astype(F32)
    half = MLA_ROPE // 2
    inv = ROPE_BASE ** (-jnp.arange(0, half, 2, dtype=F32) / half)
    r = jnp.arange(MLA_ROPE)
    freq = inv[r % (half // 2)]
    ang = jnp.where((r < half)[None, :], rows[:, None], cols[:, None]) * freq[None, :]
    cos, sin = jnp.cos(ang), jnp.sin(ang)
    first = ((r % half) < half // 2)[None, :]
    up = jnp.where(first, -sin, 0.0)
    dn = jnp.where(first, 0.0, sin)
    t32 = jnp.stack([cos, up, dn])
    ident = jnp.stack([jnp.ones((n_ctx, MLA_ROPE), F32), jnp.zeros((n_ctx, MLA_ROPE), F32),
                       jnp.zeros((n_ctx, MLA_ROPE), F32)])
    t32 = jnp.concatenate([ident, t32], axis=1)
    n = t32.shape[1]
    base = jnp.stack([jnp.ones((n, LANES), F32), jnp.zeros((n, LANES), F32), jnp.zeros((n, LANES), F32)])
    rope_m = base.at[:, :, MLA_NOPE:MLA_QK].set(t32)
    rope_d = jnp.tile(t32, (1, 1, LANES // MLA_ROPE))
    return rope_m, rope_d


def _pad_lanes(v, n):
    return jnp.zeros((1, n), F32).at[0, :v.shape[0]].set(v)


def kernel(x, c, ctx, c_ctx, w_ada, b_ada, norm1, norm2, w_in, w_out, mla_kv_norm, mla_w_uk, mla_w_uv, mla_q_norm, mla_k_norm, ssd_conv_w, ssd_conv_b, ssd_a_log, ssd_dt_bias, ssd_d, ssd_norm, s5_lam_re, s5_lam_im, s5_log_step, s5_b_re, s5_b_im, s5_c_re, s5_c_im, s5_d, s5_w_glu, s5_b_glu, diff_q_norm, diff_k_norm, diff_lq1, diff_lk1, diff_lq2, diff_lk2, diff_subln, moe_w_group, moe_b_group, moe_w_expert, moe_b_expert, moe_w_gate, moe_w_up, moe_w_down):
    bsz, seq, d = x.shape
    n_ctx = ctx.shape[1]
    depth = w_in.shape[0]
    assert n_ctx == TOK_TILE and seq % FLASH_KV_CHUNK == 0 and (n_ctx + seq) % FLASH_Q_TILE == 0 and d == 1024
    assert FLASH_Q_TILE >= n_ctx
    l = n_ctx + seq
    assert (bsz * l) % MOE_TILE == 0 and bsz * l >= 2 * MOE_TILE and l % SSD_CHUNK == 0 and n_ctx % SSD_CHUNK == 0

    cvec = jnp.zeros((8, d), F32).at[:bsz].set(c).at[bsz].set(c_ctx)
    ada = _ada(cvec, w_ada, b_ada).reshape(depth, 8, N_MOD, d)
    mod_lat = ada[:, :bsz]
    mod_ctx = jnp.broadcast_to(ada[:, bsz][:, None], mod_lat.shape)
    modtab = jnp.stack([mod_ctx, mod_lat], axis=2)
    modtab = jnp.pad(modtab, ((0, 0), (0, 0), (0, 0), (0, 8 - N_MOD), (0, 0))).reshape(depth, bsz * 2, 8, d)

    w_in_p = _pad_in_proj(w_in).astype(BF16)
    w_out_b = w_out.astype(BF16)
    rope_m, rope_d = _rope_tables(seq, n_ctx)
    wuk = jnp.zeros((depth, MLA_KV_RANK, MLA_HEADS * LANES), F32)
    for hd in range(MLA_HEADS):
        wuk = wuk.at[:, :, hd * LANES:hd * LANES + MLA_NOPE].set(mla_w_uk[:, :, hd * MLA_NOPE:(hd + 1) * MLA_NOPE])
    wuk = wuk.astype(BF16)
    wuv = jnp.zeros((depth, MLA_KV_RANK, MLA_HEADS * LANES), F32)
    for hd in range(MLA_HEADS):
        wuv = wuv.at[:, :, hd * LANES:hd * LANES + MLA_V].set(mla_w_uv[:, :, hd * MLA_V:(hd + 1) * MLA_V])
    wuv = wuv.astype(BF16)
    amax = lambda g: jnp.max(jnp.abs(g.astype(F32)), axis=-1)
    bounded_m = (math.sqrt(MLA_QK) * amax(mla_q_norm) * amax(mla_k_norm) <= SCORE_BOUND).astype(F32)
    bounded_d = (math.sqrt(DIFF_HEAD) * amax(diff_q_norm) * amax(diff_k_norm) <= SCORE_BOUND).astype(F32)
    lane = jnp.arange(LANES)
    seg = (lane[:, None] // DIFF_HEAD == lane[None, :] // DIFF_HEAD).astype(F32)
    a_vec = -jnp.exp(ssd_a_log.astype(F32)).reshape(depth, 2 * SSD_HEADS)
    s5_m, s5_w, s5_v, s5_coef = _s5_matrices(
        s5_lam_re.astype(F32), s5_lam_im.astype(F32), s5_log_step.astype(F32),
        s5_b_re.astype(F32), s5_b_im.astype(F32), s5_c_re.astype(F32), s5_c_im.astype(F32))
    w_router = jnp.concatenate([moe_w_group, moe_w_expert.transpose(0, 2, 1, 3).reshape(depth, d, N_EXPERTS)], axis=-1)
    w_router = jnp.pad(w_router, ((0, 0), (0, 0), (0, LANES - w_router.shape[-1])))
    b_router = jnp.concatenate([moe_b_group, moe_b_expert.reshape(depth, N_EXPERTS)], axis=-1)
    b_router = jnp.pad(b_router, ((0, 0), (0, LANES - b_router.shape[-1])))
    by_group = lambda w: w.astype(BF16).reshape((depth * MOE_GROUPS, MOE_PER_GROUP) + w.shape[2:])
    wg_b, wu_b, wd_b = by_group(moe_w_gate), by_group(moe_w_up), by_group(moe_w_down)
    lam_all = (jnp.exp(jnp.sum(diff_lq1.astype(F32) * diff_lk1.astype(F32), axis=-1))
               - jnp.exp(jnp.sum(diff_lq2.astype(F32) * diff_lk2.astype(F32), axis=-1)))

    n5 = l // S5_CHUNK
    xs = jnp.concatenate([ctx, x], axis=1).reshape(bsz * l, d)
    for i in range(depth):
        lam_init = 0.8 - 0.6 * math.exp(-0.3 * i)
        qm, km, vm, qd, kd, vd, ssd_raw, u5 = _front(
            xs, bsz, i, modtab[i], norm1[i][None], w_in_p, rope_m, rope_d,
            _pad_lanes(mla_q_norm[i], LANES), _pad_lanes(mla_k_norm[i], LANES), mla_kv_norm[i][None],
            wuk[i], wuv[i], jnp.tile(diff_q_norm[i], 4)[None], jnp.tile(diff_k_norm[i], 4)[None], seg)
        a = _flash(jnp.stack([bounded_m[i], jnp.zeros((), F32)]), qm, km, vm, jnp.ones((1, LANES), F32),
                   n_sub=1, n_ctx=n_ctx, post=1.0)
        f = _flash(jnp.stack([bounded_d[i], lam_all[i] + lam_init]), qd, kd, vd, jnp.tile(diff_subln[i], 2)[None],
                   n_sub=2, n_ctx=n_ctx, post=1.0 - lam_init)
        yf, yb = _ssd(ssd_raw, ssd_conv_w[i], ssd_conv_b[i][None],
                      _pad_lanes(ssd_dt_bias[i].reshape(-1), LANES), _pad_lanes(a_vec[i], LANES),
                      jnp.repeat(ssd_d[i], SSD_HEAD_DIM)[None], n_ctx)
        u5g = u5.reshape(bsz, n5, S5_CHUNK, S5_GROUPS, S5_GROUP).transpose(3, 0, 1, 2, 4)
        u5g = u5g.reshape(S5_GROUPS, bsz * n5, S5_CHUNK * S5_GROUP)
        y1, end_f, end_b = _s5_local(u5g, s5_m, s5_w, i)
        prev_f, prev_b = _s5_scan(end_f, end_b, s5_coef[i], bsz, n_ctx // S5_CHUNK)
        y5 = _s5_read(y1, prev_f, prev_b, s5_v, i)
        y5 = y5.reshape(S5_GROUPS, bsz, n5, S5_CHUNK, S5_GROUP).transpose(1, 2, 3, 0, 4).reshape(bsz, l, S5_COLS)
        xg, ginfo = _mix(xs, i, modtab[i], a, yf, yb, ssd_raw, ssd_norm[i][None], y5, u5, s5_d[i][None],
                  s5_w_glu[i].astype(BF16), s5_b_glu[i][None], f, w_out_b,
                  norm2[i][None], w_router[i], b_router[i][None])
        xg = xg.reshape(bsz * l, d + LANES)
        gid = ginfo.reshape(bsz * l, LANES)[:, MOE_PER_GROUP].astype(jnp.int32)
        src, dst, tile_group = _route_plan(gid, MOE_TILE, bsz * l // MOE_TILE + MOE_GROUPS)
        xs = _moe(xg, src, dst, tile_group, modtab[i], norm2[i][None], wg_b, wu_b, wd_b, i, l, n_ctx)
    return xs[:bsz * l].reshape(bsz, l, d)[:, n_ctx:]
```

```python
import functools
import math

import jax
import jax.numpy as jnp
from jax import lax
from jax.experimental import pallas as pl
from jax.experimental.pallas import tpu as pltpu

F32 = jnp.float32
BF16 = jnp.bfloat16
HIGHEST = lax.Precision.HIGHEST

LANES = 128
EPS = 1e-6
ROPE_BASE = 10000.0
GRID_W = 64
N_MOD = 6

GROUP_WIDTH = 256
MLA_HEADS, MLA_NOPE, MLA_ROPE, MLA_V, MLA_KV_RANK = 4, 64, 32, 64, 128
MLA_QK = MLA_NOPE + MLA_ROPE
SSD_HEADS, SSD_HEAD_DIM, SSD_GROUPS, SSD_STATE, SSD_INNER = 4, 64, 2, 64, 256
SSD_GN = SSD_GROUPS * SSD_STATE
SSD_XBC = SSD_INNER + 2 * SSD_GN
S5_GROUPS, S5_GROUP, S5_STATE = 16, 16, 64
DIFF_HEADS, DIFF_HEAD, DIFF_V = 4, 32, 64
MOE_GROUPS, MOE_PER_GROUP, N_EXPERTS, EXPERT_HIDDEN = 4, 8, 32, 256

MLA_COLS = MLA_HEADS * MLA_QK + MLA_KV_RANK + MLA_ROPE
SSD_COLS = SSD_INNER + SSD_XBC + 2 * SSD_HEADS
S5_COLS = 256
DIFF_COLS = 768

P_MLA_Q = 0
P_MLA_CKV = 512
P_MLA_KR = 640
P_SSD = 768
P_S5 = 1664
P_DIFF = 1920
P_COLS = 2944
V_ONE = 64
SCORE_BOUND = 40.0
LOG2E = 1.4426950408889634

TOK_TILE = 256
SSD_CHUNK = 256
S5_CHUNK = 32
FLASH_KV_CHUNK = 1024
FLASH_Q_TILE = 1408
MOE_TILE = 512
VMEM_LIMIT = 56 * 1024 * 1024


def _cparams(sem):
    return pltpu.CompilerParams(dimension_semantics=sem, vmem_limit_bytes=VMEM_LIMIT)


def _sigmoid(x):
    return 1.0 / (1.0 + jnp.exp(-x))


def _silu(x):
    return x * _sigmoid(x)


def _rms(x, n):
    return x * lax.rsqrt(jnp.sum(x * x, axis=-1, keepdims=True) * (1.0 / n) + EPS)


def _dot(a, b):
    return jnp.dot(a, b, preferred_element_type=F32)


def _dot_hi(a, b):
    return jnp.dot(a, b, preferred_element_type=F32, precision=HIGHEST)


def _dot_3pass(a, b):
    a_hi, b_hi = a.astype(BF16), b.astype(BF16)
    a_lo = (a - a_hi.astype(F32)).astype(BF16)
    b_lo = (b - b_hi.astype(F32)).astype(BF16)
    return _dot(a_hi, b_hi) + (_dot(a_hi, b_lo) + _dot(a_lo, b_hi))


def _dot_nt(a, b):
    return lax.dot_general(a, b, (((1,), (1,)), ((), ())), preferred_element_type=F32)


def _dot_tn(a, b):
    return lax.dot_general(a, b, (((0,), (0,)), ((), ())), preferred_element_type=F32)


def _dot_tn_hi(a, b):
    return lax.dot_general(a, b, (((0,), (0,)), ((), ())), preferred_element_type=F32, precision=HIGHEST)


def _ada_kernel(c_ref, w_ref, b_ref, o_ref):
    o_ref[0] = _dot_hi(_silu(c_ref[...]), w_ref[0]) + b_ref[0]


def _ada(cvec, w_ada, b_ada):
    depth, d, nd = w_ada.shape
    blk = 1024
    return pl.pallas_call(
        _ada_kernel,
        grid=(depth, nd // blk),
        in_specs=[pl.BlockSpec((8, d), lambda i, j: (0, 0)),
                  pl.BlockSpec((1, d, blk), lambda i, j: (i, 0, j)),
                  pl.BlockSpec((1, 1, blk), lambda i, j: (i, 0, j))],
        out_specs=pl.BlockSpec((1, 8, blk), lambda i, j: (i, 0, j)),
        out_shape=jax.ShapeDtypeStruct((depth, 8, nd), F32),
        compiler_params=_cparams(("arbitrary", "arbitrary")),
        name="ada",
    )(cvec, w_ada, b_ada.reshape(depth, 1, nd))


def _rope(x, cos, sin_up, sin_dn):
    n = x.shape[-1]
    return x * cos + pltpu.roll(x, n - 8, axis=1) * sin_up + pltpu.roll(x, 8, axis=1) * sin_dn


def _front_kernel(x_ref, mod_ref, g1_ref, w_ref, rm_ref, rd_ref, gq_ref, gk_ref, gkv_ref, wuk_ref, wuv_ref,
                  gdq_ref, gdk_ref, seg_ref, one_ref,
                  qm_ref, km_ref, vm_ref, qd_ref, kd_ref, vd_ref, ssd_ref, s5_ref):
    x = x_ref[...]
    mod = mod_ref[0]
    h = _rms(x, x.shape[-1]) * g1_ref[...]
    h = h * (1.0 + mod[1:2]) + mod[0:1]
    p = _dot(h.astype(BF16), w_ref[...])

    ssd_ref[0] = p[:, P_SSD:P_S5]
    s5_ref[0] = p[:, P_S5:P_DIFF]

    cos_m, up_m, dn_m = rm_ref[0], rm_ref[1], rm_ref[2]
    ckv = _rms(p[:, P_MLA_CKV:P_MLA_KR], MLA_KV_RANK) * gkv_ref[...]
    ckv = ckv.astype(BF16)
    k_nope = _dot(ckv, wuk_ref[...])
    vm_ref[0] = (_dot(ckv, wuv_ref[...]) + one_ref[...]).astype(BF16)
    k_rope = p[:, P_MLA_KR:P_SSD]
    q_scale = MLA_QK ** -0.5 * LOG2E
    for hd in range(MLA_HEADS):
        sl = slice(hd * LANES, (hd + 1) * LANES)
        q = _rms(p[:, sl], MLA_QK) * gq_ref[...]
        qm_ref[0, :, sl] = (_rope(q, cos_m, up_m, dn_m) * q_scale).astype(BF16)
        k = _rms(k_nope[:, sl] + k_rope, MLA_QK) * gk_ref[...]
        km_ref[0, :, sl] = _rope(k, cos_m, up_m, dn_m).astype(BF16)

    cos_d, up_d, dn_d = rd_ref[0], rd_ref[1], rd_ref[2]
    d_scale = DIFF_HEAD ** -0.5 * LOG2E
    for blk, (g_ref, o_ref, scale) in enumerate(((gdq_ref, qd_ref, d_scale), (gdk_ref, kd_ref, 1.0))):
        for half in range(2):
            sl = slice(P_DIFF + blk * 256 + half * LANES, P_DIFF + blk * 256 + (half + 1) * LANES)
            t = p[:, sl]
            ss = _dot_hi(t * t, seg_ref[...])
            t = t * lax.rsqrt(ss * (1.0 / DIFF_HEAD) + EPS) * g_ref[...]
            o_ref[0, :, half * LANES:(half + 1) * LANES] = (_rope(t, cos_d, up_d, dn_d) * scale).astype(BF16)
    vd_ref[0] = (p[:, P_DIFF + 512:P_COLS] + one_ref[...]).astype(BF16)


def _front(x, b, layer, modtab, g1, w_in_p, rope_m, rope_d, gq, gk, gkv, wuk, wuv, gdq, gdk, seg):
    l, d = rope_m.shape[1], x.shape[1]
    nt = l // TOK_TILE
    tok = lambda c: pl.BlockSpec((1, TOK_TILE, c), lambda i, j: (i, j, 0))
    full = lambda a: pl.BlockSpec(a.shape, lambda i, j: (0,) * a.ndim)
    rope_spec = pl.BlockSpec((3, TOK_TILE, LANES), lambda i, j: (0, j, 0))
    one = (jnp.arange(4 * LANES) % LANES == V_ONE).astype(F32)[None]
    outs = [(512, BF16), (512, BF16), (512, BF16), (256, BF16), (256, BF16), (512, BF16),
            (P_S5 - P_SSD, F32), (256, F32)]
    return pl.pallas_call(
        _front_kernel,
        grid=(b, nt),
        in_specs=[pl.BlockSpec((TOK_TILE, d), lambda i, j: (i * nt + j, 0)),
                  pl.BlockSpec((1, 8, d), lambda i, j: (2 * i + jnp.minimum(j, 1), 0, 0)),
                  full(g1), pl.BlockSpec((None,) + w_in_p.shape[1:], lambda i, j: (layer, 0, 0)), rope_spec, rope_spec,
                  full(gq), full(gk), full(gkv), full(wuk), full(wuv), full(gdq), full(gdk), full(seg), full(one)],
        out_specs=[tok(c) for c, _ in outs],
        out_shape=[jax.ShapeDtypeStruct((b, l, c), dt) for c, dt in outs],
        compiler_params=_cparams(("parallel", "arbitrary")),
        name="front",
    )(x, modtab, g1, w_in_p, rope_m, rope_d, gq, gk, gkv, wuk, wuv, gdq, gdk, seg, one)


def _flash_kernel(scal_ref, q_ref, k_ref, v_ref, sub_ref, o_ref, acc_scr, m_scr, *, n_sub, n_ctx, ck, post):
    qi = pl.program_id(2)
    lk = k_ref.shape[1]
    lane = lax.broadcasted_iota(jnp.int32, (1, LANES), 1)
    lo = lane < V_ONE
    q = q_ref[0]
    maps = [(a, c) for a in range(2) for c in range(n_sub)]

    def q_of(a, c):
        if n_sub == 1:
            return q[:, a * LANES:(a + 1) * LANES]
        s = (a * n_sub + c) * DIFF_HEAD
        return jnp.where((lane >= s) & (lane < s + DIFF_HEAD), q, jnp.zeros_like(q))

    qs = [q_of(a, c) for a, c in maps]

    def scores(mi, start, size):
        a = maps[mi][0]
        kk = k_ref[0, pl.ds(start, size), a * LANES:(a + 1) * LANES] if n_sub == 1 else k_ref[0, pl.ds(start, size), :]
        return _dot_nt(qs[mi], kk)

    def values(mi, start, size):
        a = maps[mi][0]
        return v_ref[0, pl.ds(start, size), a * LANES:(a + 1) * LANES]

    row = lax.broadcasted_iota(jnp.int32, (q.shape[0], 1), 0)
    sees_latent = (qi > 0) | (row >= n_ctx)

    def step_bounded(start, size, latent_keys):
        for mi in range(len(maps)):
            p = jnp.exp2(scores(mi, start, size))
            if latent_keys:
                p = jnp.where(sees_latent, p, 0.0)
            acc_scr[mi] += _dot(p.astype(BF16), values(mi, start, size))

    def step_online(start, size, latent_keys):
        for mi in range(len(maps)):
            s = scores(mi, start, size)
            if latent_keys:
                s = jnp.where(sees_latent, s, -jnp.inf)
            m_prev = m_scr[mi]
            m_new = jnp.maximum(m_prev, jnp.max(s, axis=-1, keepdims=True))
            p = jnp.exp2(s - m_new).astype(BF16)
            acc_scr[mi] = acc_scr[mi] * jnp.exp2(m_prev - m_new) + _dot(p, values(mi, start, size))
            m_scr[mi] = m_new

    def sweep(step):
        step(0, n_ctx, False)

        def body(i, carry):
            step(pl.multiple_of(n_ctx + i * ck, n_ctx), ck, True)
            return carry
        lax.fori_loop(0, (lk - n_ctx) // ck, body, 0)

    acc_scr[...] = jnp.zeros(acc_scr.shape, F32)
    bounded = scal_ref[0] != 0.0

    @pl.when(bounded)
    def _():
        sweep(step_bounded)

    @pl.when(jnp.logical_not(bounded))
    def _():
        m_scr[...] = jnp.full(m_scr.shape, -jnp.inf, F32)
        sweep(step_online)

    def normalised(mi):
        acc = acc_scr[mi]
        return acc * (1.0 / acc[:, V_ONE:V_ONE + 1])

    heads = []
    for a in range(2):
        if n_sub == 1:
            o = normalised(a)
        else:
            o = normalised(a * n_sub) - scal_ref[1] * normalised(a * n_sub + 1)
            ss = jnp.sum(jnp.where(lo, o * o, 0.0), axis=-1, keepdims=True)
            o = o * lax.rsqrt(ss * (1.0 / V_ONE) + EPS)
        heads.append(o)
    pair = jnp.where(lo, heads[0], pltpu.roll(heads[1], V_ONE, axis=1))
    o_ref[0] = pair if n_sub == 1 else pair * sub_ref[...] * post


def _flash(scal, q, k, v, subln, *, n_sub, n_ctx, post):
    b, l, qw = q.shape
    wq = qw // 2
    tq = FLASH_Q_TILE
    nq = l // tq
    kern = functools.partial(_flash_kernel, n_sub=n_sub, n_ctx=n_ctx, ck=FLASH_KV_CHUNK, post=post)
    return pl.pallas_call(
        kern,
        grid=(b, 2, nq),
        in_specs=[pl.BlockSpec(memory_space=pltpu.SMEM),
                  pl.BlockSpec((1, tq, wq), lambda i, h, j: (i, j, h)),
                  pl.BlockSpec((1, l, wq), lambda i, h, j: (i, 0, h)),
                  pl.BlockSpec((1, l, 2 * LANES), lambda i, h, j: (i, 0, h)),
                  pl.BlockSpec((1, LANES), lambda i, h, j: (0, 0))],
        out_specs=pl.BlockSpec((1, tq, LANES), lambda i, h, j: (i, j, h)),
        out_shape=jax.ShapeDtypeStruct((b, l, 2 * LANES), F32),
        scratch_shapes=[pltpu.VMEM((2 * n_sub, tq, LANES), F32),
                        pltpu.VMEM((2 * n_sub, tq, 1), F32)],
        compiler_params=_cparams(("parallel", "parallel", "arbitrary")),
        name="flash_diff" if n_sub == 2 else "flash_mla",
    )(scal, q, k, v, subln)


def _ssd_dir(d, first, xbc_ref, prev_ref, next_ref, edge_lo, edge_hi, cw_ref, cb_ref, dtb_ref, a_ref, dsk_ref,
             y_ref, st_scr):
    t = xbc_ref.shape[1]
    raw = xbc_ref[0]
    x = raw[:, SSD_INNER:SSD_INNER + SSD_XBC]
    row = lax.broadcasted_iota(jnp.int32, (t, 1), 0)
    prev_row = jnp.where(edge_lo, 0.0, prev_ref[0, 7:8, SSD_INNER:SSD_INNER + SSD_XBC])
    next_row = jnp.where(edge_hi, 0.0, next_ref[0, 0:1, SSD_INNER:SSD_INNER + SSD_XBC])
    xm1 = jnp.where(row == 0, prev_row, pltpu.roll(x, 1, axis=0))
    xp1 = jnp.where(row == t - 1, next_row, pltpu.roll(x, t - 1, axis=0))
    xc = _silu(cw_ref[0:1] * xm1 + cw_ref[1:2] * x + cw_ref[2:3] * xp1 + cb_ref[...])
    xs = xc[:, :SSD_INNER]
    bm = xc[:, SSD_INNER:SSD_INNER + SSD_GN]
    cm = xc[:, SSD_INNER + SSD_GN:]

    z = raw[:, SSD_INNER + SSD_XBC:] + dtb_ref[...]
    dt = jnp.maximum(z, 0.0) + jnp.log1p(jnp.exp(-jnp.abs(z)))
    da = dt * a_ref[...]

    ri = lax.broadcasted_iota(jnp.int32, (t, t), 0)
    ci = lax.broadcasted_iota(jnp.int32, (t, t), 1)
    mask = (ci <= ri) if d == 0 else (ci >= ri)
    mask_t = (ri <= ci) if d == 0 else (ri >= ci)
    cum = _dot_hi(mask.astype(F32), da)
    cum_t = _dot_tn_hi(da, mask_t.astype(F32))
    tot = jnp.sum(da, axis=0, keepdims=True)

    lane = lax.broadcasted_iota(jnp.int32, (1, LANES), 1)
    lo = lane < (LANES // 2)

    @pl.when(first)
    def _():
        st_scr[d] = jnp.zeros(st_scr.shape[1:], F32)

    ys = []
    for g in range(SSD_GROUPS):
        gmask = lo if g == 0 else jnp.logical_not(lo)
        cg = jnp.where(gmask, cm, 0.0)
        gram = _dot_nt(cg.astype(BF16), bm.astype(BF16))
        xpair = xs[:, g * LANES:(g + 1) * LANES]
        yh = []
        for hh in range(2):
            hd = g * 2 + hh
            col = d * SSD_HEADS + hd
            cum_c = cum[:, col:col + 1]
            cum_r = cum_t[col:col + 1, :]
            dt_c = dt[:, col:col + 1]
            tot_c = tot[:, col:col + 1]
            decay = jnp.exp(jnp.where(mask, cum_c - cum_r, -jnp.inf))
            xdt = (xpair * dt_c).astype(BF16)
            st = st_scr[d, hd]
            y = (_dot((gram * decay).astype(BF16), xdt)
                 + _dot((cg * jnp.exp(cum_c)).astype(BF16), st.astype(BF16)))
            st_scr[d, hd] = st * jnp.exp(tot_c) + _dot_tn((bm * jnp.exp(tot_c - cum_c)).astype(BF16), xdt)
            yh.append(y)
        ys.append(jnp.where(lo, yh[0], yh[1]))
    y = jnp.concatenate(ys, axis=1)
    if d == 0:
        y = y + xs * dsk_ref[...]
    y_ref[0] = y


def _ssd_kernel(xf_ref, pf_ref, nf_ref, xb_ref, pb_ref, nb_ref, cw_ref, cb_ref, dtb_ref, a_ref, dsk_ref,
                yf_ref, yb_ref, st_scr, *, n_ctx_chunks):
    s = pl.program_id(1)
    nc = pl.num_programs(1)
    cf = s
    cb = jnp.where(s < n_ctx_chunks, n_ctx_chunks - 1 - s, nc - 1 + n_ctx_chunks - s)
    seg_lo = lambda c: (c == 0) | (c == n_ctx_chunks)
    seg_hi = lambda c: (c == n_ctx_chunks - 1) | (c == nc - 1)
    _ssd_dir(0, s == 0, xf_ref, pf_ref, nf_ref, seg_lo(cf), seg_hi(cf), cw_ref, cb_ref, dtb_ref, a_ref, dsk_ref,
             yf_ref, st_scr)
    _ssd_dir(1, s == 0, xb_ref, pb_ref, nb_ref, seg_lo(cb), seg_hi(cb), cw_ref, cb_ref, dtb_ref, a_ref, dsk_ref,
             yb_ref, st_scr)


def _ssd(raw, conv_w, conv_b, dt_bias, a_vec, dskip, n_ctx):
    b, l, w = raw.shape
    t = SSD_CHUNK
    nc = l // t
    ncc = n_ctx // t
    r8 = t // 8
    nb8 = l // 8
    fwd = lambda s: s
    bwd = lambda s: jnp.where(s < ncc, ncc - 1 - s, nc - 1 + ncc - s)
    chunk = lambda f: pl.BlockSpec((1, t, w), lambda i, s: (i, f(s), 0))
    prev = lambda f: pl.BlockSpec((1, 8, w), lambda i, s: (i, jnp.maximum(f(s) * r8 - 1, 0), 0))
    nxt = lambda f: pl.BlockSpec((1, 8, w), lambda i, s: (i, jnp.minimum((f(s) + 1) * r8, nb8 - 1), 0))
    full = lambda a: pl.BlockSpec(a.shape, lambda i, s: (0,) * a.ndim)
    out = lambda f: pl.BlockSpec((1, t, SSD_INNER), lambda i, s: (i, f(s), 0))
    return pl.pallas_call(
        functools.partial(_ssd_kernel, n_ctx_chunks=ncc),
        grid=(b, nc),
        in_specs=[chunk(fwd), prev(fwd), nxt(fwd), chunk(bwd), prev(bwd), nxt(bwd),
                  full(conv_w), full(conv_b), full(dt_bias), full(a_vec), full(dskip)],
        out_specs=[out(fwd), out(bwd)],
        out_shape=[jax.ShapeDtypeStruct((b, l, SSD_INNER), F32)] * 2,
        scratch_shapes=[pltpu.VMEM((2, SSD_HEADS, LANES, LANES), F32)],
        compiler_params=_cparams(("parallel", "arbitrary")),
        name="ssd",
    )(raw, raw, raw, raw, raw, raw, conv_w, conv_b, dt_bias, a_vec, dskip)


def _s5_local_kernel(u_ref, m_ref, w_ref, y_ref, sf_ref, sb_ref):
    u = u_ref[0].astype(BF16)
    y_ref[0] = _dot(u, m_ref[0])
    s = _dot(u, w_ref[0])
    sf_ref[...] = s[:, :LANES]
    sb_ref[...] = s[:, LANES:]


def _s5_local(u, m, w, layer):
    g, nch, wd = u.shape
    nw = wd
    ns = w.shape[3]
    state = jax.ShapeDtypeStruct((nch, g * LANES), F32)
    return pl.pallas_call(
        _s5_local_kernel,
        grid=(g,),
        in_specs=[pl.BlockSpec((1, nch, wd), lambda i: (i, 0, 0)),
                  pl.BlockSpec((None, 1, wd, nw), lambda i: (layer, i, 0, 0)),
                  pl.BlockSpec((None, 1, wd, ns), lambda i: (layer, i, 0, 0))],
        out_specs=[pl.BlockSpec((1, nch, nw), lambda i: (i, 0, 0)),
                   pl.BlockSpec((nch, LANES), lambda i: (0, i)),
                   pl.BlockSpec((nch, LANES), lambda i: (0, i))],
        out_shape=[jax.ShapeDtypeStruct((g, nch, nw), F32), state, state],
        compiler_params=_cparams(("parallel",)),
        name="s5_local",
    )(u, m, w)


def _s5_scan_kernel(sf_ref, sb_ref, co_ref, of_ref, ob_ref, *, n_ctx_chunks):
    n, w = sf_ref.shape
    lane = lax.broadcasted_iota(jnp.int32, (1, w), 1)
    re_half = lane % LANES < LANES // 2

    def swap(v):
        return jnp.where(re_half, pltpu.roll(v, w - LANES // 2, axis=1), pltpu.roll(v, LANES // 2, axis=1))

    def body(i, st):
        st_f, st_b = st
        jf = i
        jb = jnp.where(i < n_ctx_chunks, n_ctx_chunks - 1 - i, n - 1 + n_ctx_chunks - i)
        of_ref[pl.ds(jf, 1), :] = st_f
        ob_ref[pl.ds(jb, 1), :] = st_b
        st_f = st_f * co_ref[0:1] + swap(st_f) * co_ref[1:2] + sf_ref[pl.ds(jf, 1), :]
        st_b = st_b * co_ref[2:3] + swap(st_b) * co_ref[3:4] + sb_ref[pl.ds(jb, 1), :]
        return st_f, st_b

    zero = jnp.zeros((1, w), F32)
    lax.fori_loop(0, n, body, (zero, zero))


def _s5_scan(sf, sb, coef, n_batch, n_ctx_chunks):
    nch, w = sf.shape
    n = nch // n_batch
    blk = pl.BlockSpec((n, w), lambda i: (i, 0))
    return pl.pallas_call(
        functools.partial(_s5_scan_kernel, n_ctx_chunks=n_ctx_chunks),
        grid=(n_batch,),
        in_specs=[blk, blk, pl.BlockSpec(coef.shape, lambda i: (0, 0))],
        out_specs=[blk, blk],
        out_shape=[jax.ShapeDtypeStruct((nch, w), F32)] * 2,
        compiler_params=_cparams(("parallel",)),
        name="s5_scan",
    )(sf, sb, coef)


def _s5_read_kernel(y1_ref, pf_ref, pb_ref, v_ref, y_ref):
    y_ref[0] = (y1_ref[0] + _dot(pf_ref[...].astype(BF16), v_ref[0, :LANES, :])
                + _dot(pb_ref[...].astype(BF16), v_ref[0, LANES:, :]))


def _s5_read(y1, prev_f, prev_b, vmat, layer):
    g, nch, nw = y1.shape
    ns = vmat.shape[2]
    state = pl.BlockSpec((nch, LANES), lambda i: (0, i))
    return pl.pallas_call(
        _s5_read_kernel,
        grid=(g,),
        in_specs=[pl.BlockSpec((1, nch, nw), lambda i: (i, 0, 0)), state, state,
                  pl.BlockSpec((None, 1, ns, nw), lambda i: (layer, i, 0, 0))],
        out_specs=pl.BlockSpec((1, nch, nw), lambda i: (i, 0, 0)),
        out_shape=jax.ShapeDtypeStruct((g, nch, nw), F32),
        compiler_params=_cparams(("parallel",)),
        name="s5_read",
    )(y1, prev_f, prev_b, vmat)


def _s5_matrices(lam_re, lam_im, log_step, b_re, b_im, c_re, c_im):
    t = S5_CHUNK
    step = jnp.exp(log_step)[..., None]
    den = lam_re * lam_re + lam_im * lam_im
    er = jnp.exp(lam_re * step)
    ar, ai = er * jnp.cos(lam_im * step), er * jnp.sin(lam_im * step)
    nr, ni = ar - 1.0, ai
    fr = (nr * lam_re + ni * lam_im) / den
    fi = (ni * lam_re - nr * lam_im) / den
    bre, bim = b_re[:, None], b_im[:, None]
    br = fr[..., None] * bre - fi[..., None] * bim
    bi = fr[..., None] * bim + fi[..., None] * bre
    kk = jnp.arange(t + 1, dtype=F32)[:, None, None, None, None]
    ek = jnp.exp(kk * (lam_re * step)[None])
    pr, pi = ek * jnp.cos(kk * (lam_im * step)[None]), ek * jnp.sin(kk * (lam_im * step)[None])
    abr = pr[..., None] * br[None] - pi[..., None] * bi[None]
    abi = pr[..., None] * bi[None] + pi[..., None] * br[None]
    ein = functools.partial(jnp.einsum, precision=HIGHEST)
    kern = ein('dsgop,kdsgpj->kdsgoj', c_re, abr[:t]) - ein('dsgop,kdsgpj->kdsgoj', c_im, abi[:t])
    kf, kb = kern[:, :, 0], kern[:, :, 1]
    lag = jnp.concatenate([kb[:0:-1], (kf[0] + kb[0])[None], kf[1:]], axis=0)
    nd = lag.shape[1]
    flat = lag.transpose(1, 2, 4, 0, 3).reshape(nd, S5_GROUPS, S5_GROUP, (2 * t - 1) * S5_GROUP)
    m = jnp.stack([flat[..., (t - 1 - j) * S5_GROUP:(2 * t - 1 - j) * S5_GROUP] for j in range(t)], axis=2)
    m = m.reshape(nd, S5_GROUPS, t * S5_GROUP, t * S5_GROUP)
    wf = jnp.stack([abr[:t][::-1, :, 0], abi[:t][::-1, :, 0]], axis=0)
    wb = jnp.stack([abr[:t, :, 1], abi[:t, :, 1]], axis=0)
    w = jnp.concatenate([wf, wb], axis=0)
    w = w.transpose(2, 3, 1, 5, 0, 4).reshape(w.shape[2], S5_GROUPS, t * S5_GROUP, 4 * S5_STATE)
    cr, ci = c_re[None], c_im[None]
    vre = cr * pr[:, :, :, :, None, :] - ci * pi[:, :, :, :, None, :]
    vim = -(cr * pi[:, :, :, :, None, :] + ci * pr[:, :, :, :, None, :])
    vf = jnp.stack([vre[1:, :, 0], vim[1:, :, 0]], axis=0)
    vb = jnp.stack([vre[:0:-1, :, 1], vim[:0:-1, :, 1]], axis=0)
    v = jnp.concatenate([vf, vb], axis=0)
    v = v.transpose(2, 3, 0, 5, 1, 4).reshape(v.shape[2], S5_GROUPS, 4 * S5_STATE, t * S5_GROUP)
    atr, ati = pr[t], pi[t]
    row = lambda re_part, im_part: jnp.concatenate([re_part, im_part], axis=-1).reshape(nd, S5_GROUPS * LANES)
    coef = jnp.stack([row(atr[:, 0], atr[:, 0]), row(-ati[:, 0], ati[:, 0]),
                      row(atr[:, 1], atr[:, 1]), row(-ati[:, 1], ati[:, 1])], axis=1)
    return m.astype(BF16), w.astype(BF16), v.astype(BF16), coef


def _mix_kernel(x_ref, mod_ref, a_ref, yf_ref, yb_ref, z_ref, gs_ref, y5_ref, u5_ref, d5_ref, wg_ref, bg_ref,
                f_ref, wo_ref, g2_ref, wr_ref, br_ref, o_ref, r_ref):
    mod = mod_ref[0]
    d = x_ref.shape[1]
    ssd = _rms((yf_ref[0] + yb_ref[0]) * _silu(z_ref[0]), SSD_INNER) * gs_ref[...]
    y5 = y5_ref[0] + d5_ref[...] * u5_ref[0]
    g5 = jax.nn.gelu(y5)
    s5 = g5 * _sigmoid(_dot(g5.astype(BF16), wg_ref[...]) + bg_ref[...])
    acc = _dot(a_ref[0].astype(BF16), wo_ref[0:256, :])
    acc += _dot(ssd.astype(BF16), wo_ref[256:512, :])
    acc += _dot(s5.astype(BF16), wo_ref[512:768, :])
    acc += _dot(f_ref[0].astype(BF16), wo_ref[768:1024, :])
    xn = x_ref[...] + mod[2:3] * acc
    h = _rms(xn, d) * g2_ref[...]
    h = h * (1.0 + mod[4:5]) + mod[3:4]
    route = _route(h, wr_ref, br_ref)
    o_ref[0, :, :d] = xn
    o_ref[0, :, d:] = route
    r_ref[0] = route


def _mix(x, layer, modtab, a, yf, yb, ssd_raw, gs, y5, u5, d5, wglu, bglu, f, wo, g2, wr, br):
    (b, l, _), d = a.shape, x.shape[1]
    nt = l // TOK_TILE
    tok = lambda c: pl.BlockSpec((1, TOK_TILE, c), lambda i, j: (i, j, 0))
    full = lambda arr: pl.BlockSpec(arr.shape, lambda i, j: (0,) * arr.ndim)
    return pl.pallas_call(
        _mix_kernel,
        grid=(b, nt),
        in_specs=[pl.BlockSpec((TOK_TILE, d), lambda i, j: (i * nt + j, 0)),
                  pl.BlockSpec((1, 8, d), lambda i, j: (2 * i + jnp.minimum(j, 1), 0, 0)),
                  tok(256), tok(256), tok(256), tok(256), full(gs), tok(256), tok(256), full(d5), full(wglu),
                  full(bglu), tok(256), pl.BlockSpec((None,) + wo.shape[1:], lambda i, j: (layer, 0, 0)),
                  full(g2), full(wr), full(br)],
        out_specs=[tok(d + LANES), tok(LANES)],
        out_shape=[jax.ShapeDtypeStruct((b, l, d + LANES), F32), jax.ShapeDtypeStruct((b, l, LANES), F32)],
        compiler_params=_cparams(("parallel", "arbitrary")),
        name="mix_out",
    )(x, modtab, a, yf, yb, ssd_raw, gs, y5, u5, d5, wglu, bglu, f, wo, g2, wr, br)


def _route(h, wr_ref, br_ref):
    lane = lax.broadcasted_iota(jnp.int32, (1, LANES), 1)
    lanef = lane.astype(F32)
    logits = _dot_3pass(h, wr_ref[...]) + br_ref[...]
    neg = -jnp.inf
    big = float(4 * LANES)
    first = lambda hit: jnp.min(jnp.where(hit, lanef, big), axis=-1, keepdims=True)
    glog = jnp.where(lane < MOE_GROUPS, logits, neg)
    gmax = jnp.max(glog, axis=-1, keepdims=True)
    gidx = first(glog == gmax)
    p_group = 1.0 / jnp.sum(jnp.exp(glog - gmax), axis=-1, keepdims=True)
    e0 = MOE_GROUPS + gidx * MOE_PER_GROUP
    elog = jnp.where((lanef >= e0) & (lanef < e0 + MOE_PER_GROUP), logits, neg)
    v1 = jnp.max(elog, axis=-1, keepdims=True)
    i1 = first(elog == v1)
    elog2 = jnp.where(lanef == i1, neg, elog)
    v2 = jnp.max(elog2, axis=-1, keepdims=True)
    i2 = first(elog2 == v2)
    e2 = jnp.exp(v2 - v1)
    w1 = p_group / (1.0 + e2)
    w2 = p_group * e2 / (1.0 + e2)
    return (jnp.where(lanef == i1 - e0, w1, 0.0) + jnp.where(lanef == i2 - e0, w2, 0.0)
            + jnp.where(lane == MOE_PER_GROUP, gidx, 0.0))


def _route_plan(gid, tm, nt):
    t = gid.shape[0]
    onehot = (gid[:, None] == jnp.arange(MOE_GROUPS, dtype=jnp.int32)[None]).astype(jnp.int32)
    csum = jnp.cumsum(onehot, axis=0)
    rank = jnp.sum((csum - onehot) * onehot, axis=1)
    padded = (csum[-1] + tm - 1) // tm * tm
    ends = jnp.cumsum(padded)
    slot = jnp.sum(onehot * (ends - padded)[None], axis=1) + rank
    src = jnp.full((nt * tm,), -1, jnp.int32).at[slot].set(jnp.arange(t, dtype=jnp.int32))
    pos = jnp.arange(nt * tm, dtype=jnp.int32)
    spare = t + (pos // tm % 2) * tm + pos % tm
    dst = jnp.concatenate([t + tm + pos[:tm], jnp.where(src >= 0, src, spare)])
    starts = jnp.arange(nt, dtype=jnp.int32) * tm
    tile_group = jnp.minimum(jnp.sum((starts[:, None] >= ends[None]).astype(jnp.int32), axis=1), MOE_GROUPS - 1)
    return src, dst, tile_group


def _moe_kernel(src_s, dst_s, tg_s, srcv_ref, xg_hbm, mod_ref, g2_ref, wg_ref, wu_ref, wd_ref, out_hbm,
                gbuf0, gbuf1, ybuf0, ybuf1, gsem, ssem, *, seq_len, n_ctx):
    del tg_s
    i = pl.program_id(0)
    nt = pl.num_programs(0)
    gbufs, ybufs = (gbuf0, gbuf1), (ybuf0, ybuf1)
    tm, d = ybuf0.shape
    nxt = jnp.minimum(i + 1, nt - 1)

    def gather_copy(tile, sl, r):
        t = jnp.maximum(src_s[tile * tm + r], 0)
        return pltpu.make_async_copy(xg_hbm.at[pl.ds(t, 1)], gbufs[sl].at[pl.ds(r, 1)], gsem.at[sl])

    def scatter_copy(tile, sl, r):
        t = dst_s[(tile + 1) * tm + r]
        return pltpu.make_async_copy(ybufs[sl].at[pl.ds(r, 1)], out_hbm.at[pl.ds(t, 1)], ssem.at[sl])

    def each_row(make_copy, tile, sl, wait):
        def body(r, c):
            cp = make_copy(tile, sl, r)
            cp.wait() if wait else cp.start()
            return c
        lax.fori_loop(0, tm, body, 0, unroll=8)

    @pl.when(i == 0)
    def _():
        each_row(gather_copy, 0, 0, False)
        n_tok = out_hbm.shape[0] - 2 * tm
        for sl in range(2):
            ybufs[sl][...] = jnp.zeros((tm, d), F32)
        fills = [pltpu.make_async_copy(ybufs[sl], out_hbm.at[pl.ds(n_tok + sl * tm, tm)], ssem.at[sl])
                 for sl in range(2)]
        for cp in fills:
            cp.start()
        for cp in fills:
            cp.wait()

    def modrow(tok, r):
        out = jnp.zeros((tm, d), F32)
        for sg in range(mod_ref.shape[0]):
            lo_t = (sg // 2) * seq_len + (0 if sg % 2 == 0 else n_ctx)
            hi_t = (sg // 2) * seq_len + (n_ctx if sg % 2 == 0 else seq_len)
            out = jnp.where((tok >= lo_t) & (tok < hi_t), mod_ref[sg, r:r + 1, :], out)
        return out

    def tile_step(sl):
        ot = 1 - sl
        each_row(gather_copy, i, sl, True)

        @pl.when(i >= 1)
        def _():
            each_row(scatter_copy, i - 2, sl, True)

        x = gbufs[sl][:, :d]
        gates = gbufs[sl][:, d:]
        tok = jnp.maximum(srcv_ref[...], 0)
        h = _rms(x, d) * g2_ref[...]
        h = (h * (1.0 + modrow(tok, 4)) + modrow(tok, 3)).astype(BF16)
        y = jnp.zeros((tm, d), F32)
        rows_per_expert = tm // MOE_PER_GROUP
        for e in range(MOE_PER_GROUP):
            for r in range(e * rows_per_expert, (e + 1) * rows_per_expert):
                gather_copy(nxt, ot, r).start()
                scatter_copy(i - 1, ot, r).start()
            hid = _silu(_dot(h, wg_ref[e])) * _dot(h, wu_ref[e]) * gates[:, e:e + 1]
            y += _dot(hid.astype(BF16), wd_ref[e])
        ybufs[sl][...] = x + modrow(tok, 5) * y

        @pl.when(i == nt - 1)
        def _():
            each_row(scatter_copy, i, sl, False)
            each_row(scatter_copy, i - 1, ot, True)
            each_row(scatter_copy, i, sl, True)
            each_row(gather_copy, nxt, ot, True)

    for sl in range(2):
        pl.when(i % 2 == sl)(functools.partial(tile_step, sl))


def _moe(xg, src, dst, tile_group, modtab, g2, wg, wu, wd, layer, seq_len, n_ctx):
    t, wx = xg.shape
    d = wx - LANES
    tm = MOE_TILE
    nt = src.shape[0] // tm
    full = lambda arr: pl.BlockSpec(arr.shape, lambda i, s, ds, g: (0,) * arr.ndim)
    wspec = lambda arr: pl.BlockSpec((None,) + arr.shape[1:], lambda i, s, ds, g: (layer * MOE_GROUPS + g[i], 0, 0, 0))
    grid_spec = pltpu.PrefetchScalarGridSpec(
        num_scalar_prefetch=3,
        grid=(nt,),
        in_specs=[pl.BlockSpec((tm, 1), lambda i, s, ds, g: (i, 0)),
                  pl.BlockSpec(memory_space=pl.ANY),
                  full(modtab), full(g2), wspec(wg), wspec(wu), wspec(wd)],
        out_specs=pl.BlockSpec(memory_space=pl.ANY),
        scratch_shapes=[pltpu.VMEM((tm, wx), F32), pltpu.VMEM((tm, wx), F32),
                        pltpu.VMEM((tm, d), F32), pltpu.VMEM((tm, d), F32),
                        pltpu.SemaphoreType.DMA((2,)), pltpu.SemaphoreType.DMA((2,))])
    return pl.pallas_call(
        functools.partial(_moe_kernel, seq_len=seq_len, n_ctx=n_ctx),
        grid_spec=grid_spec,
        out_shape=jax.ShapeDtypeStruct((t + 2 * tm, d), F32),
        compiler_params=_cparams(("arbitrary",)),
        name="moe",
    )(src, dst, tile_group, src.reshape(-1, 1), xg, modtab, g2, wg, wu, wd)


def _pad_in_proj(w_in):
    depth, d, _ = w_in.shape
    pieces = [(MLA_QK, LANES - MLA_QK)] * MLA_HEADS
    pieces += [(MLA_KV_RANK, MLA_NOPE), (MLA_ROPE, LANES - MLA_QK)]
    pieces += [(SSD_COLS, P_S5 - P_SSD - SSD_COLS), (S5_COLS, 0), (512, 0)]
    pieces += [(DIFF_V, LANES - DIFF_V)] * DIFF_HEADS
    cols, o = [], 0
    for width, gap in pieces:
        cols.append(w_in[:, :, o:o + width])
        o += width
        if gap:
            cols.append(jnp.zeros((depth, d, gap), w_in.dtype))
    out = jnp.concatenate(cols, axis=-1)
    assert o == w_in.shape[2] and out.shape[2] == P_COLS
    return out


def _rope_tables(seq, n_ctx):
    pos = jnp.arange(seq, dtype=jnp.int32)
    rows = (pos // GRID_W).astype(F32)
    cols = (pos % GRID_W).astype(F32)
    half = MLA_ROPE // 2
    inv = ROPE_BASE ** (-jnp.arange(0, half, 2, dtype=F32) / half)
    r = jnp.arange(MLA_ROPE)
    freq = inv[r % (half // 2)]
    ang = jnp.where((r < half)[None, :], rows[:, None], cols[:, None]) * freq[None, :]
    cos, sin = jnp.cos(ang), jnp.sin(ang)
    first = ((r % half) < half // 2)[None, :]
    up = jnp.where(first, -sin, 0.0)
    dn = jnp.where(first, 0.0, sin)
    t32 = jnp.stack([cos, up, dn])
    ident = jnp.stack([jnp.ones((n_ctx, MLA_ROPE), F32), jnp.zeros((n_ctx, MLA_ROPE), F32),
                       jnp.zeros((n_ctx, MLA_ROPE), F32)])
    t32 = jnp.concatenate([ident, t32], axis=1)
    n = t32.shape[1]
    base = jnp.stack([jnp.ones((n, LANES), F32), jnp.zeros((n, LANES), F32), jnp.zeros((n, LANES), F32)])
    rope_m = base.at[:, :, MLA_NOPE:MLA_QK].set(t32)
    rope_d = jnp.tile(t32, (1, 1, LANES // MLA_ROPE))
    return rope_m, rope_d


def _pad_lanes(v, n):
    return jnp.zeros((1, n), F32).at[0, :v.shape[0]].set(v)


def kernel(x, c, ctx, c_ctx, w_ada, b_ada, norm1, norm2, w_in, w_out, mla_kv_norm, mla_w_uk, mla_w_uv, mla_q_norm, mla_k_norm, ssd_conv_w, ssd_conv_b, ssd_a_log, ssd_dt_bias, ssd_d, ssd_norm, s5_lam_re, s5_lam_im, s5_log_step, s5_b_re, s5_b_im, s5_c_re, s5_c_im, s5_d, s5_w_glu, s5_b_glu, diff_q_norm, diff_k_norm, diff_lq1, diff_lk1, diff_lq2, diff_lk2, diff_subln, moe_w_group, moe_b_group, moe_w_expert, moe_b_expert, moe_w_gate, moe_w_up, moe_w_down):
    bsz, seq, d = x.shape
    n_ctx = ctx.shape[1]
    depth = w_in.shape[0]
    assert n_ctx == TOK_TILE and seq % FLASH_KV_CHUNK == 0 and (n_ctx + seq) % FLASH_Q_TILE == 0 and d == 1024
    assert FLASH_Q_TILE >= n_ctx
    l = n_ctx + seq
    assert (bsz * l) % MOE_TILE == 0 and bsz * l >= 2 * MOE_TILE and l % SSD_CHUNK == 0 and n_ctx % SSD_CHUNK == 0

    cvec = jnp.zeros((8, d), F32).at[:bsz].set(c).at[bsz].set(c_ctx)
    ada = _ada(cvec, w_ada, b_ada).reshape(depth, 8, N_MOD, d)
    mod_lat = ada[:, :bsz]
    mod_ctx = jnp.broadcast_to(ada[:, bsz][:, None], mod_lat.shape)
    modtab = jnp.stack([mod_ctx, mod_lat], axis=2)
    modtab = jnp.pad(modtab, ((0, 0), (0, 0), (0, 0), (0, 8 - N_MOD), (0, 0))).reshape(depth, bsz * 2, 8, d)

    w_in_p = _pad_in_proj(w_in).astype(BF16)
    w_out_b = w_out.astype(BF16)
    rope_m, rope_d = _rope_tables(seq, n_ctx)
    wuk = jnp.zeros((depth, MLA_KV_RANK, MLA_HEADS * LANES), F32)
    for hd in range(MLA_HEADS):
        wuk = wuk.at[:, :, hd * LANES:hd * LANES + MLA_NOPE].set(mla_w_uk[:, :, hd * MLA_NOPE:(hd + 1) * MLA_NOPE])
    wuk = wuk.astype(BF16)
    wuv = jnp.zeros((depth, MLA_KV_RANK, MLA_HEADS * LANES), F32)
    for hd in range(MLA_HEADS):
        wuv = wuv.at[:, :, hd * LANES:hd * LANES + MLA_V].set(mla_w_uv[:, :, hd * MLA_V:(hd + 1) * MLA_V])
    wuv = wuv.astype(BF16)
    amax = lambda g: jnp.max(jnp.abs(g.astype(F32)), axis=-1)
    bounded_m = (math.sqrt(MLA_QK) * amax(mla_q_norm) * amax(mla_k_norm) <= SCORE_BOUND).astype(F32)
    bounded_d = (math.sqrt(DIFF_HEAD) * amax(diff_q_norm) * amax(diff_k_norm) <= SCORE_BOUND).astype(F32)
    lane = jnp.arange(LANES)
    seg = (lane[:, None] // DIFF_HEAD == lane[None, :] // DIFF_HEAD).astype(F32)
    a_vec = -jnp.exp(ssd_a_log.astype(F32)).reshape(depth, 2 * SSD_HEADS)
    s5_m, s5_w, s5_v, s5_coef = _s5_matrices(
        s5_lam_re.astype(F32), s5_lam_im.astype(F32), s5_log_step.astype(F32),
        s5_b_re.astype(F32), s5_b_im.astype(F32), s5_c_re.astype(F32), s5_c_im.astype(F32))
    w_router = jnp.concatenate([moe_w_group, moe_w_expert.transpose(0, 2, 1, 3).reshape(depth, d, N_EXPERTS)], axis=-1)
    w_router = jnp.pad(w_router, ((0, 0), (0, 0), (0, LANES - w_router.shape[-1])))
    b_router = jnp.concatenate([moe_b_group, moe_b_expert.reshape(depth, N_EXPERTS)], axis=-1)
    b_router = jnp.pad(b_router, ((0, 0), (0, LANES - b_router.shape[-1])))
    by_group = lambda w: w.astype(BF16).reshape((depth * MOE_GROUPS, MOE_PER_GROUP) + w.shape[2:])
    wg_b, wu_b, wd_b = by_group(moe_w_gate), by_group(moe_w_up), by_group(moe_w_down)
    lam_all = (jnp.exp(jnp.sum(diff_lq1.astype(F32) * diff_lk1.astype(F32), axis=-1))
               - jnp.exp(jnp.sum(diff_lq2.astype(F32) * diff_lk2.astype(F32), axis=-1)))

    n5 = l // S5_CHUNK
    xs = jnp.concatenate([ctx, x], axis=1).reshape(bsz * l, d)
    for i in range(depth):
        lam_init = 0.8 - 0.6 * math.exp(-0.3 * i)
        qm, km, vm, qd, kd, vd, ssd_raw, u5 = _front(
            xs, bsz, i, modtab[i], norm1[i][None], w_in_p, rope_m, rope_d,
            _pad_lanes(mla_q_norm[i], LANES), _pad_lanes(mla_k_norm[i], LANES), mla_kv_norm[i][None],
            wuk[i], wuv[i], jnp.tile(diff_q_norm[i], 4)[None], jnp.tile(diff_k_norm[i], 4)[None], seg)
        a = _flash(jnp.stack([bounded_m[i], jnp.zeros((), F32)]), qm, km, vm, jnp.ones((1, LANES), F32),
                   n_sub=1, n_ctx=n_ctx, post=1.0)
        f = _flash(jnp.stack([bounded_d[i], lam_all[i] + lam_init]), qd, kd, vd, jnp.tile(diff_subln[i], 2)[None],
                   n_sub=2, n_ctx=n_ctx, post=1.0 - lam_init)
        yf, yb = _ssd(ssd_raw, ssd_conv_w[i], ssd_conv_b[i][None],
                      _pad_lanes(ssd_dt_bias[i].reshape(-1), LANES), _pad_lanes(a_vec[i], LANES),
                      jnp.repeat(ssd_d[i], SSD_HEAD_DIM)[None], n_ctx)
        u5g = u5.reshape(bsz, n5, S5_CHUNK, S5_GROUPS, S5_GROUP).transpose(3, 0, 1, 2, 4)
        u5g = u5g.reshape(S5_GROUPS, bsz * n5, S5_CHUNK * S5_GROUP)
        y1, end_f, end_b = _s5_local(u5g, s5_m, s5_w, i)
        prev_f, prev_b = _s5_scan(end_f, end_b, s5_coef[i], bsz, n_ctx // S5_CHUNK)
        y5 = _s5_read(y1, prev_f, prev_b, s5_v, i)
        y5 = y5.reshape(S5_GROUPS, bsz, n5, S5_CHUNK, S5_GROUP).transpose(1, 2, 3, 0, 4).reshape(bsz, l, S5_COLS)
        xg, ginfo = _mix(xs, i, modtab[i], a, yf, yb, ssd_raw, ssd_norm[i][None], y5, u5, s5_d[i][None],
                  s5_w_glu[i].astype(BF16), s5_b_glu[i][None], f, w_out_b,
                  norm2[i][None], w_router[i], b_router[i][None])
        xg = xg.reshape(bsz * l, d + LANES)
        gid = ginfo.reshape(bsz * l, LANES)[:, MOE_PER_GROUP].astype(jnp.int32)
        src, dst, tile_group = _route_plan(gid, MOE_TILE, bsz * l // MOE_TILE + MOE_GROUPS)
        xs = _moe(xg, src, dst, tile_group, modtab[i], norm2[i][None], wg_b, wu_b, wd_b, i, l, n_ctx)
    return xs[:bsz * l].reshape(bsz, l, d)[:, n_ctx:]
```

```python
import functools
import math

import jax
import jax.numpy as jnp
from jax import lax
from jax.experimental import pallas as pl
from jax.experimental.pallas import tpu as pltpu

F32 = jnp.float32
BF16 = jnp.bfloat16
HIGHEST = lax.Precision.HIGHEST

LANES = 128
EPS = 1e-6
ROPE_BASE = 10000.0
GRID_W = 64
N_MOD = 6

GROUP_WIDTH = 256
MLA_HEADS, MLA_NOPE, MLA_ROPE, MLA_V, MLA_KV_RANK = 4, 64, 32, 64, 128
MLA_QK = MLA_NOPE + MLA_ROPE
SSD_HEADS, SSD_HEAD_DIM, SSD_GROUPS, SSD_STATE, SSD_INNER = 4, 64, 2, 64, 256
SSD_GN = SSD_GROUPS * SSD_STATE
SSD_XBC = SSD_INNER + 2 * SSD_GN
S5_GROUPS, S5_GROUP, S5_STATE = 16, 16, 64
DIFF_HEADS, DIFF_HEAD, DIFF_V = 4, 32, 64
MOE_GROUPS, MOE_PER_GROUP, N_EXPERTS, EXPERT_HIDDEN = 4, 8, 32, 256

MLA_COLS = MLA_HEADS * MLA_QK + MLA_KV_RANK + MLA_ROPE
SSD_COLS = SSD_INNER + SSD_XBC + 2 * SSD_HEADS
S5_COLS = 256
DIFF_COLS = 768

P_MLA_Q = 0
P_MLA_CKV = 512
P_MLA_KR = 640
P_SSD = 768
P_S5 = 1664
P_DIFF = 1920
P_COLS = 2944
V_ONE = 64
SCORE_BOUND = 40.0
LOG2E = 1.4426950408889634

TOK_TILE = 256
SSD_CHUNK = 256
S5_CHUNK = 32
FLASH_KV_CHUNK = 1024
FLASH_Q_TILE = 1408
MOE_TILE = 512
VMEM_LIMIT = 56 * 1024 * 1024


def _cparams(sem):
    return pltpu.CompilerParams(dimension_semantics=sem, vmem_limit_bytes=VMEM_LIMIT)


def _sigmoid(x):
    return 1.0 / (1.0 + jnp.exp(-x))


def _silu(x):
    return x * _sigmoid(x)


def _rms(x, n):
    return x * lax.rsqrt(jnp.sum(x * x, axis=-1, keepdims=True) * (1.0 / n) + EPS)


def _dot(a, b):
    return jnp.dot(a, b, preferred_element_type=F32)


def _dot_hi(a, b):
    return jnp.dot(a, b, preferred_element_type=F32, precision=HIGHEST)


def _dot_3pass(a, b):
    a_hi, b_hi = a.astype(BF16), b.astype(BF16)
    a_lo = (a - a_hi.astype(F32)).astype(BF16)
    b_lo = (b - b_hi.astype(F32)).astype(BF16)
    return _dot(a_hi, b_hi) + (_dot(a_hi, b_lo) + _dot(a_lo, b_hi))


def _dot_nt(a, b):
    return lax.dot_general(a, b, (((1,), (1,)), ((), ())), preferred_element_type=F32)


def _dot_tn(a, b):
    return lax.dot_general(a, b, (((0,), (0,)), ((), ())), preferred_element_type=F32)


def _dot_tn_hi(a, b):
    return lax.dot_general(a, b, (((0,), (0,)), ((), ())), preferred_element_type=F32, precision=HIGHEST)


def _ada_kernel(c_ref, w_ref, b_ref, o_ref):
    o_ref[0] = _dot_hi(_silu(c_ref[...]), w_ref[0]) + b_ref[0]


def _ada(cvec, w_ada, b_ada):
    depth, d, nd = w_ada.shape
    blk = 1024
    return pl.pallas_call(
        _ada_kernel,
        grid=(depth, nd // blk),
        in_specs=[pl.BlockSpec((8, d), lambda i, j: (0, 0)),
                  pl.BlockSpec((1, d, blk), lambda i, j: (i, 0, j)),
                  pl.BlockSpec((1, 1, blk), lambda i, j: (i, 0, j))],
        out_specs=pl.BlockSpec((1, 8, blk), lambda i, j: (i, 0, j)),
        out_shape=jax.ShapeDtypeStruct((depth, 8, nd), F32),
        compiler_params=_cparams(("arbitrary", "arbitrary")),
        name="ada",
    )(cvec, w_ada, b_ada.reshape(depth, 1, nd))


def _rope(x, cos, sin_up, sin_dn):
    n = x.shape[-1]
    return x * cos + pltpu.roll(x, n - 8, axis=1) * sin_up + pltpu.roll(x, 8, axis=1) * sin_dn


def _front_kernel(x_ref, mod_ref, g1_ref, w_ref, rm_ref, rd_ref, gq_ref, gk_ref, gkv_ref, wuk_ref, wuv_ref,
                  gdq_ref, gdk_ref, seg_ref, one_ref,
                  qm_ref, km_ref, vm_ref, qd_ref, kd_ref, vd_ref, ssd_ref, s5_ref):
    x = x_ref[...]
    mod = mod_ref[0]
    h = _rms(x, x.shape[-1]) * g1_ref[...]
    h = h * (1.0 + mod[1:2]) + mod[0:1]
    p = _dot(h.astype(BF16), w_ref[...])

    ssd_ref[0] = p[:, P_SSD:P_S5]
    s5_ref[0] = p[:, P_S5:P_DIFF]

    cos_m, up_m, dn_m = rm_ref[0], rm_ref[1], rm_ref[2]
    ckv = _rms(p[:, P_MLA_CKV:P_MLA_KR], MLA_KV_RANK) * gkv_ref[...]
    ckv = ckv.astype(BF16)
    k_nope = _dot(ckv, wuk_ref[...])
    vm_ref[0] = (_dot(ckv, wuv_ref[...]) + one_ref[...]).astype(BF16)
    k_rope = p[:, P_MLA_KR:P_SSD]
    q_scale = MLA_QK ** -0.5 * LOG2E
    for hd in range(MLA_HEADS):
        sl = slice(hd * LANES, (hd + 1) * LANES)
        q = _rms(p[:, sl], MLA_QK) * gq_ref[...]
        qm_ref[0, :, sl] = (_rope(q, cos_m, up_m, dn_m) * q_scale).astype(BF16)
        k = _rms(k_nope[:, sl] + k_rope, MLA_QK) * gk_ref[...]
        km_ref[0, :, sl] = _rope(k, cos_m, up_m, dn_m).astype(BF16)

    cos_d, up_d, dn_d = rd_ref[0], rd_ref[1], rd_ref[2]
    d_scale = DIFF_HEAD ** -0.5 * LOG2E
    for blk, (g_ref, o_ref, scale) in enumerate(((gdq_ref, qd_ref, d_scale), (gdk_ref, kd_ref, 1.0))):
        for half in range(2):
            sl = slice(P_DIFF + blk * 256 + half * LANES, P_DIFF + blk * 256 + (half + 1) * LANES)
            t = p[:, sl]
            ss = _dot_hi(t * t, seg_ref[...])
            t = t * lax.rsqrt(ss * (1.0 / DIFF_HEAD) + EPS) * g_ref[...]
            o_ref[0, :, half * LANES:(half + 1) * LANES] = (_rope(t, cos_d, up_d, dn_d) * scale).astype(BF16)
    vd_ref[0] = (p[:, P_DIFF + 512:P_COLS] + one_ref[...]).astype(BF16)


def _front(x, b, layer, modtab, g1, w_in_p, rope_m, rope_d, gq, gk, gkv, wuk, wuv, gdq, gdk, seg):
    l, d = rope_m.shape[1], x.shape[1]
    nt = l // TOK_TILE
    tok = lambda c: pl.BlockSpec((1, TOK_TILE, c), lambda i, j: (i, j, 0))
    full = lambda a: pl.BlockSpec(a.shape, lambda i, j: (0,) * a.ndim)
    rope_spec = pl.BlockSpec((3, TOK_TILE, LANES), lambda i, j: (0, j, 0))
    one = (jnp.arange(4 * LANES) % LANES == V_ONE).astype(F32)[None]
    outs = [(512, BF16), (512, BF16), (512, BF16), (256, BF16), (256, BF16), (512, BF16),
            (P_S5 - P_SSD, F32), (256, F32)]
    return pl.pallas_call(
        _front_kernel,
        grid=(b, nt),
        in_specs=[pl.BlockSpec((TOK_TILE, d), lambda i, j: (i * nt + j, 0)),
                  pl.BlockSpec((1, 8, d), lambda i, j: (2 * i + jnp.minimum(j, 1), 0, 0)),
                  full(g1), pl.BlockSpec((None,) + w_in_p.shape[1:], lambda i, j: (layer, 0, 0)), rope_spec, rope_spec,
                  full(gq), full(gk), full(gkv), full(wuk), full(wuv), full(gdq), full(gdk), full(seg), full(one)],
        out_specs=[tok(c) for c, _ in outs],
        out_shape=[jax.ShapeDtypeStruct((b, l, c), dt) for c, dt in outs],
        compiler_params=_cparams(("parallel", "arbitrary")),
        name="front",
    )(x, modtab, g1, w_in_p, rope_m, rope_d, gq, gk, gkv, wuk, wuv, gdq, gdk, seg, one)


def _flash_kernel(scal_ref, q_ref, k_ref, v_ref, sub_ref, o_ref, acc_scr, m_scr, *, n_sub, n_ctx, ck, post):
    qi = pl.program_id(2)
    lk = k_ref.shape[1]
    lane = lax.broadcasted_iota(jnp.int32, (1, LANES), 1)
    lo = lane < V_ONE
    q = q_ref[0]
    maps = [(a, c) for a in range(2) for c in range(n_sub)]

    def q_of(a, c):
        if n_sub == 1:
            return q[:, a * LANES:(a + 1) * LANES]
        s = (a * n_sub + c) * DIFF_HEAD
        return jnp.where((lane >= s) & (lane < s + DIFF_HEAD), q, jnp.zeros_like(q))

    qs = [q_of(a, c) for a, c in maps]

    def scores(mi, start, size):
        a = maps[mi][0]
        kk = k_ref[0, pl.ds(start, size), a * LANES:(a + 1) * LANES] if n_sub == 1 else k_ref[0, pl.ds(start, size), :]
        return _dot_nt(qs[mi], kk)

    def values(mi, start, size):
        a = maps[mi][0]
        return v_ref[0, pl.ds(start, size), a * LANES:(a + 1) * LANES]

    row = lax.broadcasted_iota(jnp.int32, (q.shape[0], 1), 0)
    sees_latent = (qi > 0) | (row >= n_ctx)

    def step_bounded(start, size, latent_keys):
        for mi in range(len(maps)):
            p = jnp.exp2(scores(mi, start, size))
            if latent_keys:
                p = jnp.where(sees_latent, p, 0.0)
            acc_scr[mi] += _dot(p.astype(BF16), values(mi, start, size))

    def step_online(start, size, latent_keys):
        for mi in range(len(maps)):
            s = scores(mi, start, size)
            if latent_keys:
                s = jnp.where(sees_latent, s, -jnp.inf)
            m_prev = m_scr[mi]
            m_new = jnp.maximum(m_prev, jnp.max(s, axis=-1, keepdims=True))
            p = jnp.exp2(s - m_new).astype(BF16)
            acc_scr[mi] = acc_scr[mi] * jnp.exp2(m_prev - m_new) + _dot(p, values(mi, start, size))
            m_scr[mi] = m_new

    def sweep(step):
        step(0, n_ctx, False)

        def body(i, carry):
            step(pl.multiple_of(n_ctx + i * ck, n_ctx), ck, True)
            return carry
        lax.fori_loop(0, (lk - n_ctx) // ck, body, 0)

    acc_scr[...] = jnp.zeros(acc_scr.shape, F32)
    bounded = scal_ref[0] != 0.0

    @pl.when(bounded)
    def _():
        sweep(step_bounded)

    @pl.when(jnp.logical_not(bounded))
    def _():
        m_scr[...] = jnp.full(m_scr.shape, -jnp.inf, F32)
        sweep(step_online)

    def normalised(mi):
        acc = acc_scr[mi]
        return acc * (1.0 / acc[:, V_ONE:V_ONE + 1])

    heads = []
    for a in range(2):
        if n_sub == 1:
            o = normalised(a)
        else:
            o = normalised(a * n_sub) - scal_ref[1] * normalised(a * n_sub + 1)
            ss = jnp.sum(jnp.where(lo, o * o, 0.0), axis=-1, keepdims=True)
            o = o * lax.rsqrt(ss * (1.0 / V_ONE) + EPS)
        heads.append(o)
    pair = jnp.where(lo, heads[0], pltpu.roll(heads[1], V_ONE, axis=1))
    o_ref[0] = pair if n_sub == 1 else pair * sub_ref[...] * post


def _flash(scal, q, k, v, subln, *, n_sub, n_ctx, post):
    b, l, qw = q.shape
    wq = qw // 2
    tq = FLASH_Q_TILE
    nq = l // tq
    kern = functools.partial(_flash_kernel, n_sub=n_sub, n_ctx=n_ctx, ck=FLASH_KV_CHUNK, post=post)
    return pl.pallas_call(
        kern,
        grid=(b, 2, nq),
        in_specs=[pl.BlockSpec(memory_space=pltpu.SMEM),
                  pl.BlockSpec((1, tq, wq), lambda i, h, j: (i, j, h)),
                  pl.BlockSpec((1, l, wq), lambda i, h, j: (i, 0, h)),
                  pl.BlockSpec((1, l, 2 * LANES), lambda i, h, j: (i, 0, h)),
                  pl.BlockSpec((1, LANES), lambda i, h, j: (0, 0))],
        out_specs=pl.BlockSpec((1, tq, LANES), lambda i, h, j: (i, j, h)),
        out_shape=jax.ShapeDtypeStruct((b, l, 2 * LANES), F32),
        scratch_shapes=[pltpu.VMEM((2 * n_sub, tq, LANES), F32),
                        pltpu.VMEM((2 * n_sub, tq, 1), F32)],
        compiler_params=_cparams(("parallel", "parallel", "arbitrary")),
        name="flash_diff" if n_sub == 2 else "flash_mla",
    )(scal, q, k, v, subln)


def _ssd_dir(d, first, xbc_ref, prev_ref, next_ref, edge_lo, edge_hi, cw_ref, cb_ref, dtb_ref, a_ref, dsk_ref,
             y_ref, st_scr):
    t = xbc_ref.shape[1]
    raw = xbc_ref[0]
    x = raw[:, SSD_INNER:SSD_INNER + SSD_XBC]
    row = lax.broadcasted_iota(jnp.int32, (t, 1), 0)
    prev_row = jnp.where(edge_lo, 0.0, prev_ref[0, 7:8, SSD_INNER:SSD_INNER + SSD_XBC])
    next_row = jnp.where(edge_hi, 0.0, next_ref[0, 0:1, SSD_INNER:SSD_INNER + SSD_XBC])
    xm1 = jnp.where(row == 0, prev_row, pltpu.roll(x, 1, axis=0))
    xp1 = jnp.where(row == t - 1, next_row, pltpu.roll(x, t - 1, axis=0))
    xc = _silu(cw_ref[0:1] * xm1 + cw_ref[1:2] * x + cw_ref[2:3] * xp1 + cb_ref[...])
    xs = xc[:, :SSD_INNER]
    bm = xc[:, SSD_INNER:SSD_INNER + SSD_GN]
    cm = xc[:, SSD_INNER + SSD_GN:]

    z = raw[:, SSD_INNER + SSD_XBC:] + dtb_ref[...]
    dt = jnp.maximum(z, 0.0) + jnp.log1p(jnp.exp(-jnp.abs(z)))
    da = dt * a_ref[...]

    ri = lax.broadcasted_iota(jnp.int32, (t, t), 0)
    ci = lax.broadcasted_iota(jnp.int32, (t, t), 1)
    mask = (ci <= ri) if d == 0 else (ci >= ri)
    mask_t = (ri <= ci) if d == 0 else (ri >= ci)
    da_hi = da.astype(BF16)
    da_r = da - da_hi.astype(F32)
    da_mid = da_r.astype(BF16)
    da_parts = (da_hi, da_mid, (da_r - da_mid.astype(F32)).astype(BF16))
    mask_b, mask_tb = mask.astype(BF16), mask_t.astype(BF16)
    cum = sum(_dot(mask_b, part) for part in da_parts)
    cum_t = sum(_dot_tn(part, mask_tb) for part in da_parts)
    tot = jnp.sum(da, axis=0, keepdims=True)

    lane = lax.broadcasted_iota(jnp.int32, (1, LANES), 1)
    lo = lane < (LANES // 2)

    @pl.when(first)
    def _():
        st_scr[d] = jnp.zeros(st_scr.shape[1:], F32)

    ys = []
    for g in range(SSD_GROUPS):
        gmask = lo if g == 0 else jnp.logical_not(lo)
        cg = jnp.where(gmask, cm, 0.0)
        gram = _dot_nt(cg.astype(BF16), bm.astype(BF16))
        xpair = xs[:, g * LANES:(g + 1) * LANES]
        yh = []
        for hh in range(2):
            hd = g * 2 + hh
            col = d * SSD_HEADS + hd
            cum_c = cum[:, col:col + 1]
            cum_r = cum_t[col:col + 1, :]
            dt_c = dt[:, col:col + 1]
            tot_c = tot[:, col:col + 1]
            decay = jnp.exp(jnp.where(mask, cum_c - cum_r, -jnp.inf))
            xdt = (xpair * dt_c).astype(BF16)
            st = st_scr[d, hd]
            y = (_dot((gram * decay).astype(BF16), xdt)
                 + _dot((cg * jnp.exp(cum_c)).astype(BF16), st.astype(BF16)))
            st_scr[d, hd] = st * jnp.exp(tot_c) + _dot_tn((bm * jnp.exp(tot_c - cum_c)).astype(BF16), xdt)
            yh.append(y)
        ys.append(jnp.where(lo, yh[0], yh[1]))
    y = jnp.concatenate(ys, axis=1)
    if d == 0:
        y = y + xs * dsk_ref[...]
    y_ref[0] = y


def _ssd_kernel(xf_ref, pf_ref, nf_ref, xb_ref, pb_ref, nb_ref, cw_ref, cb_ref, dtb_ref, a_ref, dsk_ref,
                yf_ref, yb_ref, st_scr, *, n_ctx_chunks):
    s = pl.program_id(1)
    nc = pl.num_programs(1)
    cf = s
    cb = jnp.where(s < n_ctx_chunks, n_ctx_chunks - 1 - s, nc - 1 + n_ctx_chunks - s)
    seg_lo = lambda c: (c == 0) | (c == n_ctx_chunks)
    seg_hi = lambda c: (c == n_ctx_chunks - 1) | (c == nc - 1)
    _ssd_dir(0, s == 0, xf_ref, pf_ref, nf_ref, seg_lo(cf), seg_hi(cf), cw_ref, cb_ref, dtb_ref, a_ref, dsk_ref,
             yf_ref, st_scr)
    _ssd_dir(1, s == 0, xb_ref, pb_ref, nb_ref, seg_lo(cb), seg_hi(cb), cw_ref, cb_ref, dtb_ref, a_ref, dsk_ref,
             yb_ref, st_scr)


def _ssd(raw, conv_w, conv_b, dt_bias, a_vec, dskip, n_ctx):
    b, l, w = raw.shape
    t = SSD_CHUNK
    nc = l // t
    ncc = n_ctx // t
    r8 = t // 8
    nb8 = l // 8
    fwd = lambda s: s
    bwd = lambda s: jnp.where(s < ncc, ncc - 1 - s, nc - 1 + ncc - s)
    chunk = lambda f: pl.BlockSpec((1, t, w), lambda i, s: (i, f(s), 0))
    prev = lambda f: pl.BlockSpec((1, 8, w), lambda i, s: (i, jnp.maximum(f(s) * r8 - 1, 0), 0))
    nxt = lambda f: pl.BlockSpec((1, 8, w), lambda i, s: (i, jnp.minimum((f(s) + 1) * r8, nb8 - 1), 0))
    full = lambda a: pl.BlockSpec(a.shape, lambda i, s: (0,) * a.ndim)
    out = lambda f: pl.BlockSpec((1, t, SSD_INNER), lambda i, s: (i, f(s), 0))
    return pl.pallas_call(
        functools.partial(_ssd_kernel, n_ctx_chunks=ncc),
        grid=(b, nc),
        in_specs=[chunk(fwd), prev(fwd), nxt(fwd), chunk(bwd), prev(bwd), nxt(bwd),
                  full(conv_w), full(conv_b), full(dt_bias), full(a_vec), full(dskip)],
        out_specs=[out(fwd), out(bwd)],
        out_shape=[jax.ShapeDtypeStruct((b, l, SSD_INNER), F32)] * 2,
        scratch_shapes=[pltpu.VMEM((2, SSD_HEADS, LANES, LANES), F32)],
        compiler_params=_cparams(("parallel", "arbitrary")),
        name="ssd",
    )(raw, raw, raw, raw, raw, raw, conv_w, conv_b, dt_bias, a_vec, dskip)


def _s5_local_kernel(u_ref, m_ref, w_ref, y_ref, sf_ref, sb_ref):
    u = u_ref[0].astype(BF16)
    y_ref[0] = _dot(u, m_ref[0])
    s = _dot(u, w_ref[0])
    sf_ref[...] = s[:, :LANES]
    sb_ref[...] = s[:, LANES:]


def _s5_local(u, m, w, layer):
    g, nch, wd = u.shape
    nw = wd
    ns = w.shape[3]
    state = jax.ShapeDtypeStruct((nch, g * LANES), F32)
    return pl.pallas_call(
        _s5_local_kernel,
        grid=(g,),
        in_specs=[pl.BlockSpec((1, nch, wd), lambda i: (i, 0, 0)),
                  pl.BlockSpec((None, 1, wd, nw), lambda i: (layer, i, 0, 0)),
                  pl.BlockSpec((None, 1, wd, ns), lambda i: (layer, i, 0, 0))],
        out_specs=[pl.BlockSpec((1, nch, nw), lambda i: (i, 0, 0)),
                   pl.BlockSpec((nch, LANES), lambda i: (0, i)),
                   pl.BlockSpec((nch, LANES), lambda i: (0, i))],
        out_shape=[jax.ShapeDtypeStruct((g, nch, nw), F32), state, state],
        compiler_params=_cparams(("parallel",)),
        name="s5_local",
    )(u, m, w)


def _s5_scan_kernel(sf_ref, sb_ref, co_ref, of_ref, ob_ref, *, n_ctx_chunks):
    n, w = sf_ref.shape
    lane = lax.broadcasted_iota(jnp.int32, (1, w), 1)
    re_half = lane % LANES < LANES // 2

    def swap(v):
        return jnp.where(re_half, pltpu.roll(v, w - LANES // 2, axis=1), pltpu.roll(v, LANES // 2, axis=1))

    def body(i, st):
        st_f, st_b = st
        jf = i
        jb = jnp.where(i < n_ctx_chunks, n_ctx_chunks - 1 - i, n - 1 + n_ctx_chunks - i)
        of_ref[pl.ds(jf, 1), :] = st_f
        ob_ref[pl.ds(jb, 1), :] = st_b
        st_f = st_f * co_ref[0:1] + swap(st_f) * co_ref[1:2] + sf_ref[pl.ds(jf, 1), :]
        st_b = st_b * co_ref[2:3] + swap(st_b) * co_ref[3:4] + sb_ref[pl.ds(jb, 1), :]
        return st_f, st_b

    zero = jnp.zeros((1, w), F32)
    lax.fori_loop(0, n, body, (zero, zero))


def _s5_scan(sf, sb, coef, n_batch, n_ctx_chunks):
    nch, w = sf.shape
    n = nch // n_batch
    blk = pl.BlockSpec((n, w), lambda i: (i, 0))
    return pl.pallas_call(
        functools.partial(_s5_scan_kernel, n_ctx_chunks=n_ctx_chunks),
        grid=(n_batch,),
        in_specs=[blk, blk, pl.BlockSpec(coef.shape, lambda i: (0, 0))],
        out_specs=[blk, blk],
        out_shape=[jax.ShapeDtypeStruct((nch, w), F32)] * 2,
        compiler_params=_cparams(("parallel",)),
        name="s5_scan",
    )(sf, sb, coef)


def _s5_read_kernel(y1_ref, pf_ref, pb_ref, v_ref, y_ref):
    y_ref[0] = (y1_ref[0] + _dot(pf_ref[...].astype(BF16), v_ref[0, :LANES, :])
                + _dot(pb_ref[...].astype(BF16), v_ref[0, LANES:, :]))


def _s5_read(y1, prev_f, prev_b, vmat, layer):
    g, nch, nw = y1.shape
    ns = vmat.shape[2]
    state = pl.BlockSpec((nch, LANES), lambda i: (0, i))
    return pl.pallas_call(
        _s5_read_kernel,
        grid=(g,),
        in_specs=[pl.BlockSpec((1, nch, nw), lambda i: (i, 0, 0)), state, state,
                  pl.BlockSpec((None, 1, ns, nw), lambda i: (layer, i, 0, 0))],
        out_specs=pl.BlockSpec((1, nch, nw), lambda i: (i, 0, 0)),
        out_shape=jax.ShapeDtypeStruct((g, nch, nw), F32),
        compiler_params=_cparams(("parallel",)),
        name="s5_read",
    )(y1, prev_f, prev_b, vmat)


def _s5_matrices(lam_re, lam_im, log_step, b_re, b_im, c_re, c_im):
    t = S5_CHUNK
    step = jnp.exp(log_step)[..., None]
    den = lam_re * lam_re + lam_im * lam_im
    er = jnp.exp(lam_re * step)
    ar, ai = er * jnp.cos(lam_im * step), er * jnp.sin(lam_im * step)
    nr, ni = ar - 1.0, ai
    fr = (nr * lam_re + ni * lam_im) / den
    fi = (ni * lam_re - nr * lam_im) / den
    bre, bim = b_re[:, None], b_im[:, None]
    br = fr[..., None] * bre - fi[..., None] * bim
    bi = fr[..., None] * bim + fi[..., None] * bre
    kk = jnp.arange(t + 1, dtype=F32)[:, None, None, None, None]
    ek = jnp.exp(kk * (lam_re * step)[None])
    pr, pi = ek * jnp.cos(kk * (lam_im * step)[None]), ek * jnp.sin(kk * (lam_im * step)[None])
    abr = pr[..., None] * br[None] - pi[..., None] * bi[None]
    abi = pr[..., None] * bi[None] + pi[..., None] * br[None]
    ein = functools.partial(jnp.einsum, precision=HIGHEST)
    kern = ein('dsgop,kdsgpj->kdsgoj', c_re, abr[:t]) - ein('dsgop,kdsgpj->kdsgoj', c_im, abi[:t])
    kf, kb = kern[:, :, 0], kern[:, :, 1]
    lag = jnp.concatenate([kb[:0:-1], (kf[0] + kb[0])[None], kf[1:]], axis=0)
    nd = lag.shape[1]
    flat = lag.transpose(1, 2, 4, 0, 3).reshape(nd, S5_GROUPS, S5_GROUP, (2 * t - 1) * S5_GROUP)
    m = jnp.stack([flat[..., (t - 1 - j) * S5_GROUP:(2 * t - 1 - j) * S5_GROUP] for j in range(t)], axis=2)
    m = m.reshape(nd, S5_GROUPS, t * S5_GROUP, t * S5_GROUP)
    wf = jnp.stack([abr[:t][::-1, :, 0], abi[:t][::-1, :, 0]], axis=0)
    wb = jnp.stack([abr[:t, :, 1], abi[:t, :, 1]], axis=0)
    w = jnp.concatenate([wf, wb], axis=0)
    w = w.transpose(2, 3, 1, 5, 0, 4).reshape(w.shape[2], S5_GROUPS, t * S5_GROUP, 4 * S5_STATE)
    cr, ci = c_re[None], c_im[None]
    vre = cr * pr[:, :, :, :, None, :] - ci * pi[:, :, :, :, None, :]
    vim = -(cr * pi[:, :, :, :, None, :] + ci * pr[:, :, :, :, None, :])
    vf = jnp.stack([vre[1:, :, 0], vim[1:, :, 0]], axis=0)
    vb = jnp.stack([vre[:0:-1, :, 1], vim[:0:-1, :, 1]], axis=0)
    v = jnp.concatenate([vf, vb], axis=0)
    v = v.transpose(2, 3, 0, 5, 1, 4).reshape(v.shape[2], S5_GROUPS, 4 * S5_STATE, t * S5_GROUP)
    atr, ati = pr[t], pi[t]
    row = lambda re_part, im_part: jnp.concatenate([re_part, im_part], axis=-1).reshape(nd, S5_GROUPS * LANES)
    coef = jnp.stack([row(atr[:, 0], atr[:, 0]), row(-ati[:, 0], ati[:, 0]),
                      row(atr[:, 1], atr[:, 1]), row(-ati[:, 1], ati[:, 1])], axis=1)
    return m.astype(BF16), w.astype(BF16), v.astype(BF16), coef


def _mix_kernel(x_ref, mod_ref, a_ref, yf_ref, yb_ref, z_ref, gs_ref, y5_ref, u5_ref, d5_ref, wg_ref, bg_ref,
                f_ref, wo_ref, g2_ref, wr_ref, br_ref, o_ref, r_ref):
    mod = mod_ref[0]
    d = x_ref.shape[1]
    ssd = _rms((yf_ref[0] + yb_ref[0]) * _silu(z_ref[0]), SSD_INNER) * gs_ref[...]
    y5 = y5_ref[0] + d5_ref[...] * u5_ref[0]
    g5 = jax.nn.gelu(y5)
    s5 = g5 * _sigmoid(_dot(g5.astype(BF16), wg_ref[...]) + bg_ref[...])
    acc = _dot(a_ref[0].astype(BF16), wo_ref[0:256, :])
    acc += _dot(ssd.astype(BF16), wo_ref[256:512, :])
    acc += _dot(s5.astype(BF16), wo_ref[512:768, :])
    acc += _dot(f_ref[0].astype(BF16), wo_ref[768:1024, :])
    xn = x_ref[...] + mod[2:3] * acc
    h = _rms(xn, d) * g2_ref[...]
    h = h * (1.0 + mod[4:5]) + mod[3:4]
    route = _route(h, wr_ref, br_ref)
    o_ref[0, :, :d] = xn
    o_ref[0, :, d:] = route
    r_ref[0] = route


def _mix(x, layer, modtab, a, yf, yb, ssd_raw, gs, y5, u5, d5, wglu, bglu, f, wo, g2, wr, br):
    (b, l, _), d = a.shape, x.shape[1]
    nt = l // TOK_TILE
    tok = lambda c: pl.BlockSpec((1, TOK_TILE, c), lambda i, j: (i, j, 0))
    full = lambda arr: pl.BlockSpec(arr.shape, lambda i, j: (0,) * arr.ndim)
    return pl.pallas_call(
        _mix_kernel,
        grid=(b, nt),
        in_specs=[pl.BlockSpec((TOK_TILE, d), lambda i, j: (i * nt + j, 0)),
                  pl.BlockSpec((1, 8, d), lambda i, j: (2 * i + jnp.minimum(j, 1), 0, 0)),
                  tok(256), tok(256), tok(256), tok(256), full(gs), tok(256), tok(256), full(d5), full(wglu),
                  full(bglu), tok(256), pl.BlockSpec((None,) + wo.shape[1:], lambda i, j: (layer, 0, 0)),
                  full(g2), full(wr), full(br)],
        out_specs=[tok(d + LANES), tok(LANES)],
        out_shape=[jax.ShapeDtypeStruct((b, l, d + LANES), F32), jax.ShapeDtypeStruct((b, l, LANES), F32)],
        compiler_params=_cparams(("parallel", "arbitrary")),
        name="mix_out",
    )(x, modtab, a, yf, yb, ssd_raw, gs, y5, u5, d5, wglu, bglu, f, wo, g2, wr, br)


def _route(h, wr_ref, br_ref):
    lane = lax.broadcasted_iota(jnp.int32, (1, LANES), 1)
    lanef = lane.astype(F32)
    logits = _dot_3pass(h, wr_ref[...]) + br_ref[...]
    neg = -jnp.inf
    big = float(4 * LANES)
    first = lambda hit: jnp.min(jnp.where(hit, lanef, big), axis=-1, keepdims=True)
    glog = jnp.where(lane < MOE_GROUPS, logits, neg)
    gmax = jnp.max(glog, axis=-1, keepdims=True)
    gidx = first(glog == gmax)
    p_group = 1.0 / jnp.sum(jnp.exp(glog - gmax), axis=-1, keepdims=True)
    e0 = MOE_GROUPS + gidx * MOE_PER_GROUP
    elog = jnp.where((lanef >= e0) & (lanef < e0 + MOE_PER_GROUP), logits, neg)
    v1 = jnp.max(elog, axis=-1, keepdims=True)
    i1 = first(elog == v1)
    elog2 = jnp.where(lanef == i1, neg, elog)
    v2 = jnp.max(elog2, axis=-1, keepdims=True)
    i2 = first(elog2 == v2)
    e2 = jnp.exp(v2 - v1)
    w1 = p_group / (1.0 + e2)
    w2 = p_group * e2 / (1.0 + e2)
    return (jnp.where(lanef == i1 - e0, w1, 0.0) + jnp.where(lanef == i2 - e0, w2, 0.0)
            + jnp.where(lane == MOE_PER_GROUP, gidx, 0.0))


def _route_plan(gid, tm, nt):
    t = gid.shape[0]
    onehot = (gid[:, None] == jnp.arange(MOE_GROUPS, dtype=jnp.int32)[None]).astype(jnp.int32)
    csum = jnp.cumsum(onehot, axis=0)
    rank = jnp.sum((csum - onehot) * onehot, axis=1)
    padded = (csum[-1] + tm - 1) // tm * tm
    ends = jnp.cumsum(padded)
    slot = jnp.sum(onehot * (ends - padded)[None], axis=1) + rank
    src = jnp.full((nt * tm,), -1, jnp.int32).at[slot].set(jnp.arange(t, dtype=jnp.int32))
    pos = jnp.arange(nt * tm, dtype=jnp.int32)
    spare = t + (pos // tm % 2) * tm + pos % tm
    dst = jnp.concatenate([t + tm + pos[:tm], jnp.where(src >= 0, src, spare)])
    starts = jnp.arange(nt, dtype=jnp.int32) * tm
    tile_group = jnp.minimum(jnp.sum((starts[:, None] >= ends[None]).astype(jnp.int32), axis=1), MOE_GROUPS - 1)
    return src, dst, tile_group


def _moe_kernel(src_s, dst_s, tg_s, srcv_ref, xg_hbm, mod_ref, g2_ref, wg_ref, wu_ref, wd_ref, out_hbm,
                gbuf0, gbuf1, ybuf0, ybuf1, gsem, ssem, *, seq_len, n_ctx):
    del tg_s
    i = pl.program_id(0)
    nt = pl.num_programs(0)
    gbufs, ybufs = (gbuf0, gbuf1), (ybuf0, ybuf1)
    tm, d = ybuf0.shape
    nxt = jnp.minimum(i + 1, nt - 1)

    def gather_copy(tile, sl, r):
        t = jnp.maximum(src_s[tile * tm + r], 0)
        return pltpu.make_async_copy(xg_hbm.at[pl.ds(t, 1)], gbufs[sl].at[pl.ds(r, 1)], gsem.at[sl])

    def scatter_copy(tile, sl, r):
        t = dst_s[(tile + 1) * tm + r]
        return pltpu.make_async_copy(ybufs[sl].at[pl.ds(r, 1)], out_hbm.at[pl.ds(t, 1)], ssem.at[sl])

    def each_row(make_copy, tile, sl, wait):
        def body(r, c):
            cp = make_copy(tile, sl, r)
            cp.wait() if wait else cp.start()
            return c
        lax.fori_loop(0, tm, body, 0, unroll=8)

    @pl.when(i == 0)
    def _():
        each_row(gather_copy, 0, 0, False)
        n_tok = out_hbm.shape[0] - 2 * tm
        for sl in range(2):
            ybufs[sl][...] = jnp.zeros((tm, d), F32)
        fills = [pltpu.make_async_copy(ybufs[sl], out_hbm.at[pl.ds(n_tok + sl * tm, tm)], ssem.at[sl])
                 for sl in range(2)]
        for cp in fills:
            cp.start()
        for cp in fills:
            cp.wait()

    def modrow(tok, r):
        out = jnp.zeros((tm, d), F32)
        for sg in range(mod_ref.shape[0]):
            lo_t = (sg // 2) * seq_len + (0 if sg % 2 == 0 else n_ctx)
            hi_t = (sg // 2) * seq_len + (n_ctx if sg % 2 == 0 else seq_len)
            out = jnp.where((tok >= lo_t) & (tok < hi_t), mod_ref[sg, r:r + 1, :], out)
        return out

    def tile_step(sl):
        ot = 1 - sl
        each_row(gather_copy, i, sl, True)

        @pl.when(i >= 1)
        def _():
            each_row(scatter_copy, i - 2, sl, True)

        has_tokens = src_s[i * tm] >= 0

        @pl.when(has_tokens)
        def _():
            x = gbufs[sl][:, :d]
            gates = gbufs[sl][:, d:]
            tok = jnp.maximum(srcv_ref[...], 0)
            h = _rms(x, d) * g2_ref[...]
            h = (h * (1.0 + modrow(tok, 4)) + modrow(tok, 3)).astype(BF16)
            y = jnp.zeros((tm, d), F32)
            rows_per_expert = tm // MOE_PER_GROUP
            for e in range(MOE_PER_GROUP):
                for r in range(e * rows_per_expert, (e + 1) * rows_per_expert):
                    gather_copy(nxt, ot, r).start()
                    scatter_copy(i - 1, ot, r).start()
                hid = _silu(_dot(h, wg_ref[e])) * _dot(h, wu_ref[e]) * gates[:, e:e + 1]
                y += _dot(hid.astype(BF16), wd_ref[e])
            ybufs[sl][...] = x + modrow(tok, 5) * y

        @pl.when(jnp.logical_not(has_tokens))
        def _():
            each_row(gather_copy, nxt, ot, False)
            each_row(scatter_copy, i - 1, ot, False)

        @pl.when(i == nt - 1)
        def _():
            each_row(scatter_copy, i, sl, False)
            each_row(scatter_copy, i - 1, ot, True)
            each_row(scatter_copy, i, sl, True)
            each_row(gather_copy, nxt, ot, True)

    for sl in range(2):
        pl.when(i % 2 == sl)(functools.partial(tile_step, sl))


def _moe(xg, src, dst, tile_group, modtab, g2, wg, wu, wd, layer, seq_len, n_ctx):
    t, wx = xg.shape
    d = wx - LANES
    tm = MOE_TILE
    nt = src.shape[0] // tm
    full = lambda arr: pl.BlockSpec(arr.shape, lambda i, s, ds, g: (0,) * arr.ndim)
    wspec = lambda arr: pl.BlockSpec((None,) + arr.shape[1:], lambda i, s, ds, g: (layer * MOE_GROUPS + g[i], 0, 0, 0))
    grid_spec = pltpu.PrefetchScalarGridSpec(
        num_scalar_prefetch=3,
        grid=(nt,),
        in_specs=[pl.BlockSpec((tm, 1), lambda i, s, ds, g: (i, 0)),
                  pl.BlockSpec(memory_space=pl.ANY),
                  full(modtab), full(g2), wspec(wg), wspec(wu), wspec(wd)],
        out_specs=pl.BlockSpec(memory_space=pl.ANY),
        scratch_shapes=[pltpu.VMEM((tm, wx), F32), pltpu.VMEM((tm, wx), F32),
                        pltpu.VMEM((tm, d), F32), pltpu.VMEM((tm, d), F32),
                        pltpu.SemaphoreType.DMA((2,)), pltpu.SemaphoreType.DMA((2,))])
    return pl.pallas_call(
        functools.partial(_moe_kernel, seq_len=seq_len, n_ctx=n_ctx),
        grid_spec=grid_spec,
        out_shape=jax.ShapeDtypeStruct((t + 2 * tm, d), F32),
        compiler_params=_cparams(("arbitrary",)),
        name="moe",
    )(src, dst, tile_group, src.reshape(-1, 1), xg, modtab, g2, wg, wu, wd)


def _pad_in_proj(w_in):
    depth, d, _ = w_in.shape
    pieces = [(MLA_QK, LANES - MLA_QK)] * MLA_HEADS
    pieces += [(MLA_KV_RANK, MLA_NOPE), (MLA_ROPE, LANES - MLA_QK)]
    pieces += [(SSD_COLS, P_S5 - P_SSD - SSD_COLS), (S5_COLS, 0), (512, 0)]
    pieces += [(DIFF_V, LANES - DIFF_V)] * DIFF_HEADS
    cols, o = [], 0
    for width, gap in pieces:
        cols.append(w_in[:, :, o:o + width])
        o += width
        if gap:
            cols.append(jnp.zeros((depth, d, gap), w_in.dtype))
    out = jnp.concatenate(cols, axis=-1)
    assert o == w_in.shape[2] and out.shape[2] == P_COLS
    return out


def _rope_tables(seq, n_ctx):
    pos = jnp.arange(seq, dtype=jnp.int32)
    rows = (pos // GRID_W).astype(F32)
    cols = (pos % GRID_W).astype(F32)
    half = MLA_ROPE // 2
    inv = ROPE_BASE ** (-jnp.arange(0, half, 2, dtype=F32) / half)
    r = jnp.arange(MLA_ROPE)
    freq = inv[r % (half // 2)]
    ang = jnp.where((r < half)[None, :], rows[:, None], cols[:, None]) * freq[None, :]
    cos, sin = jnp.cos(ang), jnp.sin(ang)
    first = ((r % half) < half // 2)[None, :]
    up = jnp.where(first, -sin, 0.0)
    dn = jnp.where(first, 0.0, sin)
    t32 = jnp.stack([cos, up, dn])
    ident = jnp.stack([jnp.ones((n_ctx, MLA_ROPE), F32), jnp.zeros((n_ctx, MLA_ROPE), F32),
                       jnp.zeros((n_ctx, MLA_ROPE), F32)])
    t32 = jnp.concatenate([ident, t32], axis=1)
    n = t32.shape[1]
    base = jnp.stack([jnp.ones((n, LANES), F32), jnp.zeros((n, LANES), F32), jnp.zeros((n, LANES), F32)])
    rope_m = base.at[:, :, MLA_NOPE:MLA_QK].set(t32)
    rope_d = jnp.tile(t32, (1, 1, LANES // MLA_ROPE))
    return rope_m, rope_d


def _pad_lanes(v, n):
    return jnp.zeros((1, n), F32).at[0, :v.shape[0]].set(v)


def kernel(x, c, ctx, c_ctx, w_ada, b_ada, norm1, norm2, w_in, w_out, mla_kv_norm, mla_w_uk, mla_w_uv, mla_q_norm, mla_k_norm, ssd_conv_w, ssd_conv_b, ssd_a_log, ssd_dt_bias, ssd_d, ssd_norm, s5_lam_re, s5_lam_im, s5_log_step, s5_b_re, s5_b_im, s5_c_re, s5_c_im, s5_d, s5_w_glu, s5_b_glu, diff_q_norm, diff_k_norm, diff_lq1, diff_lk1, diff_lq2, diff_lk2, diff_subln, moe_w_group, moe_b_group, moe_w_expert, moe_b_expert, moe_w_gate, moe_w_up, moe_w_down):
    bsz, seq, d = x.shape
    n_ctx = ctx.shape[1]
    depth = w_in.shape[0]
    assert n_ctx == TOK_TILE and seq % FLASH_KV_CHUNK == 0 and (n_ctx + seq) % FLASH_Q_TILE == 0 and d == 1024
    assert FLASH_Q_TILE >= n_ctx
    l = n_ctx + seq
    assert (bsz * l) % MOE_TILE == 0 and bsz * l >= 2 * MOE_TILE and l % SSD_CHUNK == 0 and n_ctx % SSD_CHUNK == 0

    cvec = jnp.zeros((8, d), F32).at[:bsz].set(c).at[bsz].set(c_ctx)
    ada = _ada(cvec, w_ada, b_ada).reshape(depth, 8, N_MOD, d)
    mod_lat = ada[:, :bsz]
    mod_ctx = jnp.broadcast_to(ada[:, bsz][:, None], mod_lat.shape)
    modtab = jnp.stack([mod_ctx, mod_lat], axis=2)
    modtab = jnp.pad(modtab, ((0, 0), (0, 0), (0, 0), (0, 8 - N_MOD), (0, 0))).reshape(depth, bsz * 2, 8, d)

    w_in_p = _pad_in_proj(w_in).astype(BF16)
    w_out_b = w_out.astype(BF16)
    rope_m, rope_d = _rope_tables(seq, n_ctx)
    wuk = jnp.zeros((depth, MLA_KV_RANK, MLA_HEADS * LANES), F32)
    for hd in range(MLA_HEADS):
        wuk = wuk.at[:, :, hd * LANES:hd * LANES + MLA_NOPE].set(mla_w_uk[:, :, hd * MLA_NOPE:(hd + 1) * MLA_NOPE])
    wuk = wuk.astype(BF16)
    wuv = jnp.zeros((depth, MLA_KV_RANK, MLA_HEADS * LANES), F32)
    for hd in range(MLA_HEADS):
        wuv = wuv.at[:, :, hd * LANES:hd * LANES + MLA_V].set(mla_w_uv[:, :, hd * MLA_V:(hd + 1) * MLA_V])
    wuv = wuv.astype(BF16)
    amax = lambda g: jnp.max(jnp.abs(g.astype(F32)), axis=-1)
    bounded_m = (math.sqrt(MLA_QK) * amax(mla_q_norm) * amax(mla_k_norm) <= SCORE_BOUND).astype(F32)
    bounded_d = (math.sqrt(DIFF_HEAD) * amax(diff_q_norm) * amax(diff_k_norm) <= SCORE_BOUND).astype(F32)
    lane = jnp.arange(LANES)
    seg = (lane[:, None] // DIFF_HEAD == lane[None, :] // DIFF_HEAD).astype(F32)
    a_vec = -jnp.exp(ssd_a_log.astype(F32)).reshape(depth, 2 * SSD_HEADS)
    s5_m, s5_w, s5_v, s5_coef = _s5_matrices(
        s5_lam_re.astype(F32), s5_lam_im.astype(F32), s5_log_step.astype(F32),
        s5_b_re.astype(F32), s5_b_im.astype(F32), s5_c_re.astype(F32), s5_c_im.astype(F32))
    w_router = jnp.concatenate([moe_w_group, moe_w_expert.transpose(0, 2, 1, 3).reshape(depth, d, N_EXPERTS)], axis=-1)
    w_router = jnp.pad(w_router, ((0, 0), (0, 0), (0, LANES - w_router.shape[-1])))
    b_router = jnp.concatenate([moe_b_group, moe_b_expert.reshape(depth, N_EXPERTS)], axis=-1)
    b_router = jnp.pad(b_router, ((0, 0), (0, LANES - b_router.shape[-1])))
    by_group = lambda w: w.astype(BF16).reshape((depth * MOE_GROUPS, MOE_PER_GROUP) + w.shape[2:])
    wg_b, wu_b, wd_b = by_group(moe_w_gate), by_group(moe_w_up), by_group(moe_w_down)
    lam_all = (jnp.exp(jnp.sum(diff_lq1.astype(F32) * diff_lk1.astype(F32), axis=-1))
               - jnp.exp(jnp.sum(diff_lq2.astype(F32) * diff_lk2.astype(F32), axis=-1)))

    n5 = l // S5_CHUNK
    xs = jnp.concatenate([ctx, x], axis=1).reshape(bsz * l, d)
    for i in range(depth):
        lam_init = 0.8 - 0.6 * math.exp(-0.3 * i)
        qm, km, vm, qd, kd, vd, ssd_raw, u5 = _front(
            xs, bsz, i, modtab[i], norm1[i][None], w_in_p, rope_m, rope_d,
            _pad_lanes(mla_q_norm[i], LANES), _pad_lanes(mla_k_norm[i], LANES), mla_kv_norm[i][None],
            wuk[i], wuv[i], jnp.tile(diff_q_norm[i], 4)[None], jnp.tile(diff_k_norm[i], 4)[None], seg)
        a = _flash(jnp.stack([bounded_m[i], jnp.zeros((), F32)]), qm, km, vm, jnp.ones((1, LANES), F32),
                   n_sub=1, n_ctx=n_ctx, post=1.0)
        f = _flash(jnp.stack([bounded_d[i], lam_all[i] + lam_init]), qd, kd, vd, jnp.tile(diff_subln[i], 2)[None],
                   n_sub=2, n_ctx=n_ctx, post=1.0 - lam_init)
        yf, yb = _ssd(ssd_raw, ssd_conv_w[i], ssd_conv_b[i][None],
                      _pad_lanes(ssd_dt_bias[i].reshape(-1), LANES), _pad_lanes(a_vec[i], LANES),
                      jnp.repeat(ssd_d[i], SSD_HEAD_DIM)[None], n_ctx)
        u5g = u5.reshape(bsz, n5, S5_CHUNK, S5_GROUPS, S5_GROUP).transpose(3, 0, 1, 2, 4)
        u5g = u5g.reshape(S5_GROUPS, bsz * n5, S5_CHUNK * S5_GROUP)
        y1, end_f, end_b = _s5_local(u5g, s5_m, s5_w, i)
        prev_f, prev_b = _s5_scan(end_f, end_b, s5_coef[i], bsz, n_ctx // S5_CHUNK)
        y5 = _s5_read(y1, prev_f, prev_b, s5_v, i)
        y5 = y5.reshape(S5_GROUPS, bsz, n5, S5_CHUNK, S5_GROUP).transpose(1, 2, 3, 0, 4).reshape(bsz, l, S5_COLS)
        xg, ginfo = _mix(xs, i, modtab[i], a, yf, yb, ssd_raw, ssd_norm[i][None], y5, u5, s5_d[i][None],
                  s5_w_glu[i].astype(BF16), s5_b_glu[i][None], f, w_out_b,
                  norm2[i][None], w_router[i], b_router[i][None])
        xg = xg.reshape(bsz * l, d + LANES)
        gid = ginfo.reshape(bsz * l, LANES)[:, MOE_PER_GROUP].astype(jnp.int32)
        src, dst, tile_group = _route_plan(gid, MOE_TILE, bsz * l // MOE_TILE + MOE_GROUPS)
        xs = _moe(xg, src, dst, tile_group, modtab[i], norm2[i][None], wg_b, wu_b, wd_b, i, l, n_ctx)
    return xs[:bsz * l].reshape(bsz, l, d)[:, n_ctx:]
```
